```python
import math
import jax, jax.numpy as jnp
from jax import lax
import numpy as np

D_MODEL = 1024
BATCH = 4
SEQ = 4096
DEPTH = 1

GRID_W = 64
CTX_LEN = 256
N_HEADS = 8
HEAD_DIM = 64
V_DIM = 2 * HEAD_DIM
QK_W = N_HEADS * 2 * HEAD_DIM
ATTN_W = N_HEADS * V_DIM
S5_W = 512
S5_GROUP = 16
S5_GROUPS = S5_W // S5_GROUP
S5_STATE = 64
D_FF = 2816
CONV_W = 3
Q_BLOCK = 128
ROPE_THETA = 10000.0
EPS = 1e-6
MIN_NEG_RE = -1e-4
Q_OFF = 0
K_OFF = QK_W
V_OFF = 2 * QK_W
U_OFF = 2 * QK_W + ATTN_W
G_OFF = U_OFF + S5_W
N_IN = G_OFF + 2 * D_MODEL

kernel_name = 'hybrid_s5_diffattn_convffn_block'


def rms_norm(x, w):
    xf = x.astype(jnp.float32)
    y = xf * lax.rsqrt(jnp.mean(xf * xf, axis=-1, keepdims=True) + EPS)
    return y.astype(x.dtype) * w


def modulate(x, w, shift, scale):
    return rms_norm(x, w) * (1 + scale) + shift


def axial_rope_tables(rows):
    row = jnp.repeat(jnp.arange(rows), GRID_W).astype(jnp.float32)
    col = jnp.tile(jnp.arange(GRID_W), rows).astype(jnp.float32)
    half = HEAD_DIM // 2
    inv_freq = ROPE_THETA ** (-jnp.arange(0, half, 2, dtype=jnp.float32) / half)
    ang_r = row[:, None] * inv_freq
    ang_c = col[:, None] * inv_freq
    shp = (rows * GRID_W, 1, 1, half // 2)
    return (jnp.cos(ang_r).reshape(shp), jnp.sin(ang_r).reshape(shp),
            jnp.cos(ang_c).reshape(shp), jnp.sin(ang_c).reshape(shp))


def rope_1d(x, cos, sin):
    x1, x2 = jnp.split(x, 2, axis=-1)
    return jnp.concatenate([x1 * cos - x2 * sin, x2 * cos + x1 * sin], axis=-1)


def rope_2d(x, tabs):
    cos_r, sin_r, cos_c, sin_c = tabs
    x_row, x_col = jnp.split(x, 2, axis=-1)
    out = jnp.concatenate([rope_1d(x_row, cos_r, sin_r), rope_1d(x_col, cos_c, sin_c)], axis=-1)
    return out.astype(x.dtype)


def diff_attention(q, k, v, lam):
    s = jnp.einsum('bqhmd,bkhmd->bhmqk', q, k, preferred_element_type=jnp.float32) * (HEAD_DIM ** -0.5)
    p = jax.nn.softmax(s, axis=-1)
    a = p[:, :, 0] - lam.astype(jnp.float32) * p[:, :, 1]
    return jnp.einsum('bhqk,bkhe->bqhe', a.astype(v.dtype), v)


def blocked_latent_diff_attention(q, k_all, v_all, lam):
    b, l, h, m, dh = q.shape
    nb = l // Q_BLOCK
    qb = jnp.moveaxis(q.reshape(b, nb, Q_BLOCK, h, m, dh), 1, 0)
    ob = lax.map(lambda qi: diff_attention(qi, k_all, v_all, lam), qb)
    return jnp.moveaxis(ob, 0, 1).reshape(b, l, h, V_DIM)


def diff_head_out(o, subln_w, lam_init, w_branch):
    o = rms_norm(o, subln_w) * (1.0 - lam_init)
    return o.reshape(o.shape[0], o.shape[1], ATTN_W) @ w_branch


def s5_discretise(a_re, a_im, log_dt, b_re, b_im):
    lam = lax.complex(jnp.minimum(a_re.astype(jnp.float32), MIN_NEG_RE), a_im.astype(jnp.float32))
    dt = jnp.exp(log_dt.astype(jnp.float32))[:, None]
    a_bar = jnp.exp(lam * dt)
    b = lax.complex(b_re.astype(jnp.float32), b_im.astype(jnp.float32))
    b_bar = ((a_bar - 1.0) / lam)[..., None] * b
    return a_bar, b_bar


def ssm_combine(e1, e2):
    a1, b1 = e1
    a2, b2 = e2
    return a1 * a2, a2 * b1 + b2


def ssm_scan(bu, a_bar, s0, reverse):
    if s0 is not None:
        first = -1 if reverse else 0
        bu = bu.at[:, first].add(a_bar * s0)
    a = jnp.broadcast_to(a_bar, bu.shape)
    _, states = lax.associative_scan(ssm_combine, (a, bu), axis=1, reverse=reverse)
    return states


def s5_glu_proj(y, glu_w, glu_b, w_branch):
    h = jax.nn.gelu(y)
    return (h * jax.nn.sigmoid(h @ glu_w + glu_b)) @ w_branch


def s5_branch(u, uc, p, need_ctx):
    b, l, _ = u.shape
    n = uc.shape[1]
    ug = u.astype(jnp.float32).reshape(b, l, S5_GROUPS, S5_GROUP)
    ucg = uc.astype(jnp.float32).reshape(b, n, S5_GROUPS, S5_GROUP)
    d_skip = p['s5_d'].astype(jnp.float32)
    y = d_skip * u.astype(jnp.float32)
    yc = d_skip * uc.astype(jnp.float32) if need_ctx else None
    for d in range(2):
        rev = d == 1
        a_bar, b_bar = s5_discretise(p['s5_a_re'][d], p['s5_a_im'][d], p['s5_log_dt'][d],
                                     p['s5_b_re'][d], p['s5_b_im'][d])
        c_mat = lax.complex(p['s5_c_re'][d].astype(jnp.float32), p['s5_c_im'][d].astype(jnp.float32))
        s_ctx = ssm_scan(jnp.einsum('bngc,gpc->bngp', ucg, b_bar), a_bar, None, rev)
        s0 = s_ctx[:, 0] if rev else s_ctx[:, -1]
        s_lat = ssm_scan(jnp.einsum('blgc,gpc->blgp', ug, b_bar), a_bar, s0, rev)
        y = y + jnp.einsum('blgp,gcp->blgc', s_lat, c_mat).real.reshape(b, l, S5_W)
        if need_ctx:
            yc = yc + jnp.einsum('bngp,gcp->bngc', s_ctx, c_mat).real.reshape(b, n, S5_W)
    ys = s5_glu_proj(y.astype(u.dtype), p['glu_w'], p['glu_b'], p['w_branch_s5'])
    ysc = s5_glu_proj(yc.astype(uc.dtype), p['glu_w'], p['glu_b'], p['w_branch_s5']) if need_ctx else None
    return ys, ysc


def merge_branches(g, y_s5, y_attn, b_gate, w_out):
    g_s, g_a = jnp.split(g + b_gate, 2, axis=-1)
    return (jax.nn.sigmoid(g_s) * y_s5 + jax.nn.sigmoid(g_a) * y_attn) @ w_out


def conv_ffn(h, w_up, conv_w, conv_b, w_down):
    u = h @ w_up
    n = u.shape[1]
    pad = CONV_W // 2
    up = jnp.pad(u, ((0, 0), (pad, pad), (0, 0)))
    y = conv_b
    for j in range(CONV_W):
        y = y + up[:, j:j + n] * conv_w[j]
    a, g = jnp.split(y, 2, axis=-1)
    return (jax.nn.silu(g) * a) @ w_down


def hybrid_layer(x, xc, mod_x, mod_c, rope, p, lam_init, update_ctx):
    b, l, _ = x.shape
    n = xc.shape[1]
    sh1, sc1, ga1, sh2, sc2, ga2 = jnp.split(mod_x, 6, axis=-1)
    csh1, csc1, cga1, csh2, csc2, cga2 = jnp.split(mod_c, 6, axis=-1)
    w_in = p['w_in']
    z = modulate(x, p['norm1_w'], sh1, sc1) @ w_in
    q = rms_norm(z[..., Q_OFF:K_OFF].reshape(b, l, N_HEADS, 2, HEAD_DIM), p['q_norm_w'])
    k = rms_norm(z[..., K_OFF:V_OFF].reshape(b, l, N_HEADS, 2, HEAD_DIM), p['k_norm_w'])
    v = z[..., V_OFF:U_OFF].reshape(b, l, N_HEADS, V_DIM)
    u = z[..., U_OFF:G_OFF]
    g = z[..., G_OFF:]
    q = rope_2d(q, rope)
    k = rope_2d(k, rope)
    hc = modulate(xc, p['norm1_w'], csh1, csc1)
    zc = hc @ w_in[:, K_OFF:G_OFF]
    kc = rms_norm(zc[..., :QK_W].reshape(b, n, N_HEADS, 2, HEAD_DIM), p['k_norm_w'])
    vc = zc[..., QK_W:QK_W + ATTN_W].reshape(b, n, N_HEADS, V_DIM)
    uc = zc[..., QK_W + ATTN_W:]
    lam = (jnp.exp(jnp.sum((p['lam_q1'] * p['lam_k1']).astype(jnp.float32)))
           - jnp.exp(jnp.sum((p['lam_q2'] * p['lam_k2']).astype(jnp.float32))) + lam_init)
    k_all = jnp.concatenate([kc, k], axis=1)
    v_all = jnp.concatenate([vc, v], axis=1)
    y_attn = diff_head_out(blocked_latent_diff_attention(q, k_all, v_all, lam),
                           p['subln_w'], lam_init, p['w_branch_attn'])
    y_s5, y_s5_c = s5_branch(u, uc, p, update_ctx)
    x = x + ga1 * merge_branches(g, y_s5, y_attn, p['b_gate'], p['w_out'])
    x = x + ga2 * conv_ffn(modulate(x, p['norm2_w'], sh2, sc2), p['w_up'], p['conv_w'], p['conv_b'], p['w_down'])
    if update_ctx:
        qc = rms_norm((hc @ w_in[:, Q_OFF:K_OFF]).reshape(b, n, N_HEADS, 2, HEAD_DIM), p['q_norm_w'])
        gc = hc @ w_in[:, G_OFF:]
        yc_attn = diff_head_out(diff_attention(qc, kc, vc, lam), p['subln_w'], lam_init, p['w_branch_attn'])
        xc = xc + cga1 * merge_branches(gc, y_s5_c, yc_attn, p['b_gate'], p['w_out'])
        xc = xc + cga2 * conv_ffn(modulate(xc, p['norm2_w'], csh2, csc2), p['w_up'], p['conv_w'], p['conv_b'], p['w_down'])
    return x, xc


def setup_inputs(seed: int = 0) -> dict:
    key = jax.random.key(seed)
    ks = iter(jax.random.split(key, 40))

    def nrm(shape, s):
        return jax.random.normal(next(ks), shape, jnp.float32) * s

    G, P = S5_GROUPS, S5_STATE
    return {
        'x': nrm((BATCH, SEQ, D_MODEL), 1.0),
        'c': nrm((BATCH, D_MODEL), 1.0),
        'ctx': nrm((BATCH, CTX_LEN, D_MODEL), 1.0),
        'c_ctx': nrm((D_MODEL,), 1.0),
        'ada_w': nrm((DEPTH, D_MODEL, 6 * D_MODEL), 0.5 * D_MODEL ** -0.5),
        'ada_b': nrm((DEPTH, 6 * D_MODEL), 0.01),
        'norm1_w': 1.0 + nrm((DEPTH, D_MODEL), 0.01),
        'w_in': nrm((DEPTH, D_MODEL, N_IN), D_MODEL ** -0.5),
        'b_gate': nrm((DEPTH, 2 * D_MODEL), 0.01),
        'q_norm_w': 1.0 + nrm((DEPTH, HEAD_DIM), 0.01),
        'k_norm_w': 1.0 + nrm((DEPTH, HEAD_DIM), 0.01),
        'lam_q1': nrm((DEPTH, HEAD_DIM), 0.1),
        'lam_k1': nrm((DEPTH, HEAD_DIM), 0.1),
        'lam_q2': nrm((DEPTH, HEAD_DIM), 0.1),
        'lam_k2': nrm((DEPTH, HEAD_DIM), 0.1),
        'subln_w': 1.0 + nrm((DEPTH, V_DIM), 0.01),
        's5_a_re': -0.5 + nrm((DEPTH, 2, G, P), 0.01),
        's5_a_im': jnp.pi * jnp.arange(P, dtype=jnp.float32) + nrm((DEPTH, 2, G, P), 0.01),
        's5_log_dt': jax.random.uniform(next(ks), (DEPTH, 2, G), jnp.float32, math.log(1e-3), math.log(1e-1)),
        's5_b_re': nrm((DEPTH, 2, G, P, S5_GROUP), (2 * S5_GROUP) ** -0.5),
        's5_b_im': nrm((DEPTH, 2, G, P, S5_GROUP), (2 * S5_GROUP) ** -0.5),
        's5_c_re': nrm((DEPTH, 2, G, S5_GROUP, P), (2 * P) ** -0.5),
        's5_c_im': nrm((DEPTH, 2, G, S5_GROUP, P), (2 * P) ** -0.5),
        's5_d': nrm((DEPTH, S5_W), 1.0),
        'glu_w': nrm((DEPTH, S5_W, S5_W), S5_W ** -0.5),
        'glu_b': nrm((DEPTH, S5_W), 0.01),
        'w_branch_s5': nrm((DEPTH, S5_W, D_MODEL), S5_W ** -0.5),
        'w_branch_attn': nrm((DEPTH, ATTN_W, D_MODEL), ATTN_W ** -0.5),
        'w_out': nrm((DEPTH, D_MODEL, D_MODEL), D_MODEL ** -0.5),
        'norm2_w': 1.0 + nrm((DEPTH, D_MODEL), 0.01),
        'w_up': nrm((DEPTH, D_MODEL, 2 * D_FF), D_MODEL ** -0.5),
        'conv_w': nrm((DEPTH, CONV_W, 2 * D_FF), CONV_W ** -0.5),
        'conv_b': nrm((DEPTH, 2 * D_FF), 0.01),
        'w_down': nrm((DEPTH, D_FF, D_MODEL), D_FF ** -0.5),
    }


def reference(x, c, ctx, c_ctx, ada_w, ada_b, norm1_w, w_in, b_gate, q_norm_w, k_norm_w,
              lam_q1, lam_k1, lam_q2, lam_k2, subln_w, s5_a_re, s5_a_im, s5_log_dt,
              s5_b_re, s5_b_im, s5_c_re, s5_c_im, s5_d, glu_w, glu_b, w_branch_s5,
              w_branch_attn, w_out, norm2_w, w_up, conv_w, conv_b, w_down):
    rows = x.shape[1] // GRID_W
    rope = axial_rope_tables(rows)
    xc = ctx
    for li in range(DEPTH):
        p = {
            'norm1_w': norm1_w[li], 'w_in': w_in[li], 'b_gate': b_gate[li],
            'q_norm_w': q_norm_w[li], 'k_norm_w': k_norm_w[li],
            'lam_q1': lam_q1[li], 'lam_k1': lam_k1[li], 'lam_q2': lam_q2[li], 'lam_k2': lam_k2[li],
            'subln_w': subln_w[li],
            's5_a_re': s5_a_re[li], 's5_a_im': s5_a_im[li], 's5_log_dt': s5_log_dt[li],
            's5_b_re': s5_b_re[li], 's5_b_im': s5_b_im[li], 's5_c_re': s5_c_re[li], 's5_c_im': s5_c_im[li],
            's5_d': s5_d[li], 'glu_w': glu_w[li], 'glu_b': glu_b[li],
            'w_branch_s5': w_branch_s5[li], 'w_branch_attn': w_branch_attn[li], 'w_out': w_out[li],
            'norm2_w': norm2_w[li], 'w_up': w_up[li], 'conv_w': conv_w[li], 'conv_b': conv_b[li],
            'w_down': w_down[li],
        }
        mod_x = (jax.nn.silu(c) @ ada_w[li] + ada_b[li])[:, None, :]
        mod_c = (jax.nn.silu(c_ctx) @ ada_w[li] + ada_b[li])[None, None, :]
        lam_init = 0.8 - 0.6 * math.exp(-0.3 * li)
        x, xc = hybrid_layer(x, xc, mod_x, mod_c, rope, p, lam_init, li + 1 < DEPTH)
    return x
```

```python
import functools
import math

import jax
import jax.numpy as jnp
from jax import lax
from jax.experimental import pallas as pl
from jax.experimental.pallas import tpu as pltpu

D_MODEL = 1024
GRID_W = 64
N_HEADS = 8
HEAD_DIM = 64
V_DIM = 2 * HEAD_DIM
QK_W = N_HEADS * 2 * HEAD_DIM
ATTN_W = N_HEADS * V_DIM
S5_W = 512
S5_GROUP = 16
S5_GROUPS = S5_W // S5_GROUP
S5_STATE = 64
D_FF = 2816
ROPE_THETA = 10000.0
EPS = 1e-6
MIN_NEG_RE = -1e-4
K_OFF = QK_W
V_OFF = 2 * QK_W
U_OFF = 2 * QK_W + ATTN_W
G_OFF = U_OFF + S5_W
N_IN = G_OFF + 2 * D_MODEL
LAM_INIT = 0.8 - 0.6 * math.exp(-0.3 * 0)

LANES = 128
TOK_TILE = 256
CHUNK = 16
SLAB = LANES
N_SLABS = S5_W // SLAB
GROUPS_PER_SLAB = SLAB // S5_GROUP
SLAB_IN = CHUNK * SLAB
SLAB_STATE = GROUPS_PER_SLAB * 2 * S5_STATE
FF_CHUNK = 256
N_FF_CHUNKS = D_FF // FF_CHUNK
HALO = 8
VMEM_LIMIT = 56 * 1024 * 1024

F32 = jnp.float32
BF16 = jnp.bfloat16


def _sigmoid(x):
    return 1.0 / (1.0 + jnp.exp(-x))


def _dot(a, b):
    return jnp.dot(a, b, preferred_element_type=F32)


def _params(n_parallel, n_arbitrary=0):
    return pltpu.CompilerParams(
        dimension_semantics=("parallel",) * n_parallel + ("arbitrary",) * n_arbitrary,
        vmem_limit_bytes=VMEM_LIMIT)


def _mod_body(c_ref, w_ref, b_ref, o_ref):
    c = c_ref[...]
    s = c * _sigmoid(c)
    o_ref[...] = jnp.dot(s, w_ref[...], preferred_element_type=F32,
                         precision=lax.Precision.HIGHEST) + b_ref[...]


def _mod_call(cc, ada_w, ada_b):
    n = ada_w.shape[1]
    return pl.pallas_call(
        _mod_body,
        grid=(n // D_MODEL,),
        in_specs=[pl.BlockSpec((8, D_MODEL), lambda j: (0, 0)),
                  pl.BlockSpec((D_MODEL, D_MODEL), lambda j: (0, j)),
                  pl.BlockSpec((1, D_MODEL), lambda j: (0, j))],
        out_specs=pl.BlockSpec((8, D_MODEL), lambda j: (0, j)),
        out_shape=jax.ShapeDtypeStruct((8, n), F32),
        compiler_params=_params(1),
        name="mod",
    )(cc, ada_w, ada_b)


def _inproj_body(ctx_ref, x_ref, mod_ref, n1w_ref, w_ref, qw_ref, kw_ref, bg_ref, cos_ref, sin_ref,
                 seg_ref, qT_ref, k_ref, vT_ref, uc_ref, ul_ref, g_ref):
    i = pl.program_id(1)
    is_ctx = i == 0
    xt = jnp.where(is_ctx, ctx_ref[...], x_ref[...])
    ms = jnp.mean(xt * xt, axis=-1, keepdims=True)
    mod = mod_ref[...]
    h = (xt * lax.rsqrt(ms + EPS)) * n1w_ref[...] * (1.0 + mod[1:2]) + mod[0:1]
    hb = h.astype(BF16)

    cos = cos_ref[...]
    sin = sin_ref[...]
    seg = seg_ref[...]
    lane = lax.broadcasted_iota(jnp.int32, cos.shape, 1)
    first_half = (lane % 32) < 16

    def norm_rope(zh, w):
        sq = zh * zh
        hi = sq.astype(BF16)
        lo = (sq - hi.astype(F32)).astype(BF16)
        msq = _dot(hi, seg) + _dot(lo, seg)
        y = zh * lax.rsqrt(msq + EPS) * w
        partner = jnp.where(first_half, pltpu.roll(y, LANES - 16, 1), pltpu.roll(y, 16, 1))
        return y * cos + partner * sin

    zk = _dot(hb, w_ref[:, K_OFF:V_OFF])
    kw = kw_ref[...]
    for hh in range(N_HEADS):
        k_ref[hh] = norm_rope(zk[:, hh * LANES:(hh + 1) * LANES], kw).astype(BF16)

    zv = _dot(hb, w_ref[:, V_OFF:U_OFF])
    for hh in range(N_HEADS):
        vT_ref[hh] = zv[:, hh * LANES:(hh + 1) * LANES].T.astype(BF16)

    zu = _dot(hb, w_ref[:, U_OFF:G_OFF])

    @pl.when(is_ctx)
    def _():
        for s in range(N_SLABS):
            uc_ref[s] = zu[:, s * SLAB:(s + 1) * SLAB].astype(BF16)

    @pl.when(jnp.logical_not(is_ctx))
    def _():
        for s in range(N_SLABS):
            ul_ref[s] = zu[:, s * SLAB:(s + 1) * SLAB].astype(BF16)
        zq = _dot(hb, w_ref[:, 0:K_OFF])
        qw = qw_ref[...]
        row = lax.broadcasted_iota(jnp.int32, (LANES, TOK_TILE), 0)
        for hh in range(N_HEADS):
            q = norm_rope(zq[:, hh * LANES:(hh + 1) * LANES], qw) * (HEAD_DIM ** -0.5)
            qT = q.T
            qT_ref[hh, 0] = jnp.where(row < HEAD_DIM, qT, 0.0).astype(BF16)
            qT_ref[hh, 1] = jnp.where(row >= HEAD_DIM, qT, 0.0).astype(BF16)
        zg = _dot(hb, w_ref[:, G_OFF:N_IN]) + bg_ref[...]
        g_ref[...] = _sigmoid(zg)


def _inproj_call(ctx, x, mod3, n1w, w_in_b, qw, kw, bg, cos_t, sin_t, seg):
    B, L, D = x.shape
    n_ctx = ctx.shape[1]
    assert n_ctx == TOK_TILE and L % TOK_TILE == 0
    nt = L // TOK_TILE
    n_all = n_ctx + L
    T = TOK_TILE
    lat = lambda i: jnp.maximum(i - 1, 0)
    const2 = lambda b, i: (0, 0)
    return pl.pallas_call(
        _inproj_body,
        grid=(B, nt + 1),
        in_specs=[
            pl.BlockSpec((None, T, D), lambda b, i: (b, 0, 0)),
            pl.BlockSpec((None, T, D), lambda b, i: (b, lat(i), 0)),
            pl.BlockSpec((None, 6, D), lambda b, i: (jnp.where(i == 0, B, b), 0, 0)),
            pl.BlockSpec((1, D), const2),
            pl.BlockSpec((D, N_IN), const2),
            pl.BlockSpec((1, LANES), const2),
            pl.BlockSpec((1, LANES), const2),
            pl.BlockSpec((1, 2 * D), const2),
            pl.BlockSpec((T, LANES), lambda b, i: (i, 0)),
            pl.BlockSpec((T, LANES), lambda b, i: (i, 0)),
            pl.BlockSpec((LANES, LANES), const2),
        ],
        out_specs=[
            pl.BlockSpec((None, N_HEADS, 2, LANES, T), lambda b, i: (b, 0, 0, 0, lat(i))),
            pl.BlockSpec((None, N_HEADS, T, LANES), lambda b, i: (b, 0, i, 0)),
            pl.BlockSpec((None, N_HEADS, None, LANES, T), lambda b, i: (b, 0, i, 0, 0)),
            pl.BlockSpec((N_SLABS, None, T, SLAB), lambda b, i: (0, b, 0, 0)),
            pl.BlockSpec((N_SLABS, None, T, SLAB), lambda b, i: (0, b, lat(i), 0)),
            pl.BlockSpec((None, T, 2 * D), lambda b, i: (b, lat(i), 0)),
        ],
        out_shape=[
            jax.ShapeDtypeStruct((B, N_HEADS, 2, LANES, L), BF16),
            jax.ShapeDtypeStruct((B, N_HEADS, n_all, LANES), BF16),
            jax.ShapeDtypeStruct((B, N_HEADS, nt + 1, LANES, T), BF16),
            jax.ShapeDtypeStruct((N_SLABS, B, n_ctx, SLAB), BF16),
            jax.ShapeDtypeStruct((N_SLABS, B, L, SLAB), BF16),
            jax.ShapeDtypeStruct((B, L, 2 * D), F32),
        ],
        compiler_params=_params(1, 1),
        name="in_proj",
    )(ctx, x, mod3, n1w, w_in_b, qw, kw, bg, cos_t, sin_t, seg)


def _attn_body(qT_ref, k_ref, vT_ref, lq1_ref, lk1_ref, lq2_ref, lk2_ref, sw_ref, o_ref, *, n_chunks, tq):
    tk = TOK_TILE
    qT = jnp.concatenate([qT_ref[0], qT_ref[1]], axis=1)

    def step(j, carry):
        m, l, acc = carry
        kc = k_ref[pl.ds(pl.multiple_of(j * tk, tk), tk), :]
        s = _dot(kc, qT)
        m_new = jnp.maximum(m, jnp.max(s, axis=0, keepdims=True))
        alpha = jnp.exp(m - m_new)
        p = jnp.exp(s - m_new)
        l = alpha * l + jnp.sum(p, axis=0, keepdims=True)
        acc = alpha * acc + _dot(vT_ref[j], p.astype(BF16))
        return m_new, l, acc

    m0 = jnp.full((1, 2 * tq), -1e30, F32)
    l0 = jnp.zeros((1, 2 * tq), F32)
    acc0 = jnp.zeros((LANES, 2 * tq), F32)
    _, l, acc = lax.fori_loop(0, n_chunks, step, (m0, l0, acc0))
    oT = acc / l
    lam = (jnp.exp(jnp.sum(lq1_ref[...] * lk1_ref[...], axis=-1, keepdims=True))
           - jnp.exp(jnp.sum(lq2_ref[...] * lk2_ref[...], axis=-1, keepdims=True)) + LAM_INIT)
    od = (oT[:, :tq] - lam * oT[:, tq:]).T
    ms = jnp.mean(od * od, axis=-1, keepdims=True)
    o_ref[...] = (od * lax.rsqrt(ms + EPS)) * sw_ref[...] * (1.0 - LAM_INIT)


def _attn_call(qT, k, vT, lq1, lk1, lq2, lk2, sw, tq):
    B, H, _, _, L = qT.shape
    n_all = k.shape[2]
    n_chunks = vT.shape[2]
    vec = pl.BlockSpec((1, HEAD_DIM), lambda b, h, i: (0, 0))
    return pl.pallas_call(
        functools.partial(_attn_body, n_chunks=n_chunks, tq=tq),
        grid=(B, H, L // tq),
        in_specs=[
            pl.BlockSpec((None, None, 2, LANES, tq), lambda b, h, i: (b, h, 0, 0, i)),
            pl.BlockSpec((None, None, n_all, LANES), lambda b, h, i: (b, h, 0, 0)),
            pl.BlockSpec((None, None, n_chunks, LANES, TOK_TILE), lambda b, h, i: (b, h, 0, 0, 0)),
            vec, vec, vec, vec,
            pl.BlockSpec((1, V_DIM), lambda b, h, i: (0, 0)),
        ],
        out_specs=pl.BlockSpec((None, tq, V_DIM), lambda b, h, i: (b, i, h)),
        out_shape=jax.ShapeDtypeStruct((B, L, ATTN_W), F32),
        compiler_params=_params(3),
        name="attn",
    )(qT, k, vT, lq1, lk1, lq2, lk2, sw)


def _s5prep_body(are_ref, aim_ref, ldt_ref, btr_ref, bti_ref, cr_ref, ci_ref, dsk_ref,
                 k_ref, k0_ref, bz_ref, cc_ref, a16_ref):
    n = lax.broadcasted_iota(jnp.int32, (24, 1), 0).astype(F32)
    k0 = jnp.zeros((S5_GROUP, S5_GROUP), F32)
    for d in range(2):
        lr = jnp.minimum(are_ref[d:d + 1, :], MIN_NEG_RE)
        li = aim_ref[d:d + 1, :]
        dt = jnp.exp(ldt_ref[d:d + 1, :])
        mag1 = jnp.exp(lr * dt)
        ar = mag1 * jnp.cos(li * dt)
        ai = mag1 * jnp.sin(li * dt)
        den = lr * lr + li * li
        nr = ar - 1.0
        cfr = (nr * lr + ai * li) / den
        cfi = (ai * lr - nr * li) / den
        btr = btr_ref[d]
        bti = bti_ref[d]
        bbr = cfr * btr - cfi * bti
        bbi = cfr * bti + cfi * btr
        mag = jnp.exp(n * (lr * dt))
        ang = n * (li * dt)
        pr = mag * jnp.cos(ang)
        pim = mag * jnp.sin(ang)
        cr = cr_ref[d]
        ci = ci_ref[d]

        def c_pow(e):
            r_ = pr[e:e + 1]
            i_ = pim[e:e + 1]
            return jnp.concatenate([cr * r_ - ci * i_, -(cr * i_ + ci * r_)], axis=1)

        lhs = jnp.concatenate([c_pow(tau) for tau in range(CHUNK)], axis=0)
        rhs = jnp.concatenate([bbr, bbi], axis=1)
        kall = lax.dot_general(lhs, rhs, (((1,), (1,)), ((), ())), preferred_element_type=F32,
                               precision=lax.Precision.HIGHEST)
        k_ref[d] = kall
        k0 = k0 + kall[0:S5_GROUP]
        for t in range(CHUNK):
            e = CHUNK - 1 - t if d == 0 else t
            r_ = pr[e:e + 1]
            i_ = pim[e:e + 1]
            bz_ref[d, t] = jnp.concatenate([r_ * bbr - i_ * bbi, r_ * bbi + i_ * bbr], axis=1)
            cc_ref[d, t] = c_pow(t + 1 if d == 0 else CHUNK - t)
        a16_ref[d] = jnp.concatenate([pr[CHUNK:CHUNK + 1], pim[CHUNK:CHUNK + 1]], axis=1)
    r = lax.broadcasted_iota(jnp.int32, (S5_GROUP, S5_GROUP), 0)
    c = lax.broadcasted_iota(jnp.int32, (S5_GROUP, S5_GROUP), 1)
    k0_ref[...] = k0 + jnp.where(r == c, dsk_ref[...], 0.0)


def _s5prep_call(are, aim, ldt, btr, bti, cr, ci, dsk):
    G, P, C = S5_GROUPS, S5_STATE, S5_GROUP
    g3 = lambda g: (g, 0, 0)
    g4 = lambda g: (g, 0, 0, 0)
    g5 = lambda g: (g, 0, 0, 0, 0)
    return pl.pallas_call(
        _s5prep_body,
        grid=(G,),
        in_specs=[pl.BlockSpec((None, 2, P), g3), pl.BlockSpec((None, 2, P), g3),
                  pl.BlockSpec((None, 2, 1), g3),
                  pl.BlockSpec((None, 2, C, P), g4), pl.BlockSpec((None, 2, C, P), g4),
                  pl.BlockSpec((None, 2, C, P), g4), pl.BlockSpec((None, 2, C, P), g4),
                  pl.BlockSpec((None, C, 1), g3)],
        out_specs=[pl.BlockSpec((None, 2, CHUNK * C, C), g4),
                   pl.BlockSpec((None, C, C), g3),
                   pl.BlockSpec((None, 2, CHUNK, C, 2 * P), g5),
                   pl.BlockSpec((None, 2, CHUNK, C, 2 * P), g5),
                   pl.BlockSpec((None, 2, 1, 2 * P), g4)],
        out_shape=[jax.ShapeDtypeStruct((G, 2, CHUNK * C, C), F32),
                   jax.ShapeDtypeStruct((G, C, C), F32),
                   jax.ShapeDtypeStruct((G, 2, CHUNK, C, 2 * P), F32),
                   jax.ShapeDtypeStruct((G, 2, CHUNK, C, 2 * P), F32),
                   jax.ShapeDtypeStruct((G, 2, 1, 2 * P), F32)],
        compiler_params=_params(1),
        name="s5_prep",
    )(are, aim, ldt, btr, bti, cr, ci, dsk)


def _expand_s5_tables(kall, k0, bz, cc, a16):
    G, C, P, S, GS = S5_GROUPS, S5_GROUP, S5_STATE, N_SLABS, GROUPS_PER_SLAB
    eye = jnp.eye(GS, dtype=F32)
    kk = kall.reshape(G, 2, CHUNK, C, C)
    lags = jnp.concatenate([kk[:, 1, :0:-1], k0[:, None], kk[:, 0, 1:]], axis=1)
    tp = jnp.arange(CHUNK)
    idx = tp[None, :] - tp[:, None] + (CHUNK - 1)
    kg = lags[:, idx].reshape(S, GS, CHUNK, CHUNK, C, C)
    m = jnp.einsum('sgabcd,gh->sagdbhc', kg, eye).reshape(S, SLAB_IN, SLAB_IN).astype(BF16)
    bz6 = bz.reshape(S, GS, 2, CHUNK, C, 2, P)
    bzx = jnp.einsum('sgdtcrp,gh->dstgcrhp', bz6, eye).reshape(2, S, SLAB_IN, SLAB_STATE).astype(BF16)
    cc6 = cc.reshape(S, GS, 2, CHUNK, C, 2, P)
    ccx = jnp.einsum('sgdtcrp,gh->dsrhptgc', cc6, eye).reshape(2, S, SLAB_STATE, SLAB_IN).astype(BF16)
    a16x = a16.reshape(S, GS, 2, 2, P).transpose(2, 0, 3, 1, 4).reshape(2, S, 1, SLAB_STATE)
    return m, bzx, ccx, a16x


def _s5scan_body(uc_ref, ul_ref, bz_ref, a_ref, sin_ref, zc_ref, zl_ref, st_ref, *, n_cc, n_lc):
    d = pl.program_id(1)
    B = uc_ref.shape[0]
    w = bz_ref[...]
    zc_ref[...] = _dot(uc_ref[...].reshape(B * n_cc, SLAB_IN), w).reshape(B, n_cc, SLAB_STATE)
    zl_ref[...] = _dot(ul_ref[...].reshape(B * n_lc, SLAB_IN), w).reshape(B, n_lc, SLAB_STATE)
    a = a_ref[...]
    half = SLAB_STATE // 2
    a_same = jnp.concatenate([a[:, :half], a[:, :half]], axis=1)
    a_swap = jnp.concatenate([-a[:, half:], a[:, half:]], axis=1)

    def advance(s, z):
        return a_same * s + a_swap * pltpu.roll(s, half, 1) + z

    def ctx_step(i, s):
        kk = jnp.where(d == 0, i, n_cc - 1 - i)
        return tuple(advance(s[b], zc_ref[b, pl.ds(kk, 1), :]) for b in range(B))

    def lat_step(i, s):
        kk = jnp.where(d == 0, i, n_lc - 1 - i)
        for b in range(B):
            st_ref[b, pl.ds(kk, 1), :] = s[b]
        return tuple(advance(s[b], zl_ref[b, pl.ds(kk, 1), :]) for b in range(B))

    s = lax.fori_loop(0, n_cc, ctx_step, tuple(jnp.zeros((1, SLAB_STATE), F32) for _ in range(B)))
    lax.fori_loop(0, n_lc, lat_step, s)
    sin_ref[...] = st_ref[...].astype(BF16)


def _s5scan_call(uc16, ul16, bzx, a16x):
    S, B, n_cc, _ = uc16.shape
    n_lc = ul16.shape[2]
    return pl.pallas_call(
        functools.partial(_s5scan_body, n_cc=n_cc, n_lc=n_lc),
        grid=(S, 2),
        in_specs=[pl.BlockSpec((None, B, n_cc, SLAB_IN), lambda s, d: (s, 0, 0, 0)),
                  pl.BlockSpec((None, B, n_lc, SLAB_IN), lambda s, d: (s, 0, 0, 0)),
                  pl.BlockSpec((None, None, SLAB_IN, SLAB_STATE), lambda s, d: (d, s, 0, 0)),
                  pl.BlockSpec((None, None, 1, SLAB_STATE), lambda s, d: (d, s, 0, 0))],
        out_specs=pl.BlockSpec((None, None, B, n_lc, SLAB_STATE), lambda s, d: (d, s, 0, 0, 0)),
        out_shape=jax.ShapeDtypeStruct((2, S, B, n_lc, SLAB_STATE), BF16),
        scratch_shapes=[pltpu.VMEM((B, n_cc, SLAB_STATE), F32),
                        pltpu.VMEM((B, n_lc, SLAB_STATE), F32),
                        pltpu.VMEM((B, n_lc, SLAB_STATE), F32)],
        compiler_params=_params(2),
        name="s5_scan",
    )(uc16, ul16, bzx, a16x)


def _s5out_body(ul_ref, m_ref, cc_ref, sin_ref, y_ref):
    B, n_lc, _ = ul_ref.shape
    rows = B * n_lc
    y = _dot(ul_ref[...].reshape(rows, SLAB_IN), m_ref[...])
    y = y + _dot(sin_ref[0].reshape(rows, SLAB_STATE), cc_ref[0])
    y = y + _dot(sin_ref[1].reshape(rows, SLAB_STATE), cc_ref[1])
    y_ref[...] = y.reshape(B, n_lc, y.shape[-1])


def _s5out_call(ul16, m, ccx, sin, nb):
    S, B, n_lc, _ = ul16.shape
    return pl.pallas_call(
        _s5out_body,
        grid=(S, SLAB_IN // nb),
        in_specs=[pl.BlockSpec((None, B, n_lc, SLAB_IN), lambda s, j: (s, 0, 0, 0)),
                  pl.BlockSpec((None, SLAB_IN, nb), lambda s, j: (s, 0, j)),
                  pl.BlockSpec((2, None, SLAB_STATE, nb), lambda s, j: (0, s, 0, j)),
                  pl.BlockSpec((2, None, B, n_lc, SLAB_STATE), lambda s, j: (0, s, 0, 0, 0))],
        out_specs=pl.BlockSpec((None, B, n_lc, nb), lambda s, j: (s, 0, 0, j)),
        out_shape=jax.ShapeDtypeStruct((S, B, n_lc, SLAB_IN), F32),
        compiler_params=_params(2),
        name="s5_out",
    )(ul16, m, ccx, sin)


def _merge_body(y_ref, o_ref, g_ref, x_ref, mod_ref, gw_ref, gb_ref, wbs_ref, wba_ref, wo_ref, out_ref):
    y = jnp.concatenate([y_ref[s] for s in range(N_SLABS)], axis=1)
    hh = 0.5 * y * (1.0 + jnp.tanh(math.sqrt(2.0 / math.pi) * (y + 0.044715 * (y * y * y))))
    gl = _dot(hh.astype(BF16), gw_ref[...]) + gb_ref[...]
    ys = _dot((hh * _sigmoid(gl)).astype(BF16), wbs_ref[...])
    ya = _dot(o_ref[...].astype(BF16), wba_ref[...])
    g = g_ref[...]
    mix = g[:, :D_MODEL] * ys + g[:, D_MODEL:] * ya
    out_ref[...] = x_ref[...] + mod_ref[2:3] * _dot(mix.astype(BF16), wo_ref[...])


def _merge_call(y, o, g, x, mod3, gw, gb, wbs, wba, wo, tm):
    B, L, D = x.shape
    c2 = lambda b, i: (0, 0)
    return pl.pallas_call(
        _merge_body,
        grid=(B, L // tm),
        in_specs=[pl.BlockSpec((N_SLABS, None, tm, SLAB), lambda b, i: (0, b, i, 0)),
                  pl.BlockSpec((None, tm, ATTN_W), lambda b, i: (b, i, 0)),
                  pl.BlockSpec((None, tm, 2 * D), lambda b, i: (b, i, 0)),
                  pl.BlockSpec((None, tm, D), lambda b, i: (b, i, 0)),
                  pl.BlockSpec((None, 6, D), lambda b, i: (b, 0, 0)),
                  pl.BlockSpec((S5_W, S5_W), c2), pl.BlockSpec((1, S5_W), c2),
                  pl.BlockSpec((S5_W, D), c2), pl.BlockSpec((ATTN_W, D), c2), pl.BlockSpec((D, D), c2)],
        out_specs=pl.BlockSpec((None, tm, D), lambda b, i: (b, i, 0)),
        out_shape=jax.ShapeDtypeStruct((B, L, D), F32),
        compiler_params=_params(2),
        name="merge",
    )(y, o, g, x, mod3, gw, gb, wbs, wba, wo)


def _ffn_body(xp_ref, x_ref, xn_ref, mod_ref, n2w_ref, wup_ref, cw_ref, cb_ref, wdn_ref, out_ref, *, tm):
    i = pl.program_id(1)
    last = pl.num_programs(1) - 1
    mod = mod_ref[...]
    n2w = n2w_ref[...]

    def modulated(v):
        ms = jnp.mean(v * v, axis=-1, keepdims=True)
        return (v * lax.rsqrt(ms + EPS)) * n2w * (1.0 + mod[4:5]) + mod[3:4]

    x = x_ref[...]
    hp = modulated(xp_ref[...]) * (i > 0).astype(F32)
    hn = modulated(xn_ref[...]) * (i < last).astype(F32)
    hcat = jnp.concatenate([hp, modulated(x), hn], axis=0).astype(BF16)
    rows = tm + 2 * HALO

    def conv(u, cw, cb):
        prev = pltpu.roll(u, 1, 0)[HALO:HALO + tm]
        nxt = pltpu.roll(u, rows - 1, 0)[HALO:HALO + tm]
        return cb + prev * cw[0:1] + u[HALO:HALO + tm] * cw[1:2] + nxt * cw[2:3]

    def chunk(f, acc):
        ya = conv(_dot(hcat, wup_ref[0, f]), cw_ref[0, f], cb_ref[0, f])
        yg = conv(_dot(hcat, wup_ref[1, f]), cw_ref[1, f], cb_ref[1, f])
        act = (yg * _sigmoid(yg)) * ya
        return acc + _dot(act.astype(BF16), wdn_ref[f])

    acc = lax.fori_loop(0, N_FF_CHUNKS, chunk, jnp.zeros((tm, D_MODEL), F32))
    out_ref[...] = x + mod[5:6] * acc


def _ffn_call(x1, mod3, n2w, wup, cw, cb, wdn, tm):
    B, L, D = x1.shape
    nh = tm // HALO
    n_halo_blocks = L // HALO
    c2 = lambda b, i: (0, 0)
    c3 = lambda b, i: (0, 0, 0)
    c4 = lambda b, i: (0, 0, 0, 0)
    return pl.pallas_call(
        functools.partial(_ffn_body, tm=tm),
        grid=(B, L // tm),
        in_specs=[pl.BlockSpec((None, HALO, D), lambda b, i: (b, jnp.maximum(i * nh - 1, 0), 0)),
                  pl.BlockSpec((None, tm, D), lambda b, i: (b, i, 0)),
                  pl.BlockSpec((None, HALO, D), lambda b, i: (b, jnp.minimum((i + 1) * nh, n_halo_blocks - 1), 0)),
                  pl.BlockSpec((None, 6, D), lambda b, i: (b, 0, 0)),
                  pl.BlockSpec((1, D), c2),
                  pl.BlockSpec((2, N_FF_CHUNKS, D, FF_CHUNK), c4),
                  pl.BlockSpec((2, N_FF_CHUNKS, 3, FF_CHUNK), c4),
                  pl.BlockSpec((2, N_FF_CHUNKS, 1, FF_CHUNK), c4),
                  pl.BlockSpec((N_FF_CHUNKS, FF_CHUNK, D), c3)],
        out_specs=pl.BlockSpec((None, tm, D), lambda b, i: (b, i, 0)),
        out_shape=jax.ShapeDtypeStruct((B, L, D), F32),
        compiler_params=_params(2),
        name="ffn",
    )(x1, x1, x1, mod3, n2w, wup, cw, cb, wdn)


def _rope_tables(n_ctx, L):
    t = jnp.arange(L)
    row = (t // GRID_W).astype(F32)
    col = (t % GRID_W).astype(F32)
    half = HEAD_DIM // 2
    inv_freq = ROPE_THETA ** (-jnp.arange(0, half, 2, dtype=F32) / half)
    ang_r = row[:, None] * inv_freq
    ang_c = col[:, None] * inv_freq

    def lanes(fr, fc, sign):
        per_map = jnp.concatenate([sign * fr, fr, sign * fc, fc], axis=1)
        return jnp.concatenate([per_map, per_map], axis=1)

    cos_t = lanes(jnp.cos(ang_r), jnp.cos(ang_c), 1.0)
    sin_t = lanes(jnp.sin(ang_r), jnp.sin(ang_c), -1.0)
    cos_t = jnp.concatenate([jnp.ones((n_ctx, LANES), F32), cos_t], axis=0)
    sin_t = jnp.concatenate([jnp.zeros((n_ctx, LANES), F32), sin_t], axis=0)
    return cos_t, sin_t


def kernel(x, c, ctx, c_ctx, ada_w, ada_b, norm1_w, w_in, b_gate, q_norm_w, k_norm_w, lam_q1, lam_k1, lam_q2, lam_k2, subln_w, s5_a_re, s5_a_im, s5_log_dt, s5_b_re, s5_b_im, s5_c_re, s5_c_im, s5_d, glu_w, glu_b, w_branch_s5, w_branch_attn, w_out, norm2_w, w_up, conv_w, conv_b, w_down):
    B, L, D = x.shape
    n_ctx = ctx.shape[1]
    assert ada_w.shape[0] == 1 and D == D_MODEL and B + 1 <= 8

    cc = jnp.zeros((8, D), F32).at[:B].set(c).at[B].set(c_ctx)
    mod3 = _mod_call(cc, ada_w[0], ada_b).reshape(8, 6, D)

    cos_t, sin_t = _rope_tables(n_ctx, L)
    lane = jnp.arange(LANES)
    seg = jnp.where((lane[:, None] // HEAD_DIM) == (lane[None, :] // HEAD_DIM), 1.0 / HEAD_DIM, 0.0).astype(BF16)
    qw = jnp.tile(q_norm_w[0], 2)[None]
    kw = jnp.tile(k_norm_w[0], 2)[None]
    qT, k, vT, uc, ul, g = _inproj_call(ctx, x, mod3, norm1_w, w_in[0].astype(BF16), qw, kw, b_gate,
                                        cos_t, sin_t, seg)

    o = _attn_call(qT, k, vT, lam_q1, lam_k1, lam_q2, lam_k2, subln_w, tq=256)

    tr = lambda a: jnp.transpose(a[0], (1, 0, 2))
    kall, k0, bz, ccp, a16 = _s5prep_call(
        tr(s5_a_re), tr(s5_a_im), tr(s5_log_dt[..., None]),
        jnp.transpose(s5_b_re[0], (1, 0, 3, 2)), jnp.transpose(s5_b_im[0], (1, 0, 3, 2)),
        jnp.transpose(s5_c_re[0], (1, 0, 2, 3)), jnp.transpose(s5_c_im[0], (1, 0, 2, 3)),
        s5_d[0].reshape(S5_GROUPS, S5_GROUP, 1))
    m, bzx, ccx, a16x = _expand_s5_tables(kall, k0, bz, ccp, a16)
    uc16 = uc.reshape(N_SLABS, B, n_ctx // CHUNK, SLAB_IN)
    ul16 = ul.reshape(N_SLABS, B, L // CHUNK, SLAB_IN)
    sin = _s5scan_call(uc16, ul16, bzx, a16x)
    y = _s5out_call(ul16, m, ccx, sin, nb=512).reshape(N_SLABS, B, L, SLAB)

    x1 = _merge_call(y, o, g, x, mod3, glu_w[0].astype(BF16), glu_b, w_branch_s5[0].astype(BF16),
                     w_branch_attn[0].astype(BF16), w_out[0].astype(BF16), tm=512)

    nf = N_FF_CHUNKS
    wup = w_up[0].astype(BF16).reshape(D, 2, nf, FF_CHUNK).transpose(1, 2, 0, 3)
    cw = conv_w[0].reshape(3, 2, nf, FF_CHUNK).transpose(1, 2, 0, 3)
    cb = conv_b[0].reshape(2, nf, 1, FF_CHUNK)
    wdn = w_down[0].astype(BF16).reshape(nf, FF_CHUNK, D)
    return _ffn_call(x1, mod3, norm2_w, wup, cw, cb, wdn, tm=512)
```

```python
import functools
import math

import jax
import jax.numpy as jnp
from jax import lax
from jax.experimental import pallas as pl
from jax.experimental.pallas import tpu as pltpu

D_MODEL = 1024
GRID_W = 64
N_HEADS = 8
HEAD_DIM = 64
V_DIM = 2 * HEAD_DIM
QK_W = N_HEADS * 2 * HEAD_DIM
ATTN_W = N_HEADS * V_DIM
S5_W = 512
S5_GROUP = 16
S5_GROUPS = S5_W // S5_GROUP
S5_STATE = 64
D_FF = 2816
ROPE_THETA = 10000.0
EPS = 1e-6
MIN_NEG_RE = -1e-4
K_OFF = QK_W
V_OFF = 2 * QK_W
U_OFF = 2 * QK_W + ATTN_W
G_OFF = U_OFF + S5_W
N_IN = G_OFF + 2 * D_MODEL
LAM_INIT = 0.8 - 0.6 * math.exp(-0.3 * 0)
Q_SCALE = HEAD_DIM ** -0.5 * math.log2(math.e)

LANES = 128
TOK_TILE = 256
CHUNK = 16
SLAB = LANES
N_SLABS = S5_W // SLAB
GROUPS_PER_SLAB = SLAB // S5_GROUP
SLAB_IN = CHUNK * SLAB
SLAB_STATE = GROUPS_PER_SLAB * 2 * S5_STATE
FF_CHUNK = 256
N_FF_CHUNKS = D_FF // FF_CHUNK
HALO = 8
VMEM_LIMIT = 56 * 1024 * 1024

F32 = jnp.float32
BF16 = jnp.bfloat16


def _sigmoid(x):
    return 1.0 / (1.0 + jnp.exp(-x))


def _dot(a, b):
    return jnp.dot(a, b, preferred_element_type=F32)


def _params(n_parallel, n_arbitrary=0):
    return pltpu.CompilerParams(
        dimension_semantics=("parallel",) * n_parallel + ("arbitrary",) * n_arbitrary,
        vmem_limit_bytes=VMEM_LIMIT)


def _mod_body(c_ref, w_ref, b_ref, o_ref):
    c = c_ref[...]
    s = c * _sigmoid(c)
    o_ref[...] = jnp.dot(s, w_ref[...], preferred_element_type=F32,
                         precision=lax.Precision.HIGHEST) + b_ref[...]


def _mod_call(cc, ada_w, ada_b):
    n = ada_w.shape[1]
    return pl.pallas_call(
        _mod_body,
        grid=(n // D_MODEL,),
        in_specs=[pl.BlockSpec((8, D_MODEL), lambda j: (0, 0)),
                  pl.BlockSpec((D_MODEL, D_MODEL), lambda j: (0, j)),
                  pl.BlockSpec((1, D_MODEL), lambda j: (0, j))],
        out_specs=pl.BlockSpec((8, D_MODEL), lambda j: (0, j)),
        out_shape=jax.ShapeDtypeStruct((8, n), F32),
        compiler_params=_params(1),
        name="mod",
    )(cc, ada_w, ada_b)


def _inproj_body(ctx_ref, x_ref, mod_ref, n1w_ref, w_ref, qw_ref, kw_ref, bg_ref, cos_ref, sin_ref,
                 seg_ref, qT_ref, k_ref, vT_ref, uc_ref, ul_ref, g_ref):
    i = pl.program_id(1)
    is_ctx = i == 0
    xt = jnp.where(is_ctx, ctx_ref[...], x_ref[...])
    ms = jnp.mean(xt * xt, axis=-1, keepdims=True)
    mod = mod_ref[...]
    h = (xt * lax.rsqrt(ms + EPS)) * n1w_ref[...] * (1.0 + mod[1:2]) + mod[0:1]
    hb = h.astype(BF16)

    cos = cos_ref[...]
    sin = sin_ref[...]
    seg = seg_ref[...]
    lane = lax.broadcasted_iota(jnp.int32, cos.shape, 1)
    first_half = (lane % 32) < 16

    def norm_rope(zh, w):
        sq = zh * zh
        hi = sq.astype(BF16)
        lo = (sq - hi.astype(F32)).astype(BF16)
        msq = _dot(hi, seg) + _dot(lo, seg)
        y = zh * lax.rsqrt(msq + EPS) * w
        partner = jnp.where(first_half, pltpu.roll(y, LANES - 16, 1), pltpu.roll(y, 16, 1))
        return y * cos + partner * sin

    zk = _dot(hb, w_ref[:, K_OFF:V_OFF])
    kw = kw_ref[...]
    for hh in range(N_HEADS):
        k_ref[hh] = norm_rope(zk[:, hh * LANES:(hh + 1) * LANES], kw).astype(BF16)

    zv = _dot(hb, w_ref[:, V_OFF:U_OFF])
    for hh in range(N_HEADS):
        vT_ref[hh] = zv[:, hh * LANES:(hh + 1) * LANES].T.astype(BF16)

    zu = _dot(hb, w_ref[:, U_OFF:G_OFF])

    @pl.when(is_ctx)
    def _():
        for s in range(N_SLABS):
            uc_ref[s] = zu[:, s * SLAB:(s + 1) * SLAB].astype(BF16)

    @pl.when(jnp.logical_not(is_ctx))
    def _():
        for s in range(N_SLABS):
            ul_ref[s] = zu[:, s * SLAB:(s + 1) * SLAB].astype(BF16)
        zq = _dot(hb, w_ref[:, 0:K_OFF])
        qw = qw_ref[...]
        row = lax.broadcasted_iota(jnp.int32, (LANES, TOK_TILE), 0)
        for hh in range(N_HEADS):
            q = norm_rope(zq[:, hh * LANES:(hh + 1) * LANES], qw) * Q_SCALE
            qT = q.T
            qT_ref[hh, 0] = jnp.where(row < HEAD_DIM, qT, 0.0).astype(BF16)
            qT_ref[hh, 1] = jnp.where(row >= HEAD_DIM, qT, 0.0).astype(BF16)
        zg = _dot(hb, w_ref[:, G_OFF:N_IN]) + bg_ref[...]
        g_ref[...] = _sigmoid(zg)


def _inproj_call(ctx, x, mod3, n1w, w_in_b, qw, kw, bg, cos_t, sin_t, seg):
    B, L, D = x.shape
    n_ctx = ctx.shape[1]
    assert n_ctx == TOK_TILE and L % TOK_TILE == 0
    nt = L // TOK_TILE
    n_all = n_ctx + L
    T = TOK_TILE
    lat = lambda i: jnp.maximum(i - 1, 0)
    const2 = lambda b, i: (0, 0)
    return pl.pallas_call(
        _inproj_body,
        grid=(B, nt + 1),
        in_specs=[
            pl.BlockSpec((None, T, D), lambda b, i: (b, 0, 0)),
            pl.BlockSpec((None, T, D), lambda b, i: (b, lat(i), 0)),
            pl.BlockSpec((None, 6, D), lambda b, i: (jnp.where(i == 0, B, b), 0, 0)),
            pl.BlockSpec((1, D), const2),
            pl.BlockSpec((D, N_IN), const2),
            pl.BlockSpec((1, LANES), const2),
            pl.BlockSpec((1, LANES), const2),
            pl.BlockSpec((1, 2 * D), const2),
            pl.BlockSpec((T, LANES), lambda b, i: (i, 0)),
            pl.BlockSpec((T, LANES), lambda b, i: (i, 0)),
            pl.BlockSpec((LANES, LANES), const2),
        ],
        out_specs=[
            pl.BlockSpec((None, N_HEADS, 2, LANES, T), lambda b, i: (b, 0, 0, 0, lat(i))),
            pl.BlockSpec((None, N_HEADS, T, LANES), lambda b, i: (b, 0, i, 0)),
            pl.BlockSpec((None, N_HEADS, LANES, T), lambda b, i: (b, 0, 0, i)),
            pl.BlockSpec((N_SLABS, None, T, SLAB), lambda b, i: (0, b, 0, 0)),
            pl.BlockSpec((N_SLABS, None, T, SLAB), lambda b, i: (0, b, lat(i), 0)),
            pl.BlockSpec((None, T, 2 * D), lambda b, i: (b, lat(i), 0)),
        ],
        out_shape=[
            jax.ShapeDtypeStruct((B, N_HEADS, 2, LANES, L), BF16),
            jax.ShapeDtypeStruct((B, N_HEADS, n_all, LANES), BF16),
            jax.ShapeDtypeStruct((B, N_HEADS, LANES, n_all), BF16),
            jax.ShapeDtypeStruct((N_SLABS, B, n_ctx, SLAB), BF16),
            jax.ShapeDtypeStruct((N_SLABS, B, L, SLAB), BF16),
            jax.ShapeDtypeStruct((B, L, 2 * D), F32),
        ],
        compiler_params=_params(1, 1),
        name="in_proj",
    )(ctx, x, mod3, n1w, w_in_b, qw, kw, bg, cos_t, sin_t, seg)


def _attn_body(qT_ref, k_ref, vT_ref, lq1_ref, lk1_ref, lq2_ref, lk2_ref, sw_ref, o_ref, *, tq):
    qT = jnp.concatenate([qT_ref[0], qT_ref[1]], axis=1)
    s = _dot(k_ref[...], qT)
    m = jnp.max(s, axis=0, keepdims=True)
    p = jnp.exp2(s - m)
    l = jnp.sum(p, axis=0, keepdims=True)
    oT = _dot(vT_ref[...], p.astype(BF16)) / l
    lam = (jnp.exp(jnp.sum(lq1_ref[...] * lk1_ref[...], axis=-1, keepdims=True))
           - jnp.exp(jnp.sum(lq2_ref[...] * lk2_ref[...], axis=-1, keepdims=True)) + LAM_INIT)
    od = (oT[:, :tq] - lam * oT[:, tq:]).T
    ms = jnp.mean(od * od, axis=-1, keepdims=True)
    o_ref[...] = (od * lax.rsqrt(ms + EPS)) * sw_ref[...] * (1.0 - LAM_INIT)


def _attn_call(qT, k, vT, lq1, lk1, lq2, lk2, sw, tq):
    B, H, _, _, L = qT.shape
    n_all = k.shape[2]
    vec = pl.BlockSpec((1, HEAD_DIM), lambda b, h, i: (0, 0))
    return pl.pallas_call(
        functools.partial(_attn_body, tq=tq),
        grid=(B, H, L // tq),
        in_specs=[
            pl.BlockSpec((None, None, 2, LANES, tq), lambda b, h, i: (b, h, 0, 0, i)),
            pl.BlockSpec((None, None, n_all, LANES), lambda b, h, i: (b, h, 0, 0)),
            pl.BlockSpec((None, None, LANES, n_all), lambda b, h, i: (b, h, 0, 0)),
            vec, vec, vec, vec,
            pl.BlockSpec((1, V_DIM), lambda b, h, i: (0, 0)),
        ],
        out_specs=pl.BlockSpec((None, tq, V_DIM), lambda b, h, i: (b, i, h)),
        out_shape=jax.ShapeDtypeStruct((B, L, ATTN_W), F32),
        compiler_params=_params(3),
        name="attn",
    )(qT, k, vT, lq1, lk1, lq2, lk2, sw)


def _cmul(xr, xi, yr, yi):
    return xr * yr - xi * yi, xr * yi + xi * yr


def _a_bar(are, aim, ldt):
    lr = jnp.minimum(are, MIN_NEG_RE)
    dt = jnp.exp(ldt)
    mag = jnp.exp(lr * dt)
    return lr, aim, mag * jnp.cos(aim * dt), mag * jnp.sin(aim * dt)


def _powers(ar, ai, n):
    pows = [(jnp.ones_like(ar), jnp.zeros_like(ar))]
    for _ in range(n):
        pows.append(_cmul(pows[-1][0], pows[-1][1], ar, ai))
    return pows


def _s5tab_body(aar_ref, aai_ref, adt_ref, abr_ref, abi_ref, bar_ref, bai_ref, bdt_ref, bcr_ref, bci_ref,
                car_ref, cai_ref, cdt_ref, dsk_ref, m_ref, bz_ref, cc_ref, a16_ref, lag_ref):
    d = pl.program_id(1)
    fwd = d == 0
    half = SLAB_STATE // 2
    sel = lambda a, b: (jnp.where(fwd, a[0], b[0]), jnp.where(fwd, a[1], b[1]))

    lr, li, ar, ai = _a_bar(aar_ref[...], aai_ref[...], adt_ref[...])
    den = lr * lr + li * li
    nr = ar - 1.0
    cfr = (nr * lr + ai * li) / den
    cfi = (ai * lr - nr * li) / den
    bbr, bbi = _cmul(cfr, cfi, abr_ref[...], abi_ref[...])
    pa = _powers(ar, ai, CHUNK - 1)
    row_a = lax.broadcasted_iota(jnp.int32, (SLAB, half), 0) // S5_GROUP
    col_a = lax.broadcasted_iota(jnp.int32, (SLAB, half), 1) // S5_STATE
    diag_a = row_a == col_a

    def expand_a(v):
        return jnp.where(diag_a, jnp.concatenate([v] * (half // LANES), axis=1), 0.0)

    for t in range(CHUNK):
        pw = sel(pa[CHUNK - 1 - t], pa[t])
        zr, zi = _cmul(pw[0], pw[1], bbr, bbi)
        bz_ref[t * SLAB:(t + 1) * SLAB, :] = jnp.concatenate([expand_a(zr), expand_a(zi)], axis=1).astype(BF16)

    _, _, br_, bi_ = _a_bar(bar_ref[...], bai_ref[...], bdt_ref[...])
    pb = _powers(br_, bi_, CHUNK)
    cr = bcr_ref[...]
    ci = bci_ref[...]
    row_b = lax.broadcasted_iota(jnp.int32, (half, SLAB), 0) // S5_STATE
    col_b = lax.broadcasted_iota(jnp.int32, (half, SLAB), 1) // S5_GROUP
    diag_b = row_b == col_b

    def expand_b(v):
        return jnp.where(diag_b, jnp.concatenate([v] * GROUPS_PER_SLAB, axis=0), 0.0)

    for t in range(CHUNK):
        pw = sel(pb[t + 1], pb[CHUNK - t])
        ctr, cti = _cmul(cr, ci, pw[0], pw[1])
        cc_ref[:, t * SLAB:(t + 1) * SLAB] = jnp.concatenate([expand_b(ctr), expand_b(-cti)], axis=0).astype(BF16)

    lane = lax.broadcasted_iota(jnp.int32, (SLAB, LANES), 1)
    lhs = jnp.where(lane < S5_STATE, bbr, bbi)
    r_k = lax.broadcasted_iota(jnp.int32, (SLAB, SLAB), 0)
    c_k = lax.broadcasted_iota(jnp.int32, (SLAB, SLAB), 1)
    diag_k = (r_k // S5_GROUP) == (c_k // S5_GROUP)
    sign = 1 - 2 * d
    for tau in range(CHUNK):
        ctr, cti = _cmul(cr, ci, pb[tau][0], pb[tau][1])
        rhs = jnp.concatenate([ctr, -cti], axis=0)
        ktau = jnp.where(diag_k, jnp.dot(lhs, rhs, preferred_element_type=F32,
                                         precision=lax.Precision.HIGHEST), 0.0)
        if tau == 0:
            @pl.when(fwd)
            def _():
                lag_ref[CHUNK - 1] = ktau + jnp.where(r_k == c_k, dsk_ref[...], 0.0)

            @pl.when(jnp.logical_not(fwd))
            def _():
                lag_ref[CHUNK - 1] = lag_ref[CHUNK - 1] + ktau
        else:
            lag_ref[CHUNK - 1 + sign * tau] = ktau

    @pl.when(jnp.logical_not(fwd))
    def _():
        for tp in range(CHUNK):
            for t in range(CHUNK):
                m_ref[tp * SLAB:(tp + 1) * SLAB, t * SLAB:(t + 1) * SLAB] = lag_ref[t - tp + CHUNK - 1].astype(BF16)

    _, _, xr, xi = _a_bar(car_ref[...], cai_ref[...], cdt_ref[...])
    for _ in range(4):
        xr, xi = _cmul(xr, xi, xr, xi)
    a16_ref[...] = jnp.concatenate([xr, xi], axis=1)


def _s5tab_call(s5_a_re, s5_a_im, s5_log_dt, s5_b_re, s5_b_im, s5_c_re, s5_c_im, s5_d):
    S, GS, C, P = N_SLABS, GROUPS_PER_SLAB, S5_GROUP, S5_STATE
    half = SLAB_STATE // 2

    def lay_a(a):
        return jnp.broadcast_to(a.reshape(2, S, GS, 1, 1, P), (2, S, GS, C, 2, P)).reshape(2, S, SLAB, LANES)

    def lay_b(a):
        a = a.reshape(2, S, GS, P).transpose(0, 1, 3, 2)
        return jnp.broadcast_to(a[..., None], (2, S, P, GS, C)).reshape(2, S, P, SLAB)

    ldt = jnp.broadcast_to(s5_log_dt[..., None], (2, S5_GROUPS, P))
    b_a = lambda b: jnp.broadcast_to(b.transpose(0, 1, 3, 2).reshape(2, S, GS, C, 1, P),
                                     (2, S, GS, C, 2, P)).reshape(2, S, SLAB, LANES)
    c_b = lambda c: c.reshape(2, S, GS, C, P).transpose(0, 1, 4, 2, 3).reshape(2, S, P, SLAB)
    lay_c = lambda a: a.reshape(2, S, 1, half)
    blk = lambda r, c: pl.BlockSpec((None, None, r, c), lambda s, d: (d, s, 0, 0))
    return pl.pallas_call(
        _s5tab_body,
        grid=(S, 2),
        in_specs=[blk(SLAB, LANES)] * 5 + [blk(P, SLAB)] * 5 + [blk(1, half)] * 3
                 + [pl.BlockSpec((None, 1, SLAB), lambda s, d: (s, 0, 0))],
        out_specs=[pl.BlockSpec((None, SLAB_IN, SLAB_IN), lambda s, d: (s, 0, 0)),
                   blk(SLAB_IN, SLAB_STATE), blk(SLAB_STATE, SLAB_IN), blk(1, SLAB_STATE)],
        out_shape=[jax.ShapeDtypeStruct((S, SLAB_IN, SLAB_IN), BF16),
                   jax.ShapeDtypeStruct((2, S, SLAB_IN, SLAB_STATE), BF16),
                   jax.ShapeDtypeStruct((2, S, SLAB_STATE, SLAB_IN), BF16),
                   jax.ShapeDtypeStruct((2, S, 1, SLAB_STATE), F32)],
        scratch_shapes=[pltpu.VMEM((2 * CHUNK - 1, SLAB, SLAB), F32)],
        compiler_params=_params(0, 2),
        name="s5_tab",
    )(lay_a(s5_a_re), lay_a(s5_a_im), lay_a(ldt), b_a(s5_b_re), b_a(s5_b_im),
      lay_b(s5_a_re), lay_b(s5_a_im), lay_b(ldt), c_b(s5_c_re), c_b(s5_c_im),
      lay_c(s5_a_re), lay_c(s5_a_im), lay_c(ldt), s5_d.reshape(S, 1, SLAB))


def _s5scan_body(uc_ref, ul_ref, bz_ref, a_ref, sin_ref, zc_ref, zl_ref, st_ref, *, n_cc, n_lc):
    d = pl.program_id(1)
    B = uc_ref.shape[0]
    w = bz_ref[...]
    zc_ref[...] = _dot(uc_ref[...].reshape(B * n_cc, SLAB_IN), w).reshape(B, n_cc, SLAB_STATE)
    zl_ref[...] = _dot(ul_ref[...].reshape(B * n_lc, SLAB_IN), w).reshape(B, n_lc, SLAB_STATE)
    a = a_ref[...]
    half = SLAB_STATE // 2
    a_same = jnp.concatenate([a[:, :half], a[:, :half]], axis=1)
    a_swap = jnp.concatenate([-a[:, half:], a[:, half:]], axis=1)

    def advance(s, z):
        return a_same * s + a_swap * pltpu.roll(s, half, 1) + z

    def ctx_step(i, s):
        kk = jnp.where(d == 0, i, n_cc - 1 - i)
        return tuple(advance(s[b], zc_ref[b, pl.ds(kk, 1), :]) for b in range(B))

    def lat_step(i, s):
        kk = jnp.where(d == 0, i, n_lc - 1 - i)
        for b in range(B):
            st_ref[b, pl.ds(kk, 1), :] = s[b]
        return tuple(advance(s[b], zl_ref[b, pl.ds(kk, 1), :]) for b in range(B))

    s = lax.fori_loop(0, n_cc, ctx_step, tuple(jnp.zeros((1, SLAB_STATE), F32) for _ in range(B)))
    lax.fori_loop(0, n_lc, lat_step, s)
    sin_ref[...] = st_ref[...].astype(BF16)


def _s5scan_call(uc16, ul16, bzx, a16x):
    S, B, n_cc, _ = uc16.shape
    n_lc = ul16.shape[2]
    return pl.pallas_call(
        functools.partial(_s5scan_body, n_cc=n_cc, n_lc=n_lc),
        grid=(S, 2),
        in_specs=[pl.BlockSpec((None, B, n_cc, SLAB_IN), lambda s, d: (s, 0, 0, 0)),
                  pl.BlockSpec((None, B, n_lc, SLAB_IN), lambda s, d: (s, 0, 0, 0)),
                  pl.BlockSpec((None, None, SLAB_IN, SLAB_STATE), lambda s, d: (d, s, 0, 0)),
                  pl.BlockSpec((None, None, 1, SLAB_STATE), lambda s, d: (d, s, 0, 0))],
        out_specs=pl.BlockSpec((None, None, B, n_lc, SLAB_STATE), lambda s, d: (d, s, 0, 0, 0)),
        out_shape=jax.ShapeDtypeStruct((2, S, B, n_lc, SLAB_STATE), BF16),
        scratch_shapes=[pltpu.VMEM((B, n_cc, SLAB_STATE), F32),
                        pltpu.VMEM((B, n_lc, SLAB_STATE), F32),
                        pltpu.VMEM((B, n_lc, SLAB_STATE), F32)],
        compiler_params=_params(2),
        name="s5_scan",
    )(uc16, ul16, bzx, a16x)


def _s5out_body(ul_ref, m_ref, cc_ref, sin_ref, y_ref):
    B, n_lc, _ = ul_ref.shape
    rows = B * n_lc
    y = _dot(ul_ref[...].reshape(rows, SLAB_IN), m_ref[...])
    y = y + _dot(sin_ref[0].reshape(rows, SLAB_STATE), cc_ref[0])
    y = y + _dot(sin_ref[1].reshape(rows, SLAB_STATE), cc_ref[1])
    y_ref[...] = y.reshape(B, n_lc, y.shape[-1])


def _s5out_call(ul16, m, ccx, sin, nb):
    S, B, n_lc, _ = ul16.shape
    return pl.pallas_call(
        _s5out_body,
        grid=(S, SLAB_IN // nb),
        in_specs=[pl.BlockSpec((None, B, n_lc, SLAB_IN), lambda s, j: (s, 0, 0, 0)),
                  pl.BlockSpec((None, SLAB_IN, nb), lambda s, j: (s, 0, j)),
                  pl.BlockSpec((2, None, SLAB_STATE, nb), lambda s, j: (0, s, 0, j)),
                  pl.BlockSpec((2, None, B, n_lc, SLAB_STATE), lambda s, j: (0, s, 0, 0, 0))],
        out_specs=pl.BlockSpec((None, B, n_lc, nb), lambda s, j: (s, 0, 0, j)),
        out_shape=jax.ShapeDtypeStruct((S, B, n_lc, SLAB_IN), F32),
        compiler_params=_params(2),
        name="s5_out",
    )(ul16, m, ccx, sin)


def _merge_body(y_ref, o_ref, g_ref, x_ref, mod_ref, gw_ref, gb_ref, wbs_ref, wba_ref, wo_ref, out_ref):
    y = jnp.concatenate([y_ref[s] for s in range(N_SLABS)], axis=1)
    hh = 0.5 * y * (1.0 + jnp.tanh(math.sqrt(2.0 / math.pi) * (y + 0.044715 * (y * y * y))))
    gl = _dot(hh.astype(BF16), gw_ref[...]) + gb_ref[...]
    ys = _dot((hh * _sigmoid(gl)).astype(BF16), wbs_ref[...])
    ya = _dot(o_ref[...].astype(BF16), wba_ref[...])
    g = g_ref[...]
    mix = g[:, :D_MODEL] * ys + g[:, D_MODEL:] * ya
    out_ref[...] = x_ref[...] + mod_ref[2:3] * _dot(mix.astype(BF16), wo_ref[...])


def _merge_call(y, o, g, x, mod3, gw, gb, wbs, wba, wo, tm):
    B, L, D = x.shape
    c2 = lambda b, i: (0, 0)
    return pl.pallas_call(
        _merge_body,
        grid=(B, L // tm),
        in_specs=[pl.BlockSpec((N_SLABS, None, tm, SLAB), lambda b, i: (0, b, i, 0)),
                  pl.BlockSpec((None, tm, ATTN_W), lambda b, i: (b, i, 0)),
                  pl.BlockSpec((None, tm, 2 * D), lambda b, i: (b, i, 0)),
                  pl.BlockSpec((None, tm, D), lambda b, i: (b, i, 0)),
                  pl.BlockSpec((None, 6, D), lambda b, i: (b, 0, 0)),
                  pl.BlockSpec((S5_W, S5_W), c2), pl.BlockSpec((1, S5_W), c2),
                  pl.BlockSpec((S5_W, D), c2), pl.BlockSpec((ATTN_W, D), c2), pl.BlockSpec((D, D), c2)],
        out_specs=pl.BlockSpec((None, tm, D), lambda b, i: (b, i, 0)),
        out_shape=jax.ShapeDtypeStruct((B, L, D), F32),
        compiler_params=_params(2),
        name="merge",
    )(y, o, g, x, mod3, gw, gb, wbs, wba, wo)


def _ffn_body(xp_ref, x_ref, xn_ref, mod_ref, n2w_ref, wup_ref, cw_ref, cb_ref, wdn_ref, out_ref, *, tm):
    i = pl.program_id(1)
    last = pl.num_programs(1) - 1
    mod = mod_ref[...]
    n2w = n2w_ref[...]

    def modulated(v):
        ms = jnp.mean(v * v, axis=-1, keepdims=True)
        return (v * lax.rsqrt(ms + EPS)) * n2w * (1.0 + mod[4:5]) + mod[3:4]

    x = x_ref[...]
    hp = modulated(xp_ref[...]) * (i > 0).astype(F32)
    hn = modulated(xn_ref[...]) * (i < last).astype(F32)
    hcat = jnp.concatenate([hp, modulated(x), hn], axis=0).astype(BF16)
    rows = tm + 2 * HALO

    def conv(u, cw, cb):
        prev = pltpu.roll(u, 1, 0)[HALO:HALO + tm]
        nxt = pltpu.roll(u, rows - 1, 0)[HALO:HALO + tm]
        return cb + prev * cw[0:1] + u[HALO:HALO + tm] * cw[1:2] + nxt * cw[2:3]

    def chunk(f, acc):
        ya = conv(_dot(hcat, wup_ref[0, f]), cw_ref[0, f], cb_ref[0, f])
        yg = conv(_dot(hcat, wup_ref[1, f]), cw_ref[1, f], cb_ref[1, f])
        act = (yg * _sigmoid(yg)) * ya
        return acc + _dot(act.astype(BF16), wdn_ref[f])

    acc = lax.fori_loop(0, N_FF_CHUNKS, chunk, jnp.zeros((tm, D_MODEL), F32))
    out_ref[...] = x + mod[5:6] * acc


def _ffn_call(x1, mod3, n2w, wup, cw, cb, wdn, tm):
    B, L, D = x1.shape
    nh = tm // HALO
    n_halo_blocks = L // HALO
    c2 = lambda b, i: (0, 0)
    c3 = lambda b, i: (0, 0, 0)
    c4 = lambda b, i: (0, 0, 0, 0)
    return pl.pallas_call(
        functools.partial(_ffn_body, tm=tm),
        grid=(B, L // tm),
        in_specs=[pl.BlockSpec((None, HALO, D), lambda b, i: (b, jnp.maximum(i * nh - 1, 0), 0)),
                  pl.BlockSpec((None, tm, D), lambda b, i: (b, i, 0)),
                  pl.BlockSpec((None, HALO, D), lambda b, i: (b, jnp.minimum((i + 1) * nh, n_halo_blocks - 1), 0)),
                  pl.BlockSpec((None, 6, D), lambda b, i: (b, 0, 0)),
                  pl.BlockSpec((1, D), c2),
                  pl.BlockSpec((2, N_FF_CHUNKS, D, FF_CHUNK), c4),
                  pl.BlockSpec((2, N_FF_CHUNKS, 3, FF_CHUNK), c4),
                  pl.BlockSpec((2, N_FF_CHUNKS, 1, FF_CHUNK), c4),
                  pl.BlockSpec((N_FF_CHUNKS, FF_CHUNK, D), c3)],
        out_specs=pl.BlockSpec((None, tm, D), lambda b, i: (b, i, 0)),
        out_shape=jax.ShapeDtypeStruct((B, L, D), F32),
        compiler_params=_params(2),
        name="ffn",
    )(x1, x1, x1, mod3, n2w, wup, cw, cb, wdn)


def _rope_tables(n_ctx, L):
    t = jnp.arange(L)
    row = (t // GRID_W).astype(F32)
    col = (t % GRID_W).astype(F32)
    half = HEAD_DIM // 2
    inv_freq = ROPE_THETA ** (-jnp.arange(0, half, 2, dtype=F32) / half)
    ang_r = row[:, None] * inv_freq
    ang_c = col[:, None] * inv_freq

    def lanes(fr, fc, sign):
        per_map = jnp.concatenate([sign * fr, fr, sign * fc, fc], axis=1)
        return jnp.concatenate([per_map, per_map], axis=1)

    cos_t = lanes(jnp.cos(ang_r), jnp.cos(ang_c), 1.0)
    sin_t = lanes(jnp.sin(ang_r), jnp.sin(ang_c), -1.0)
    cos_t = jnp.concatenate([jnp.ones((n_ctx, LANES), F32), cos_t], axis=0)
    sin_t = jnp.concatenate([jnp.zeros((n_ctx, LANES), F32), sin_t], axis=0)
    return cos_t, sin_t


def kernel(x, c, ctx, c_ctx, ada_w, ada_b, norm1_w, w_in, b_gate, q_norm_w, k_norm_w, lam_q1, lam_k1, lam_q2, lam_k2, subln_w, s5_a_re, s5_a_im, s5_log_dt, s5_b_re, s5_b_im, s5_c_re, s5_c_im, s5_d, glu_w, glu_b, w_branch_s5, w_branch_attn, w_out, norm2_w, w_up, conv_w, conv_b, w_down):
    B, L, D = x.shape
    n_ctx = ctx.shape[1]
    assert ada_w.shape[0] == 1 and D == D_MODEL and B + 1 <= 8

    cc = jnp.zeros((8, D), F32).at[:B].set(c).at[B].set(c_ctx)
    mod3 = _mod_call(cc, ada_w[0], ada_b).reshape(8, 6, D)

    cos_t, sin_t = _rope_tables(n_ctx, L)
    lane = jnp.arange(LANES)
    seg = jnp.where((lane[:, None] // HEAD_DIM) == (lane[None, :] // HEAD_DIM), 1.0 / HEAD_DIM, 0.0).astype(BF16)
    qw = jnp.tile(q_norm_w[0], 2)[None]
    kw = jnp.tile(k_norm_w[0], 2)[None]
    qT, k, vT, uc, ul, g = _inproj_call(ctx, x, mod3, norm1_w, w_in[0].astype(BF16), qw, kw, b_gate,
                                        cos_t, sin_t, seg)

    o = _attn_call(qT, k, vT, lam_q1, lam_k1, lam_q2, lam_k2, subln_w, tq=256)

    m, bzx, ccx, a16x = _s5tab_call(s5_a_re[0], s5_a_im[0], s5_log_dt[0], s5_b_re[0], s5_b_im[0],
                                    s5_c_re[0], s5_c_im[0], s5_d[0])
    uc16 = uc.reshape(N_SLABS, B, n_ctx // CHUNK, SLAB_IN)
    ul16 = ul.reshape(N_SLABS, B, L // CHUNK, SLAB_IN)
    sin = _s5scan_call(uc16, ul16, bzx, a16x)
    y = _s5out_call(ul16, m, ccx, sin, nb=512).reshape(N_SLABS, B, L, SLAB)

    x1 = _merge_call(y, o, g, x, mod3, glu_w[0].astype(BF16), glu_b, w_branch_s5[0].astype(BF16),
                     w_branch_attn[0].astype(BF16), w_out[0].astype(BF16), tm=512)

    nf = N_FF_CHUNKS
    wup = w_up[0].astype(BF16).reshape(D, 2, nf, FF_CHUNK).transpose(1, 2, 0, 3)
    cw = conv_w[0].reshape(3, 2, nf, FF_CHUNK).transpose(1, 2, 0, 3)
    cb = conv_b[0].reshape(2, nf, 1, FF_CHUNK)
    wdn = w_down[0].astype(BF16).reshape(nf, FF_CHUNK, D)
    return _ffn_call(x1, mod3, norm2_w, wup, cw, cb, wdn, tm=512)
```

```python
import functools
import math

import jax
import jax.numpy as jnp
from jax import lax
from jax.experimental import pallas as pl
from jax.experimental.pallas import tpu as pltpu

D_MODEL = 1024
GRID_W = 64
N_HEADS = 8
HEAD_DIM = 64
V_DIM = 2 * HEAD_DIM
QK_W = N_HEADS * 2 * HEAD_DIM
ATTN_W = N_HEADS * V_DIM
S5_W = 512
S5_GROUP = 16
S5_GROUPS = S5_W // S5_GROUP
S5_STATE = 64
D_FF = 2816
ROPE_THETA = 10000.0
EPS = 1e-6
MIN_NEG_RE = -1e-4
K_OFF = QK_W
V_OFF = 2 * QK_W
U_OFF = 2 * QK_W + ATTN_W
G_OFF = U_OFF + S5_W
N_IN = G_OFF + 2 * D_MODEL
LAM_INIT = 0.8 - 0.6 * math.exp(-0.3 * 0)
Q_SCALE = HEAD_DIM ** -0.5 * math.log2(math.e)

LANES = 128
TOK_TILE = 256
CHUNK = 16
SLAB = LANES
N_SLABS = S5_W // SLAB
GROUPS_PER_SLAB = SLAB // S5_GROUP
SLAB_IN = CHUNK * SLAB
SLAB_STATE = GROUPS_PER_SLAB * 2 * S5_STATE
FF_CHUNK = 256
N_FF_CHUNKS = D_FF // FF_CHUNK
HALO = 8
VMEM_LIMIT = 56 * 1024 * 1024

F32 = jnp.float32
BF16 = jnp.bfloat16


def _sigmoid(x):
    return 1.0 / (1.0 + jnp.exp(-x))


def _dot(a, b):
    return jnp.dot(a, b, preferred_element_type=F32)


def _params(n_parallel, n_arbitrary=0):
    return pltpu.CompilerParams(
        dimension_semantics=("parallel",) * n_parallel + ("arbitrary",) * n_arbitrary,
        vmem_limit_bytes=VMEM_LIMIT)


def _mod_body(c_ref, w_ref, b_ref, o_ref):
    c = c_ref[...]
    s = c * _sigmoid(c)
    o_ref[...] = jnp.dot(s, w_ref[...], preferred_element_type=F32,
                         precision=lax.Precision.HIGHEST) + b_ref[...]


def _mod_call(cc, ada_w, ada_b):
    n = ada_w.shape[1]
    return pl.pallas_call(
        _mod_body,
        grid=(n // D_MODEL,),
        in_specs=[pl.BlockSpec((8, D_MODEL), lambda j: (0, 0)),
                  pl.BlockSpec((D_MODEL, D_MODEL), lambda j: (0, j)),
                  pl.BlockSpec((1, D_MODEL), lambda j: (0, j))],
        out_specs=pl.BlockSpec((8, D_MODEL), lambda j: (0, j)),
        out_shape=jax.ShapeDtypeStruct((8, n), F32),
        compiler_params=_params(1),
        name="mod",
    )(cc, ada_w, ada_b)


def _inproj_body(ctx_ref, x_ref, mod_ref, n1w_ref, w_ref, qw_ref, kw_ref, bg_ref, cos_ref, sin_ref,
                 seg_ref, qT_ref, k_ref, vT_ref, uc_ref, ul_ref, g_ref):
    i = pl.program_id(1)
    is_ctx = i == 0
    xt = jnp.where(is_ctx, ctx_ref[...], x_ref[...])
    ms = jnp.mean(xt * xt, axis=-1, keepdims=True)
    mod = mod_ref[...]
    h = (xt * lax.rsqrt(ms + EPS)) * n1w_ref[...] * (1.0 + mod[1:2]) + mod[0:1]
    hb = h.astype(BF16)

    cos = cos_ref[...]
    sin = sin_ref[...]
    seg = seg_ref[...]
    lane = lax.broadcasted_iota(jnp.int32, cos.shape, 1)
    first_half = (lane % 32) < 16

    def norm_rope(zh, w):
        sq = zh * zh
        hi = sq.astype(BF16)
        lo = (sq - hi.astype(F32)).astype(BF16)
        msq = _dot(hi, seg) + _dot(lo, seg)
        y = zh * lax.rsqrt(msq + EPS) * w
        partner = jnp.where(first_half, pltpu.roll(y, LANES - 16, 1), pltpu.roll(y, 16, 1))
        return y * cos + partner * sin

    zk = _dot(hb, w_ref[:, K_OFF:V_OFF])
    kw = kw_ref[...]
    for hh in range(N_HEADS):
        k_ref[hh] = norm_rope(zk[:, hh * LANES:(hh + 1) * LANES], kw).astype(BF16)

    zv = _dot(hb, w_ref[:, V_OFF:U_OFF])
    for hh in range(N_HEADS):
        vT_ref[hh] = zv[:, hh * LANES:(hh + 1) * LANES].T.astype(BF16)

    zu = _dot(hb, w_ref[:, U_OFF:G_OFF])

    @pl.when(is_ctx)
    def _():
        for s in range(N_SLABS):
            uc_ref[s] = zu[:, s * SLAB:(s + 1) * SLAB].astype(BF16)

    @pl.when(jnp.logical_not(is_ctx))
    def _():
        for s in range(N_SLABS):
            ul_ref[s] = zu[:, s * SLAB:(s + 1) * SLAB].astype(BF16)
        zq = _dot(hb, w_ref[:, 0:K_OFF])
        qw = qw_ref[...]
        row = lax.broadcasted_iota(jnp.int32, (LANES, TOK_TILE), 0)
        for hh in range(N_HEADS):
            q = norm_rope(zq[:, hh * LANES:(hh + 1) * LANES], qw) * Q_SCALE
            qT = q.T
            qT_ref[hh, 0] = jnp.where(row < HEAD_DIM, qT, 0.0).astype(BF16)
            qT_ref[hh, 1] = jnp.where(row >= HEAD_DIM, qT, 0.0).astype(BF16)
        zg = _dot(hb, w_ref[:, G_OFF:N_IN]) + bg_ref[...]
        g_ref[...] = _sigmoid(zg)


def _inproj_call(ctx, x, mod3, n1w, w_in_b, qw, kw, bg, cos_t, sin_t, seg):
    B, L, D = x.shape
    n_ctx = ctx.shape[1]
    assert n_ctx == TOK_TILE and L % TOK_TILE == 0
    nt = L // TOK_TILE
    n_all = n_ctx + L
    T = TOK_TILE
    lat = lambda i: jnp.maximum(i - 1, 0)
    const2 = lambda b, i: (0, 0)
    return pl.pallas_call(
        _inproj_body,
        grid=(B, nt + 1),
        in_specs=[
            pl.BlockSpec((None, T, D), lambda b, i: (b, 0, 0)),
            pl.BlockSpec((None, T, D), lambda b, i: (b, lat(i), 0)),
            pl.BlockSpec((None, 6, D), lambda b, i: (jnp.where(i == 0, B, b), 0, 0)),
            pl.BlockSpec((1, D), const2),
            pl.BlockSpec((D, N_IN), const2),
            pl.BlockSpec((1, LANES), const2),
            pl.BlockSpec((1, LANES), const2),
            pl.BlockSpec((1, 2 * D), const2),
            pl.BlockSpec((T, LANES), lambda b, i: (i, 0)),
            pl.BlockSpec((T, LANES), lambda b, i: (i, 0)),
            pl.BlockSpec((LANES, LANES), const2),
        ],
        out_specs=[
            pl.BlockSpec((None, N_HEADS, 2, LANES, T), lambda b, i: (b, 0, 0, 0, lat(i))),
            pl.BlockSpec((None, N_HEADS, T, LANES), lambda b, i: (b, 0, i, 0)),
            pl.BlockSpec((None, N_HEADS, LANES, T), lambda b, i: (b, 0, 0, i)),
            pl.BlockSpec((N_SLABS, None, T, SLAB), lambda b, i: (0, b, 0, 0)),
            pl.BlockSpec((N_SLABS, None, T, SLAB), lambda b, i: (0, b, lat(i), 0)),
            pl.BlockSpec((None, T, 2 * D), lambda b, i: (b, lat(i), 0)),
        ],
        out_shape=[
            jax.ShapeDtypeStruct((B, N_HEADS, 2, LANES, L), BF16),
            jax.ShapeDtypeStruct((B, N_HEADS, n_all, LANES), BF16),
            jax.ShapeDtypeStruct((B, N_HEADS, LANES, n_all), BF16),
            jax.ShapeDtypeStruct((N_SLABS, B, n_ctx, SLAB), BF16),
            jax.ShapeDtypeStruct((N_SLABS, B, L, SLAB), BF16),
            jax.ShapeDtypeStruct((B, L, 2 * D), F32),
        ],
        compiler_params=_params(1, 1),
        name="in_proj",
    )(ctx, x, mod3, n1w, w_in_b, qw, kw, bg, cos_t, sin_t, seg)


def _attn_body(qT_ref, k_ref, vT_ref, lq1_ref, lk1_ref, lq2_ref, lk2_ref, sw_ref, o_ref,
               s0_ref, s1_ref, m0_ref, m1_ref, *, tq):
    g = pl.program_id(0)
    tk = TOK_TILE
    n_all = k_ref.shape[0]

    @pl.when(g == 0)
    def _():
        s1_ref[...] = jnp.zeros(s1_ref.shape, F32)
        m1_ref[...] = jnp.zeros(m1_ref.shape, F32)

    def step(s_w, m_w, s_r, m_r):
        qT = jnp.concatenate([qT_ref[0], qT_ref[1]], axis=1)
        m_prev = m_r[...]
        m8 = l8 = acc = None
        for lo in range(0, n_all, tk):
            s_new = _dot(k_ref[lo:lo + tk, :], qT)
            s_w[lo:lo + tk, :] = s_new
            part = jnp.max(s_new.reshape(tk // 8, 8, 2 * tq), axis=0)
            m8 = part if m8 is None else jnp.maximum(m8, part)

            p = jnp.exp2(s_r[lo:lo + tk, :] - m_prev)
            part = jnp.sum(p.reshape(tk // 8, 8, 2 * tq), axis=0)
            l8 = part if l8 is None else l8 + part
            pv = _dot(vT_ref[:, lo:lo + tk], p.astype(BF16))
            acc = pv if acc is None else acc + pv
        m_w[...] = jnp.max(m8, axis=0, keepdims=True)

        oT = acc / jnp.sum(l8, axis=0, keepdims=True)
        lam = (jnp.exp(jnp.sum(lq1_ref[...] * lk1_ref[...], axis=-1, keepdims=True))
               - jnp.exp(jnp.sum(lq2_ref[...] * lk2_ref[...], axis=-1, keepdims=True)) + LAM_INIT)
        od = (oT[:, :tq] - lam * oT[:, tq:]).T
        ms = jnp.mean(od * od, axis=-1, keepdims=True)
        o_ref[...] = (od * lax.rsqrt(ms + EPS)) * sw_ref[...] * (1.0 - LAM_INIT)

    @pl.when(g % 2 == 0)
    def _():
        step(s0_ref, m0_ref, s1_ref, m1_ref)

    @pl.when(g % 2 == 1)
    def _():
        step(s1_ref, m1_ref, s0_ref, m0_ref)


def _attn_call(qT, k, vT, lq1, lk1, lq2, lk2, sw, tq):
    B, H, _, _, L = qT.shape
    n_all = k.shape[2]
    nq = L // tq
    n_tiles = B * H * nq

    def tile(t):
        return t // (H * nq), (t // nq) % H, t % nq

    def scored(g):
        return tile(jnp.minimum(g, n_tiles - 1))

    def finished(g):
        return tile(jnp.maximum(g - 1, 0))

    def q_map(g):
        b, h, i = scored(g)
        return b, h, 0, 0, i

    def k_map(g):
        b, h, _ = scored(g)
        return b, h, 0, 0

    def v_map(g):
        b, h, _ = finished(g)
        return b, h, 0, 0

    def o_map(g):
        b, h, i = finished(g)
        return b, i, h

    vec = pl.BlockSpec((1, HEAD_DIM), lambda g: (0, 0))
    return pl.pallas_call(
        functools.partial(_attn_body, tq=tq),
        grid=(n_tiles + 1,),
        in_specs=[
            pl.BlockSpec((None, None, 2, LANES, tq), q_map),
            pl.BlockSpec((None, None, n_all, LANES), k_map),
            pl.BlockSpec((None, None, LANES, n_all), v_map),
            vec, vec, vec, vec,
            pl.BlockSpec((1, V_DIM), lambda g: (0, 0)),
        ],
        out_specs=pl.BlockSpec((None, tq, V_DIM), o_map),
        out_shape=jax.ShapeDtypeStruct((B, L, ATTN_W), F32),
        scratch_shapes=[pltpu.VMEM((n_all, 2 * tq), F32), pltpu.VMEM((n_all, 2 * tq), F32),
                        pltpu.VMEM((1, 2 * tq), F32), pltpu.VMEM((1, 2 * tq), F32)],
        compiler_params=_params(0, 1),
        name="attn",
    )(qT, k, vT, lq1, lk1, lq2, lk2, sw)


def _cmul(xr, xi, yr, yi):
    return xr * yr - xi * yi, xr * yi + xi * yr


def _a_bar(are, aim, ldt):
    lr = jnp.minimum(are, MIN_NEG_RE)
    dt = jnp.exp(ldt)
    mag = jnp.exp(lr * dt)
    return lr, aim, mag * jnp.cos(aim * dt), mag * jnp.sin(aim * dt)


def _powers(ar, ai, n):
    pows = [(jnp.ones_like(ar), jnp.zeros_like(ar))]
    for _ in range(n):
        pows.append(_cmul(pows[-1][0], pows[-1][1], ar, ai))
    return pows


def _s5tab_body(aar_ref, aai_ref, adt_ref, abr_ref, abi_ref, bar_ref, bai_ref, bdt_ref, bcr_ref, bci_ref,
                car_ref, cai_ref, cdt_ref, dsk_ref, m_ref, bz_ref, cc_ref, a16_ref, lag_ref):
    d = pl.program_id(1)
    fwd = d == 0
    half = SLAB_STATE // 2
    sel = lambda a, b: (jnp.where(fwd, a[0], b[0]), jnp.where(fwd, a[1], b[1]))

    lr, li, ar, ai = _a_bar(aar_ref[...], aai_ref[...], adt_ref[...])
    den = lr * lr + li * li
    nr = ar - 1.0
    cfr = (nr * lr + ai * li) / den
    cfi = (ai * lr - nr * li) / den
    bbr, bbi = _cmul(cfr, cfi, abr_ref[...], abi_ref[...])
    pa = _powers(ar, ai, CHUNK - 1)
    row_a = lax.broadcasted_iota(jnp.int32, (SLAB, half), 0) // S5_GROUP
    col_a = lax.broadcasted_iota(jnp.int32, (SLAB, half), 1) // S5_STATE
    diag_a = row_a == col_a

    def expand_a(v):
        return jnp.where(diag_a, jnp.concatenate([v] * (half // LANES), axis=1), 0.0)

    for t in range(CHUNK):
        pw = sel(pa[CHUNK - 1 - t], pa[t])
        zr, zi = _cmul(pw[0], pw[1], bbr, bbi)
        bz_ref[t * SLAB:(t + 1) * SLAB, :] = jnp.concatenate([expand_a(zr), expand_a(zi)], axis=1).astype(BF16)

    _, _, br_, bi_ = _a_bar(bar_ref[...], bai_ref[...], bdt_ref[...])
    pb = _powers(br_, bi_, CHUNK)
    cr = bcr_ref[...]
    ci = bci_ref[...]
    row_b = lax.broadcasted_iota(jnp.int32, (half, SLAB), 0) // S5_STATE
    col_b = lax.broadcasted_iota(jnp.int32, (half, SLAB), 1) // S5_GROUP
    diag_b = row_b == col_b

    def expand_b(v):
        return jnp.where(diag_b, jnp.concatenate([v] * GROUPS_PER_SLAB, axis=0), 0.0)

    for t in range(CHUNK):
        pw = sel(pb[t + 1], pb[CHUNK - t])
        ctr, cti = _cmul(cr, ci, pw[0], pw[1])
        cc_ref[:, t * SLAB:(t + 1) * SLAB] = jnp.concatenate([expand_b(ctr), expand_b(-cti)], axis=0).astype(BF16)

    lane = lax.broadcasted_iota(jnp.int32, (SLAB, LANES), 1)
    lhs = jnp.where(lane < S5_STATE, bbr, bbi)
    r_k = lax.broadcasted_iota(jnp.int32, (SLAB, SLAB), 0)
    c_k = lax.broadcasted_iota(jnp.int32, (SLAB, SLAB), 1)
    diag_k = (r_k // S5_GROUP) == (c_k // S5_GROUP)
    sign = 1 - 2 * d
    for tau in range(CHUNK):
        ctr, cti = _cmul(cr, ci, pb[tau][0], pb[tau][1])
        rhs = jnp.concatenate([ctr, -cti], axis=0)
        ktau = jnp.where(diag_k, jnp.dot(lhs, rhs, preferred_element_type=F32,
                                         precision=lax.Precision.HIGHEST), 0.0)
        if tau == 0:
            @pl.when(fwd)
            def _():
                lag_ref[CHUNK - 1] = ktau + jnp.where(r_k == c_k, dsk_ref[...], 0.0)

            @pl.when(jnp.logical_not(fwd))
            def _():
                lag_ref[CHUNK - 1] = lag_ref[CHUNK - 1] + ktau
        else:
            lag_ref[CHUNK - 1 + sign * tau] = ktau

    @pl.when(jnp.logical_not(fwd))
    def _():
        for tp in range(CHUNK):
            for t in range(CHUNK):
                m_ref[tp * SLAB:(tp + 1) * SLAB, t * SLAB:(t + 1) * SLAB] = lag_ref[t - tp + CHUNK - 1].astype(BF16)

    _, _, xr, xi = _a_bar(car_ref[...], cai_ref[...], cdt_ref[...])
    for _ in range(4):
        xr, xi = _cmul(xr, xi, xr, xi)
    a16_ref[...] = jnp.concatenate([xr, xi], axis=1)


def _s5tab_call(s5_a_re, s5_a_im, s5_log_dt, s5_b_re, s5_b_im, s5_c_re, s5_c_im, s5_d):
    S, GS, C, P = N_SLABS, GROUPS_PER_SLAB, S5_GROUP, S5_STATE
    half = SLAB_STATE // 2

    def lay_a(a):
        return jnp.broadcast_to(a.reshape(2, S, GS, 1, 1, P), (2, S, GS, C, 2, P)).reshape(2, S, SLAB, LANES)

    def lay_b(a):
        a = a.reshape(2, S, GS, P).transpose(0, 1, 3, 2)
        return jnp.broadcast_to(a[..., None], (2, S, P, GS, C)).reshape(2, S, P, SLAB)

    ldt = jnp.broadcast_to(s5_log_dt[..., None], (2, S5_GROUPS, P))
    b_a = lambda b: jnp.broadcast_to(b.transpose(0, 1, 3, 2).reshape(2, S, GS, C, 1, P),
                                     (2, S, GS, C, 2, P)).reshape(2, S, SLAB, LANES)
    c_b = lambda c: c.reshape(2, S, GS, C, P).transpose(0, 1, 4, 2, 3).reshape(2, S, P, SLAB)
    lay_c = lambda a: a.reshape(2, S, 1, half)
    blk = lambda r, c: pl.BlockSpec((None, None, r, c), lambda s, d: (d, s, 0, 0))
    return pl.pallas_call(
        _s5tab_body,
        grid=(S, 2),
        in_specs=[blk(SLAB, LANES)] * 5 + [blk(P, SLAB)] * 5 + [blk(1, half)] * 3
                 + [pl.BlockSpec((None, 1, SLAB), lambda s, d: (s, 0, 0))],
        out_specs=[pl.BlockSpec((None, SLAB_IN, SLAB_IN), lambda s, d: (s, 0, 0)),
                   blk(SLAB_IN, SLAB_STATE), blk(SLAB_STATE, SLAB_IN), blk(1, SLAB_STATE)],
        out_shape=[jax.ShapeDtypeStruct((S, SLAB_IN, SLAB_IN), BF16),
                   jax.ShapeDtypeStruct((2, S, SLAB_IN, SLAB_STATE), BF16),
                   jax.ShapeDtypeStruct((2, S, SLAB_STATE, SLAB_IN), BF16),
                   jax.ShapeDtypeStruct((2, S, 1, SLAB_STATE), F32)],
        scratch_shapes=[pltpu.VMEM((2 * CHUNK - 1, SLAB, SLAB), F32)],
        compiler_params=_params(0, 2),
        name="s5_tab",
    )(lay_a(s5_a_re), lay_a(s5_a_im), lay_a(ldt), b_a(s5_b_re), b_a(s5_b_im),
      lay_b(s5_a_re), lay_b(s5_a_im), lay_b(ldt), c_b(s5_c_re), c_b(s5_c_im),
      lay_c(s5_a_re), lay_c(s5_a_im), lay_c(ldt), s5_d.reshape(S, 1, SLAB))


def _s5scan_body(uc_ref, ul_ref, bz_ref, a_ref, sin_ref, zc_ref, zl_ref, st_ref, *, n_cc, n_lc):
    d = pl.program_id(1)
    B = uc_ref.shape[0]
    w = bz_ref[...]
    zc_ref[...] = _dot(uc_ref[...].reshape(B * n_cc, SLAB_IN), w).reshape(B, n_cc, SLAB_STATE)
    zl_ref[...] = _dot(ul_ref[...].reshape(B * n_lc, SLAB_IN), w).reshape(B, n_lc, SLAB_STATE)
    a = a_ref[...]
    half = SLAB_STATE // 2
    a_same = jnp.concatenate([a[:, :half], a[:, :half]], axis=1)
    a_swap = jnp.concatenate([-a[:, half:], a[:, half:]], axis=1)

    def advance(s, z):
        return a_same * s + a_swap * pltpu.roll(s, half, 1) + z

    def ctx_step(i, s):
        kk = jnp.where(d == 0, i, n_cc - 1 - i)
        return tuple(advance(s[b], zc_ref[b, pl.ds(kk, 1), :]) for b in range(B))

    def lat_step(i, s):
        kk = jnp.where(d == 0, i, n_lc - 1 - i)
        for b in range(B):
            st_ref[b, pl.ds(kk, 1), :] = s[b]
        return tuple(advance(s[b], zl_ref[b, pl.ds(kk, 1), :]) for b in range(B))

    s = lax.fori_loop(0, n_cc, ctx_step, tuple(jnp.zeros((1, SLAB_STATE), F32) for _ in range(B)))
    lax.fori_loop(0, n_lc, lat_step, s)
    sin_ref[...] = st_ref[...].astype(BF16)


def _s5scan_call(uc16, ul16, bzx, a16x):
    S, B, n_cc, _ = uc16.shape
    n_lc = ul16.shape[2]
    return pl.pallas_call(
        functools.partial(_s5scan_body, n_cc=n_cc, n_lc=n_lc),
        grid=(S, 2),
        in_specs=[pl.BlockSpec((None, B, n_cc, SLAB_IN), lambda s, d: (s, 0, 0, 0)),
                  pl.BlockSpec((None, B, n_lc, SLAB_IN), lambda s, d: (s, 0, 0, 0)),
                  pl.BlockSpec((None, None, SLAB_IN, SLAB_STATE), lambda s, d: (d, s, 0, 0)),
                  pl.BlockSpec((None, None, 1, SLAB_STATE), lambda s, d: (d, s, 0, 0))],
        out_specs=pl.BlockSpec((None, None, B, n_lc, SLAB_STATE), lambda s, d: (d, s, 0, 0, 0)),
        out_shape=jax.ShapeDtypeStruct((2, S, B, n_lc, SLAB_STATE), BF16),
        scratch_shapes=[pltpu.VMEM((B, n_cc, SLAB_STATE), F32),
                        pltpu.VMEM((B, n_lc, SLAB_STATE), F32),
                        pltpu.VMEM((B, n_lc, SLAB_STATE), F32)],
        compiler_params=_params(2),
        name="s5_scan",
    )(uc16, ul16, bzx, a16x)


def _s5out_body(ul_ref, m_ref, cc_ref, sin_ref, y_ref):
    B, n_lc, _ = ul_ref.shape
    rows = B * n_lc
    y = _dot(ul_ref[...].reshape(rows, SLAB_IN), m_ref[...])
    y = y + _dot(sin_ref[0].reshape(rows, SLAB_STATE), cc_ref[0])
    y = y + _dot(sin_ref[1].reshape(rows, SLAB_STATE), cc_ref[1])
    y_ref[...] = y.reshape(B, n_lc, y.shape[-1])


def _s5out_call(ul16, m, ccx, sin, nb):
    S, B, n_lc, _ = ul16.shape
    return pl.pallas_call(
        _s5out_body,
        grid=(S, SLAB_IN // nb),
        in_specs=[pl.BlockSpec((None, B, n_lc, SLAB_IN), lambda s, j: (s, 0, 0, 0)),
                  pl.BlockSpec((None, SLAB_IN, nb), lambda s, j: (s, 0, j)),
                  pl.BlockSpec((2, None, SLAB_STATE, nb), lambda s, j: (0, s, 0, j)),
                  pl.BlockSpec((2, None, B, n_lc, SLAB_STATE), lambda s, j: (0, s, 0, 0, 0))],
        out_specs=pl.BlockSpec((None, B, n_lc, nb), lambda s, j: (s, 0, 0, j)),
        out_shape=jax.ShapeDtypeStruct((S, B, n_lc, SLAB_IN), F32),
        compiler_params=_params(2),
        name="s5_out",
    )(ul16, m, ccx, sin)


def _merge_body(y_ref, o_ref, g_ref, x_ref, mod_ref, gw_ref, gb_ref, wbs_ref, wba_ref, wo_ref, out_ref):
    y = jnp.concatenate([y_ref[s] for s in range(N_SLABS)], axis=1)
    hh = 0.5 * y * (1.0 + jnp.tanh(math.sqrt(2.0 / math.pi) * (y + 0.044715 * (y * y * y))))
    gl = _dot(hh.astype(BF16), gw_ref[...]) + gb_ref[...]
    ys = _dot((hh * _sigmoid(gl)).astype(BF16), wbs_ref[...])
    ya = _dot(o_ref[...].astype(BF16), wba_ref[...])
    g = g_ref[...]
    mix = g[:, :D_MODEL] * ys + g[:, D_MODEL:] * ya
    out_ref[...] = x_ref[...] + mod_ref[2:3] * _dot(mix.astype(BF16), wo_ref[...])


def _merge_call(y, o, g, x, mod3, gw, gb, wbs, wba, wo, tm):
    B, L, D = x.shape
    c2 = lambda b, i: (0, 0)
    return pl.pallas_call(
        _merge_body,
        grid=(B, L // tm),
        in_specs=[pl.BlockSpec((N_SLABS, None, tm, SLAB), lambda b, i: (0, b, i, 0)),
                  pl.BlockSpec((None, tm, ATTN_W), lambda b, i: (b, i, 0)),
                  pl.BlockSpec((None, tm, 2 * D), lambda b, i: (b, i, 0)),
                  pl.BlockSpec((None, tm, D), lambda b, i: (b, i, 0)),
                  pl.BlockSpec((None, 6, D), lambda b, i: (b, 0, 0)),
                  pl.BlockSpec((S5_W, S5_W), c2), pl.BlockSpec((1, S5_W), c2),
                  pl.BlockSpec((S5_W, D), c2), pl.BlockSpec((ATTN_W, D), c2), pl.BlockSpec((D, D), c2)],
        out_specs=pl.BlockSpec((None, tm, D), lambda b, i: (b, i, 0)),
        out_shape=jax.ShapeDtypeStruct((B, L, D), F32),
        compiler_params=_params(2),
        name="merge",
    )(y, o, g, x, mod3, gw, gb, wbs, wba, wo)


def _ffn_body(xp_ref, x_ref, xn_ref, mod_ref, n2w_ref, wup_ref, cw_ref, cb_ref, wdn_ref, out_ref, *, tm):
    i = pl.program_id(1)
    last = pl.num_programs(1) - 1
    mod = mod_ref[...]
    n2w = n2w_ref[...]

    def modulated(v):
        ms = jnp.mean(v * v, axis=-1, keepdims=True)
        return (v * lax.rsqrt(ms + EPS)) * n2w * (1.0 + mod[4:5]) + mod[3:4]

    x = x_ref[...]
    hp = modulated(xp_ref[...]) * (i > 0).astype(F32)
    hn = modulated(xn_ref[...]) * (i < last).astype(F32)
    hcat = jnp.concatenate([hp, modulated(x), hn], axis=0).astype(BF16)
    rows = tm + 2 * HALO

    def conv(u, cw, cb):
        prev = pltpu.roll(u, 1, 0)[HALO:HALO + tm]
        nxt = pltpu.roll(u, rows - 1, 0)[HALO:HALO + tm]
        return cb + prev * cw[0:1] + u[HALO:HALO + tm] * cw[1:2] + nxt * cw[2:3]

    def chunk(f, acc):
        ya = conv(_dot(hcat, wup_ref[0, f]), cw_ref[0, f], cb_ref[0, f])
        yg = conv(_dot(hcat, wup_ref[1, f]), cw_ref[1, f], cb_ref[1, f])
        act = (yg * _sigmoid(yg)) * ya
        return acc + _dot(act.astype(BF16), wdn_ref[f])

    acc = lax.fori_loop(0, N_FF_CHUNKS, chunk, jnp.zeros((tm, D_MODEL), F32))
    out_ref[...] = x + mod[5:6] * acc


def _ffn_call(x1, mod3, n2w, wup, cw, cb, wdn, tm):
    B, L, D = x1.shape
    nh = tm // HALO
    n_halo_blocks = L // HALO
    c2 = lambda b, i: (0, 0)
    c3 = lambda b, i: (0, 0, 0)
    c4 = lambda b, i: (0, 0, 0, 0)
    return pl.pallas_call(
        functools.partial(_ffn_body, tm=tm),
        grid=(B, L // tm),
        in_specs=[pl.BlockSpec((None, HALO, D), lambda b, i: (b, jnp.maximum(i * nh - 1, 0), 0)),
                  pl.BlockSpec((None, tm, D), lambda b, i: (b, i, 0)),
                  pl.BlockSpec((None, HALO, D), lambda b, i: (b, jnp.minimum((i + 1) * nh, n_halo_blocks - 1), 0)),
                  pl.BlockSpec((None, 6, D), lambda b, i: (b, 0, 0)),
                  pl.BlockSpec((1, D), c2),
                  pl.BlockSpec((2, N_FF_CHUNKS, D, FF_CHUNK), c4),
                  pl.BlockSpec((2, N_FF_CHUNKS, 3, FF_CHUNK), c4),
                  pl.BlockSpec((2, N_FF_CHUNKS, 1, FF_CHUNK), c4),
                  pl.BlockSpec((N_FF_CHUNKS, FF_CHUNK, D), c3)],
        out_specs=pl.BlockSpec((None, tm, D), lambda b, i: (b, i, 0)),
        out_shape=jax.ShapeDtypeStruct((B, L, D), F32),
        compiler_params=_params(2),
        name="ffn",
    )(x1, x1, x1, mod3, n2w, wup, cw, cb, wdn)


def _rope_tables(n_ctx, L):
    t = jnp.arange(L)
    row = (t // GRID_W).astype(F32)
    col = (t % GRID_W).astype(F32)
    half = HEAD_DIM // 2
    inv_freq = ROPE_THETA ** (-jnp.arange(0, half, 2, dtype=F32) / half)
    ang_r = row[:, None] * inv_freq
    ang_c = col[:, None] * inv_freq

    def lanes(fr, fc, sign):
        per_map = jnp.concatenate([sign * fr, fr, sign * fc, fc], axis=1)
        return jnp.concatenate([per_map, per_map], axis=1)

    cos_t = lanes(jnp.cos(ang_r), jnp.cos(ang_c), 1.0)
    sin_t = lanes(jnp.sin(ang_r), jnp.sin(ang_c), -1.0)
    cos_t = jnp.concatenate([jnp.ones((n_ctx, LANES), F32), cos_t], axis=0)
    sin_t = jnp.concatenate([jnp.zeros((n_ctx, LANES), F32), sin_t], axis=0)
    return cos_t, sin_t


def kernel(x, c, ctx, c_ctx, ada_w, ada_b, norm1_w, w_in, b_gate, q_norm_w, k_norm_w, lam_q1, lam_k1, lam_q2, lam_k2, subln_w, s5_a_re, s5_a_im, s5_log_dt, s5_b_re, s5_b_im, s5_c_re, s5_c_im, s5_d, glu_w, glu_b, w_branch_s5, w_branch_attn, w_out, norm2_w, w_up, conv_w, conv_b, w_down):
    B, L, D = x.shape
    n_ctx = ctx.shape[1]
    assert ada_w.shape[0] == 1 and D == D_MODEL and B + 1 <= 8

    cc = jnp.zeros((8, D), F32).at[:B].set(c).at[B].set(c_ctx)
    mod3 = _mod_call(cc, ada_w[0], ada_b).reshape(8, 6, D)

    cos_t, sin_t = _rope_tables(n_ctx, L)
    lane = jnp.arange(LANES)
    seg = jnp.where((lane[:, None] // HEAD_DIM) == (lane[None, :] // HEAD_DIM), 1.0 / HEAD_DIM, 0.0).astype(BF16)
    qw = jnp.tile(q_norm_w[0], 2)[None]
    kw = jnp.tile(k_norm_w[0], 2)[None]
    qT, k, vT, uc, ul, g = _inproj_call(ctx, x, mod3, norm1_w, w_in[0].astype(BF16), qw, kw, b_gate,
                                        cos_t, sin_t, seg)

    o = _attn_call(qT, k, vT, lam_q1, lam_k1, lam_q2, lam_k2, subln_w, tq=256)

    m, bzx, ccx, a16x = _s5tab_call(s5_a_re[0], s5_a_im[0], s5_log_dt[0], s5_b_re[0], s5_b_im[0],
                                    s5_c_re[0], s5_c_im[0], s5_d[0])
    uc16 = uc.reshape(N_SLABS, B, n_ctx // CHUNK, SLAB_IN)
    ul16 = ul.reshape(N_SLABS, B, L // CHUNK, SLAB_IN)
    sin = _s5scan_call(uc16, ul16, bzx, a16x)
    y = _s5out_call(ul16, m, ccx, sin, nb=512).reshape(N_SLABS, B, L, SLAB)

    x1 = _merge_call(y, o, g, x, mod3, glu_w[0].astype(BF16), glu_b, w_branch_s5[0].astype(BF16),
                     w_branch_attn[0].astype(BF16), w_out[0].astype(BF16), tm=512)

    nf = N_FF_CHUNKS
    wup = w_up[0].astype(BF16).reshape(D, 2, nf, FF_CHUNK).transpose(1, 2, 0, 3)
    cw = conv_w[0].reshape(3, 2, nf, FF_CHUNK).transpose(1, 2, 0, 3)
    cb = conv_b[0].reshape(2, nf, 1, FF_CHUNK)
    wdn = w_down[0].astype(BF16).reshape(nf, FF_CHUNK, D)
    return _ffn_call(x1, mod3, norm2_w, wup, cw, cb, wdn, tm=512)
```

```python
import functools
import math

import jax
import jax.numpy as jnp
from jax import lax
from jax.experimental import pallas as pl
from jax.experimental.pallas import tpu as pltpu

D_MODEL = 1024
GRID_W = 64
N_HEADS = 8
HEAD_DIM = 64
V_DIM = 2 * HEAD_DIM
QK_W = N_HEADS * 2 * HEAD_DIM
ATTN_W = N_HEADS * V_DIM
S5_W = 512
S5_GROUP = 16
S5_GROUPS = S5_W // S5_GROUP
S5_STATE = 64
D_FF = 2816
ROPE_THETA = 10000.0
EPS = 1e-6
MIN_NEG_RE = -1e-4
K_OFF = QK_W
V_OFF = 2 * QK_W
U_OFF = 2 * QK_W + ATTN_W
G_OFF = U_OFF + S5_W
N_IN = G_OFF + 2 * D_MODEL
LAM_INIT = 0.8 - 0.6 * math.exp(-0.3 * 0)
Q_SCALE = HEAD_DIM ** -0.5 * math.log2(math.e)

LANES = 128
TOK_TILE = 256
CHUNK = 16
SLAB = LANES
N_SLABS = S5_W // SLAB
GROUPS_PER_SLAB = SLAB // S5_GROUP
SLAB_IN = CHUNK * SLAB
SLAB_STATE = GROUPS_PER_SLAB * 2 * S5_STATE
FF_CHUNK = 256
N_FF_CHUNKS = D_FF // FF_CHUNK
HALO = 8
VMEM_LIMIT = 56 * 1024 * 1024

F32 = jnp.float32
BF16 = jnp.bfloat16


def _sigmoid(x):
    return 1.0 / (1.0 + jnp.exp(-x))


def _dot(a, b):
    return jnp.dot(a, b, preferred_element_type=F32)


def _params(n_parallel, n_arbitrary=0):
    return pltpu.CompilerParams(
        dimension_semantics=("parallel",) * n_parallel + ("arbitrary",) * n_arbitrary,
        vmem_limit_bytes=VMEM_LIMIT)


def _mod_body(c_ref, w_ref, b_ref, o_ref):
    c = c_ref[...]
    s = c * _sigmoid(c)
    o_ref[...] = jnp.dot(s, w_ref[...], preferred_element_type=F32,
                         precision=lax.Precision.HIGHEST) + b_ref[...]


def _mod_call(cc, ada_w, ada_b):
    n = ada_w.shape[1]
    return pl.pallas_call(
        _mod_body,
        grid=(n // D_MODEL,),
        in_specs=[pl.BlockSpec((8, D_MODEL), lambda j: (0, 0)),
                  pl.BlockSpec((D_MODEL, D_MODEL), lambda j: (0, j)),
                  pl.BlockSpec((1, D_MODEL), lambda j: (0, j))],
        out_specs=pl.BlockSpec((8, D_MODEL), lambda j: (0, j)),
        out_shape=jax.ShapeDtypeStruct((8, n), F32),
        compiler_params=_params(1),
        name="mod",
    )(cc, ada_w, ada_b)


def _inproj_body(ctx_ref, x_ref, mod_ref, n1w_ref, w_ref, qw_ref, kw_ref, bg_ref, cos_ref, sin_ref,
                 seg_ref, qT_ref, k_ref, vT_ref, uc_ref, ul_ref, g_ref, u_scr):
    i = pl.program_id(1)
    is_ctx = i == 0
    xt = jnp.where(is_ctx, ctx_ref[...], x_ref[...])
    ms = jnp.mean(xt * xt, axis=-1, keepdims=True)
    mod = mod_ref[...]
    h = (xt * lax.rsqrt(ms + EPS)) * n1w_ref[...] * (1.0 + mod[1:2]) + mod[0:1]
    hb = h.astype(BF16)

    cos = cos_ref[...]
    sin = sin_ref[...]
    seg = seg_ref[...]
    lane = lax.broadcasted_iota(jnp.int32, cos.shape, 1)
    first_half = (lane % 32) < 16

    def norm_rope(zh, w):
        sq = zh * zh
        hi = sq.astype(BF16)
        lo = (sq - hi.astype(F32)).astype(BF16)
        msq = _dot(hi, seg) + _dot(lo, seg)
        y = zh * lax.rsqrt(msq + EPS) * w
        partner = jnp.where(first_half, pltpu.roll(y, LANES - 16, 1), pltpu.roll(y, 16, 1))
        return y * cos + partner * sin

    zk = _dot(hb, w_ref[:, K_OFF:V_OFF])
    kw = kw_ref[...]
    for hh in range(N_HEADS):
        k_ref[hh] = norm_rope(zk[:, hh * LANES:(hh + 1) * LANES], kw).astype(BF16)

    zv = _dot(hb, w_ref[:, V_OFF:U_OFF])
    for hh in range(N_HEADS):
        vT_ref[hh] = zv[:, hh * LANES:(hh + 1) * LANES].T.astype(BF16)

    zu = _dot(hb, w_ref[:, U_OFF:G_OFF])
    for s in range(N_SLABS):
        u_scr[s] = zu[:, s * SLAB:(s + 1) * SLAB]

    def store_u(dst_ref):
        for s in range(N_SLABS):
            for t in range(CHUNK):
                piece = u_scr[s, pl.ds(t, TOK_TILE // CHUNK, stride=CHUNK), :]
                dst_ref[s, :, t * SLAB:(t + 1) * SLAB] = piece.astype(BF16)

    @pl.when(is_ctx)
    def _():
        store_u(uc_ref)

    @pl.when(jnp.logical_not(is_ctx))
    def _():
        store_u(ul_ref)
        zq = _dot(hb, w_ref[:, 0:K_OFF])
        qw = qw_ref[...]
        row = lax.broadcasted_iota(jnp.int32, (LANES, TOK_TILE), 0)
        for hh in range(N_HEADS):
            q = norm_rope(zq[:, hh * LANES:(hh + 1) * LANES], qw) * Q_SCALE
            qT = q.T
            qT_ref[hh, 0] = jnp.where(row < HEAD_DIM, qT, 0.0).astype(BF16)
            qT_ref[hh, 1] = jnp.where(row >= HEAD_DIM, qT, 0.0).astype(BF16)
        zg = _dot(hb, w_ref[:, G_OFF:N_IN]) + bg_ref[...]
        g_ref[...] = _sigmoid(zg)


def _inproj_call(ctx, x, mod3, n1w, w_in_b, qw, kw, bg, cos_t, sin_t, seg):
    B, L, D = x.shape
    n_ctx = ctx.shape[1]
    assert n_ctx == TOK_TILE and L % TOK_TILE == 0
    nt = L // TOK_TILE
    n_all = n_ctx + L
    T = TOK_TILE
    lat = lambda i: jnp.maximum(i - 1, 0)
    const2 = lambda b, i: (0, 0)
    return pl.pallas_call(
        _inproj_body,
        grid=(B, nt + 1),
        in_specs=[
            pl.BlockSpec((None, T, D), lambda b, i: (b, 0, 0)),
            pl.BlockSpec((None, T, D), lambda b, i: (b, lat(i), 0)),
            pl.BlockSpec((None, 6, D), lambda b, i: (jnp.where(i == 0, B, b), 0, 0)),
            pl.BlockSpec((1, D), const2),
            pl.BlockSpec((D, N_IN), const2),
            pl.BlockSpec((1, LANES), const2),
            pl.BlockSpec((1, LANES), const2),
            pl.BlockSpec((1, 2 * D), const2),
            pl.BlockSpec((T, LANES), lambda b, i: (i, 0)),
            pl.BlockSpec((T, LANES), lambda b, i: (i, 0)),
            pl.BlockSpec((LANES, LANES), const2),
        ],
        out_specs=[
            pl.BlockSpec((None, N_HEADS, 2, LANES, T), lambda b, i: (b, 0, 0, 0, lat(i))),
            pl.BlockSpec((None, N_HEADS, T, LANES), lambda b, i: (b, 0, i, 0)),
            pl.BlockSpec((None, N_HEADS, LANES, T), lambda b, i: (b, 0, 0, i)),
            pl.BlockSpec((N_SLABS, None, T // CHUNK, SLAB_IN), lambda b, i: (0, b, 0, 0)),
            pl.BlockSpec((N_SLABS, None, T // CHUNK, SLAB_IN), lambda b, i: (0, b, lat(i), 0)),
            pl.BlockSpec((None, T, 2 * D), lambda b, i: (b, lat(i), 0)),
        ],
        out_shape=[
            jax.ShapeDtypeStruct((B, N_HEADS, 2, LANES, L), BF16),
            jax.ShapeDtypeStruct((B, N_HEADS, n_all, LANES), BF16),
            jax.ShapeDtypeStruct((B, N_HEADS, LANES, n_all), BF16),
            jax.ShapeDtypeStruct((N_SLABS, B, n_ctx // CHUNK, SLAB_IN), BF16),
            jax.ShapeDtypeStruct((N_SLABS, B, L // CHUNK, SLAB_IN), BF16),
            jax.ShapeDtypeStruct((B, L, 2 * D), F32),
        ],
        scratch_shapes=[pltpu.VMEM((N_SLABS, T, SLAB), F32)],
        compiler_params=_params(1, 1),
        name="in_proj",
    )(ctx, x, mod3, n1w, w_in_b, qw, kw, bg, cos_t, sin_t, seg)


def _attn_body(qT_ref, k_ref, vT_ref, lq1_ref, lk1_ref, lq2_ref, lk2_ref, sw_ref, o_ref,
               s0_ref, s1_ref, m0_ref, m1_ref, *, tq):
    g = pl.program_id(0)
    tk = TOK_TILE
    n_all = k_ref.shape[0]

    @pl.when(g == 0)
    def _():
        s1_ref[...] = jnp.zeros(s1_ref.shape, F32)
        m1_ref[...] = jnp.zeros(m1_ref.shape, F32)

    def step(s_w, m_w, s_r, m_r):
        qT = jnp.concatenate([qT_ref[0], qT_ref[1]], axis=1)
        m_prev = m_r[...]
        m8 = l8 = acc = None
        for lo in range(0, n_all, tk):
            s_new = _dot(k_ref[lo:lo + tk, :], qT)
            s_w[lo:lo + tk, :] = s_new
            part = jnp.max(s_new.reshape(tk // 8, 8, 2 * tq), axis=0)
            m8 = part if m8 is None else jnp.maximum(m8, part)

            p = jnp.exp2(s_r[lo:lo + tk, :] - m_prev)
            part = jnp.sum(p.reshape(tk // 8, 8, 2 * tq), axis=0)
            l8 = part if l8 is None else l8 + part
            pv = _dot(vT_ref[:, lo:lo + tk], p.astype(BF16))
            acc = pv if acc is None else acc + pv
        m_w[...] = jnp.max(m8, axis=0, keepdims=True)

        oT = acc / jnp.sum(l8, axis=0, keepdims=True)
        lam = (jnp.exp(jnp.sum(lq1_ref[...] * lk1_ref[...], axis=-1, keepdims=True))
               - jnp.exp(jnp.sum(lq2_ref[...] * lk2_ref[...], axis=-1, keepdims=True)) + LAM_INIT)
        od = (oT[:, :tq] - lam * oT[:, tq:]).T
        ms = jnp.mean(od * od, axis=-1, keepdims=True)
        o_ref[...] = (od * lax.rsqrt(ms + EPS)) * sw_ref[...] * (1.0 - LAM_INIT)

    @pl.when(g % 2 == 0)
    def _():
        step(s0_ref, m0_ref, s1_ref, m1_ref)

    @pl.when(g % 2 == 1)
    def _():
        step(s1_ref, m1_ref, s0_ref, m0_ref)


def _attn_call(qT, k, vT, lq1, lk1, lq2, lk2, sw, tq):
    B, H, _, _, L = qT.shape
    n_all = k.shape[2]
    nq = L // tq
    n_tiles = B * H * nq

    def tile(t):
        return t // (H * nq), (t // nq) % H, t % nq

    def scored(g):
        return tile(jnp.minimum(g, n_tiles - 1))

    def finished(g):
        return tile(jnp.maximum(g - 1, 0))

    def q_map(g):
        b, h, i = scored(g)
        return b, h, 0, 0, i

    def k_map(g):
        b, h, _ = scored(g)
        return b, h, 0, 0

    def v_map(g):
        b, h, _ = finished(g)
        return b, h, 0, 0

    def o_map(g):
        b, h, i = finished(g)
        return b, i, h

    vec = pl.BlockSpec((1, HEAD_DIM), lambda g: (0, 0))
    return pl.pallas_call(
        functools.partial(_attn_body, tq=tq),
        grid=(n_tiles + 1,),
        in_specs=[
            pl.BlockSpec((None, None, 2, LANES, tq), q_map),
            pl.BlockSpec((None, None, n_all, LANES), k_map),
            pl.BlockSpec((None, None, LANES, n_all), v_map),
            vec, vec, vec, vec,
            pl.BlockSpec((1, V_DIM), lambda g: (0, 0)),
        ],
        out_specs=pl.BlockSpec((None, tq, V_DIM), o_map),
        out_shape=jax.ShapeDtypeStruct((B, L, ATTN_W), F32),
        scratch_shapes=[pltpu.VMEM((n_all, 2 * tq), F32), pltpu.VMEM((n_all, 2 * tq), F32),
                        pltpu.VMEM((1, 2 * tq), F32), pltpu.VMEM((1, 2 * tq), F32)],
        compiler_params=_params(0, 1),
        name="attn",
    )(qT, k, vT, lq1, lk1, lq2, lk2, sw)


def _cmul(xr, xi, yr, yi):
    return xr * yr - xi * yi, xr * yi + xi * yr


def _a_bar(are, aim, ldt):
    lr = jnp.minimum(are, MIN_NEG_RE)
    dt = jnp.exp(ldt)
    mag = jnp.exp(lr * dt)
    return lr, aim, mag * jnp.cos(aim * dt), mag * jnp.sin(aim * dt)


def _powers(ar, ai, n):
    pows = [(jnp.ones_like(ar), jnp.zeros_like(ar))]
    for _ in range(n):
        pows.append(_cmul(pows[-1][0], pows[-1][1], ar, ai))
    return pows


def _s5tab_body(aar_ref, aai_ref, adt_ref, abr_ref, abi_ref, bar_ref, bai_ref, bdt_ref, bcr_ref, bci_ref,
                car_ref, cai_ref, cdt_ref, dsk_ref, m_ref, bz_ref, cc_ref, a16_ref, lag_ref):
    d = pl.program_id(1)
    fwd = d == 0
    half = SLAB_STATE // 2
    sel = lambda a, b: (jnp.where(fwd, a[0], b[0]), jnp.where(fwd, a[1], b[1]))

    lr, li, ar, ai = _a_bar(aar_ref[...], aai_ref[...], adt_ref[...])
    den = lr * lr + li * li
    nr = ar - 1.0
    cfr = (nr * lr + ai * li) / den
    cfi = (ai * lr - nr * li) / den
    bbr, bbi = _cmul(cfr, cfi, abr_ref[...], abi_ref[...])
    pa = _powers(ar, ai, CHUNK - 1)
    row_a = lax.broadcasted_iota(jnp.int32, (SLAB, half), 0) // S5_GROUP
    col_a = lax.broadcasted_iota(jnp.int32, (SLAB, half), 1) // S5_STATE
    diag_a = row_a == col_a

    def expand_a(v):
        return jnp.where(diag_a, jnp.concatenate([v] * (half // LANES), axis=1), 0.0)

    for t in range(CHUNK):
        pw = sel(pa[CHUNK - 1 - t], pa[t])
        zr, zi = _cmul(pw[0], pw[1], bbr, bbi)
        bz_ref[t * SLAB:(t + 1) * SLAB, :] = jnp.concatenate([expand_a(zr), expand_a(zi)], axis=1).astype(BF16)

    _, _, br_, bi_ = _a_bar(bar_ref[...], bai_ref[...], bdt_ref[...])
    pb = _powers(br_, bi_, CHUNK)
    cr = bcr_ref[...]
    ci = bci_ref[...]
    row_b = lax.broadcasted_iota(jnp.int32, (half, SLAB), 0) // S5_STATE
    col_b = lax.broadcasted_iota(jnp.int32, (half, SLAB), 1) // S5_GROUP
    diag_b = row_b == col_b

    def expand_b(v):
        return jnp.where(diag_b, jnp.concatenate([v] * GROUPS_PER_SLAB, axis=0), 0.0)

    for t in range(CHUNK):
        pw = sel(pb[t + 1], pb[CHUNK - t])
        ctr, cti = _cmul(cr, ci, pw[0], pw[1])
        cc_ref[:, t * SLAB:(t + 1) * SLAB] = jnp.concatenate([expand_b(ctr), expand_b(-cti)], axis=0).astype(BF16)

    lane = lax.broadcasted_iota(jnp.int32, (SLAB, LANES), 1)
    lhs = jnp.where(lane < S5_STATE, bbr, bbi)
    r_k = lax.broadcasted_iota(jnp.int32, (SLAB, SLAB), 0)
    c_k = lax.broadcasted_iota(jnp.int32, (SLAB, SLAB), 1)
    diag_k = (r_k // S5_GROUP) == (c_k // S5_GROUP)
    sign = 1 - 2 * d
    for tau in range(CHUNK):
        ctr, cti = _cmul(cr, ci, pb[tau][0], pb[tau][1])
        rhs = jnp.concatenate([ctr, -cti], axis=0)
        ktau = jnp.where(diag_k, jnp.dot(lhs, rhs, preferred_element_type=F32,
                                         precision=lax.Precision.HIGHEST), 0.0)
        if tau == 0:
            @pl.when(fwd)
            def _():
                lag_ref[CHUNK - 1] = ktau + jnp.where(r_k == c_k, dsk_ref[...], 0.0)

            @pl.when(jnp.logical_not(fwd))
            def _():
                lag_ref[CHUNK - 1] = lag_ref[CHUNK - 1] + ktau
        else:
            lag_ref[CHUNK - 1 + sign * tau] = ktau

    @pl.when(jnp.logical_not(fwd))
    def _():
        for tp in range(CHUNK):
            for t in range(CHUNK):
                m_ref[tp * SLAB:(tp + 1) * SLAB, t * SLAB:(t + 1) * SLAB] = lag_ref[t - tp + CHUNK - 1].astype(BF16)

    _, _, xr, xi = _a_bar(car_ref[...], cai_ref[...], cdt_ref[...])
    for _ in range(4):
        xr, xi = _cmul(xr, xi, xr, xi)
    a16_ref[...] = jnp.concatenate([xr, xi], axis=1)


def _s5tab_call(s5_a_re, s5_a_im, s5_log_dt, s5_b_re, s5_b_im, s5_c_re, s5_c_im, s5_d):
    S, GS, C, P = N_SLABS, GROUPS_PER_SLAB, S5_GROUP, S5_STATE
    half = SLAB_STATE // 2

    def lay_a(a):
        return jnp.broadcast_to(a.reshape(2, S, GS, 1, 1, P), (2, S, GS, C, 2, P)).reshape(2, S, SLAB, LANES)

    def lay_b(a):
        a = a.reshape(2, S, GS, P).transpose(0, 1, 3, 2)
        return jnp.broadcast_to(a[..., None], (2, S, P, GS, C)).reshape(2, S, P, SLAB)

    ldt = jnp.broadcast_to(s5_log_dt[..., None], (2, S5_GROUPS, P))
    b_a = lambda b: jnp.broadcast_to(b.transpose(0, 1, 3, 2).reshape(2, S, GS, C, 1, P),
                                     (2, S, GS, C, 2, P)).reshape(2, S, SLAB, LANES)
    c_b = lambda c: c.reshape(2, S, GS, C, P).transpose(0, 1, 4, 2, 3).reshape(2, S, P, SLAB)
    lay_c = lambda a: a.reshape(2, S, 1, half)
    blk = lambda r, c: pl.BlockSpec((None, None, r, c), lambda s, d: (d, s, 0, 0))
    return pl.pallas_call(
        _s5tab_body,
        grid=(S, 2),
        in_specs=[blk(SLAB, LANES)] * 5 + [blk(P, SLAB)] * 5 + [blk(1, half)] * 3
                 + [pl.BlockSpec((None, 1, SLAB), lambda s, d: (s, 0, 0))],
        out_specs=[pl.BlockSpec((None, SLAB_IN, SLAB_IN), lambda s, d: (s, 0, 0)),
                   blk(SLAB_IN, SLAB_STATE), blk(SLAB_STATE, SLAB_IN), blk(1, SLAB_STATE)],
        out_shape=[jax.ShapeDtypeStruct((S, SLAB_IN, SLAB_IN), BF16),
                   jax.ShapeDtypeStruct((2, S, SLAB_IN, SLAB_STATE), BF16),
                   jax.ShapeDtypeStruct((2, S, SLAB_STATE, SLAB_IN), BF16),
                   jax.ShapeDtypeStruct((2, S, 1, SLAB_STATE), F32)],
        scratch_shapes=[pltpu.VMEM((2 * CHUNK - 1, SLAB, SLAB), F32)],
        compiler_params=_params(0, 2),
        name="s5_tab",
    )(lay_a(s5_a_re), lay_a(s5_a_im), lay_a(ldt), b_a(s5_b_re), b_a(s5_b_im),
      lay_b(s5_a_re), lay_b(s5_a_im), lay_b(ldt), c_b(s5_c_re), c_b(s5_c_im),
      lay_c(s5_a_re), lay_c(s5_a_im), lay_c(ldt), s5_d.reshape(S, 1, SLAB))


def _s5scan_body(uc_ref, ul_ref, bz_ref, a_ref, sin_ref, zc_ref, zl_ref, st_ref, *, n_cc, n_lc):
    d = pl.program_id(1)
    B = uc_ref.shape[0]
    w = bz_ref[...]
    zc_ref[...] = _dot(uc_ref[...].reshape(B * n_cc, SLAB_IN), w).reshape(B, n_cc, SLAB_STATE)
    zl_ref[...] = _dot(ul_ref[...].reshape(B * n_lc, SLAB_IN), w).reshape(B, n_lc, SLAB_STATE)
    a = a_ref[...]
    half = SLAB_STATE // 2
    a_same = jnp.concatenate([a[:, :half], a[:, :half]], axis=1)
    a_swap = jnp.concatenate([-a[:, half:], a[:, half:]], axis=1)

    def advance(s, z):
        return a_same * s + a_swap * pltpu.roll(s, half, 1) + z

    def ctx_step(i, s):
        kk = jnp.where(d == 0, i, n_cc - 1 - i)
        return tuple(advance(s[b], zc_ref[b, pl.ds(kk, 1), :]) for b in range(B))

    def lat_step(i, s):
        kk = jnp.where(d == 0, i, n_lc - 1 - i)
        for b in range(B):
            st_ref[b, pl.ds(kk, 1), :] = s[b]
        return tuple(advance(s[b], zl_ref[b, pl.ds(kk, 1), :]) for b in range(B))

    s = lax.fori_loop(0, n_cc, ctx_step, tuple(jnp.zeros((1, SLAB_STATE), F32) for _ in range(B)))
    lax.fori_loop(0, n_lc, lat_step, s)
    sin_ref[...] = st_ref[...].astype(BF16)


def _s5scan_call(uc16, ul16, bzx, a16x):
    S, B, n_cc, _ = uc16.shape
    n_lc = ul16.shape[2]
    return pl.pallas_call(
        functools.partial(_s5scan_body, n_cc=n_cc, n_lc=n_lc),
        grid=(S, 2),
        in_specs=[pl.BlockSpec((None, B, n_cc, SLAB_IN), lambda s, d: (s, 0, 0, 0)),
                  pl.BlockSpec((None, B, n_lc, SLAB_IN), lambda s, d: (s, 0, 0, 0)),
                  pl.BlockSpec((None, None, SLAB_IN, SLAB_STATE), lambda s, d: (d, s, 0, 0)),
                  pl.BlockSpec((None, None, 1, SLAB_STATE), lambda s, d: (d, s, 0, 0))],
        out_specs=pl.BlockSpec((None, None, B, n_lc, SLAB_STATE), lambda s, d: (d, s, 0, 0, 0)),
        out_shape=jax.ShapeDtypeStruct((2, S, B, n_lc, SLAB_STATE), BF16),
        scratch_shapes=[pltpu.VMEM((B, n_cc, SLAB_STATE), F32),
                        pltpu.VMEM((B, n_lc, SLAB_STATE), F32),
                        pltpu.VMEM((B, n_lc, SLAB_STATE), F32)],
        compiler_params=_params(2),
        name="s5_scan",
    )(uc16, ul16, bzx, a16x)


def _s5out_body(ul_ref, m_ref, cc_ref, sin_ref, y_ref):
    B, n_lc, _ = ul_ref.shape
    rows = B * n_lc
    y = _dot(ul_ref[...].reshape(rows, SLAB_IN), m_ref[...])
    y = y + _dot(sin_ref[0].reshape(rows, SLAB_STATE), cc_ref[0])
    y = y + _dot(sin_ref[1].reshape(rows, SLAB_STATE), cc_ref[1])
    y_ref[...] = y.reshape(B, n_lc, y.shape[-1])


def _s5out_call(ul16, m, ccx, sin, nb):
    S, B, n_lc, _ = ul16.shape
    return pl.pallas_call(
        _s5out_body,
        grid=(S, SLAB_IN // nb),
        in_specs=[pl.BlockSpec((None, B, n_lc, SLAB_IN), lambda s, j: (s, 0, 0, 0)),
                  pl.BlockSpec((None, SLAB_IN, nb), lambda s, j: (s, 0, j)),
                  pl.BlockSpec((2, None, SLAB_STATE, nb), lambda s, j: (0, s, 0, j)),
                  pl.BlockSpec((2, None, B, n_lc, SLAB_STATE), lambda s, j: (0, s, 0, 0, 0))],
        out_specs=pl.BlockSpec((None, B, n_lc, nb), lambda s, j: (s, 0, 0, j)),
        out_shape=jax.ShapeDtypeStruct((S, B, n_lc, SLAB_IN), F32),
        compiler_params=_params(2),
        name="s5_out",
    )(ul16, m, ccx, sin)


def _merge_body(y_ref, o_ref, g_ref, x_ref, mod_ref, gw_ref, gb_ref, wbs_ref, wba_ref, wo_ref, out_ref, y_scr):
    n_rows = y_ref.shape[1]
    for s in range(N_SLABS):
        for t in range(CHUNK):
            y_scr[s, pl.ds(t, n_rows, stride=CHUNK), :] = y_ref[s, :, t * SLAB:(t + 1) * SLAB]
    y = jnp.concatenate([y_scr[s] for s in range(N_SLABS)], axis=1)
    hh = 0.5 * y * (1.0 + jnp.tanh(math.sqrt(2.0 / math.pi) * (y + 0.044715 * (y * y * y))))
    gl = _dot(hh.astype(BF16), gw_ref[...]) + gb_ref[...]
    ys = _dot((hh * _sigmoid(gl)).astype(BF16), wbs_ref[...])
    ya = _dot(o_ref[...].astype(BF16), wba_ref[...])
    g = g_ref[...]
    mix = g[:, :D_MODEL] * ys + g[:, D_MODEL:] * ya
    out_ref[...] = x_ref[...] + mod_ref[2:3] * _dot(mix.astype(BF16), wo_ref[...])


def _merge_call(y, o, g, x, mod3, gw, gb, wbs, wba, wo, tm):
    B, L, D = x.shape
    c2 = lambda b, i: (0, 0)
    return pl.pallas_call(
        _merge_body,
        grid=(B, L // tm),
        in_specs=[pl.BlockSpec((N_SLABS, None, tm // CHUNK, SLAB_IN), lambda b, i: (0, b, i, 0)),
                  pl.BlockSpec((None, tm, ATTN_W), lambda b, i: (b, i, 0)),
                  pl.BlockSpec((None, tm, 2 * D), lambda b, i: (b, i, 0)),
                  pl.BlockSpec((None, tm, D), lambda b, i: (b, i, 0)),
                  pl.BlockSpec((None, 6, D), lambda b, i: (b, 0, 0)),
                  pl.BlockSpec((S5_W, S5_W), c2), pl.BlockSpec((1, S5_W), c2),
                  pl.BlockSpec((S5_W, D), c2), pl.BlockSpec((ATTN_W, D), c2), pl.BlockSpec((D, D), c2)],
        out_specs=pl.BlockSpec((None, tm, D), lambda b, i: (b, i, 0)),
        out_shape=jax.ShapeDtypeStruct((B, L, D), F32),
        scratch_shapes=[pltpu.VMEM((N_SLABS, tm, SLAB), F32)],
        compiler_params=_params(2),
        name="merge",
    )(y, o, g, x, mod3, gw, gb, wbs, wba, wo)


def _ffn_body(xp_ref, x_ref, xn_ref, mod_ref, n2w_ref, wup_ref, cw_ref, cb_ref, wdn_ref, out_ref, *, tm):
    i = pl.program_id(1)
    last = pl.num_programs(1) - 1
    mod = mod_ref[...]
    n2w = n2w_ref[...]

    def modulated(v):
        ms = jnp.mean(v * v, axis=-1, keepdims=True)
        return (v * lax.rsqrt(ms + EPS)) * n2w * (1.0 + mod[4:5]) + mod[3:4]

    x = x_ref[...]
    hp = modulated(xp_ref[...]) * (i > 0).astype(F32)
    hn = modulated(xn_ref[...]) * (i < last).astype(F32)
    hcat = jnp.concatenate([hp, modulated(x), hn], axis=0).astype(BF16)
    rows = tm + 2 * HALO

    def conv(u, cw, cb):
        prev = pltpu.roll(u, 1, 0)[HALO:HALO + tm]
        nxt = pltpu.roll(u, rows - 1, 0)[HALO:HALO + tm]
        return cb + prev * cw[0:1] + u[HALO:HALO + tm] * cw[1:2] + nxt * cw[2:3]

    ups, acts, acc = {}, {}, None
    for f in range(N_FF_CHUNKS + 2):
        if f < N_FF_CHUNKS:
            ups[f] = (_dot(hcat, wup_ref[0, f]), _dot(hcat, wup_ref[1, f]))
        e = f - 1
        if 0 <= e < N_FF_CHUNKS:
            ua, ug = ups.pop(e)
            ya = conv(ua, cw_ref[0, e], cb_ref[0, e])
            yg = conv(ug, cw_ref[1, e], cb_ref[1, e])
            acts[e] = ((yg * _sigmoid(yg)) * ya).astype(BF16)
        e = f - 2
        if 0 <= e < N_FF_CHUNKS:
            down = _dot(acts.pop(e), wdn_ref[e])
            acc = down if acc is None else acc + down
    out_ref[...] = x + mod[5:6] * acc


def _ffn_call(x1, mod3, n2w, wup, cw, cb, wdn, tm):
    B, L, D = x1.shape
    nh = tm // HALO
    n_halo_blocks = L // HALO
    c2 = lambda b, i: (0, 0)
    c3 = lambda b, i: (0, 0, 0)
    c4 = lambda b, i: (0, 0, 0, 0)
    return pl.pallas_call(
        functools.partial(_ffn_body, tm=tm),
        grid=(B, L // tm),
        in_specs=[pl.BlockSpec((None, HALO, D), lambda b, i: (b, jnp.maximum(i * nh - 1, 0), 0)),
                  pl.BlockSpec((None, tm, D), lambda b, i: (b, i, 0)),
                  pl.BlockSpec((None, HALO, D), lambda b, i: (b, jnp.minimum((i + 1) * nh, n_halo_blocks - 1), 0)),
                  pl.BlockSpec((None, 6, D), lambda b, i: (b, 0, 0)),
                  pl.BlockSpec((1, D), c2),
                  pl.BlockSpec((2, N_FF_CHUNKS, D, FF_CHUNK), c4),
                  pl.BlockSpec((2, N_FF_CHUNKS, 3, FF_CHUNK), c4),
                  pl.BlockSpec((2, N_FF_CHUNKS, 1, FF_CHUNK), c4),
                  pl.BlockSpec((N_FF_CHUNKS, FF_CHUNK, D), c3)],
        out_specs=pl.BlockSpec((None, tm, D), lambda b, i: (b, i, 0)),
        out_shape=jax.ShapeDtypeStruct((B, L, D), F32),
        compiler_params=_params(2),
        name="ffn",
    )(x1, x1, x1, mod3, n2w, wup, cw, cb, wdn)


def _rope_tables(n_ctx, L):
    t = jnp.arange(L)
    row = (t // GRID_W).astype(F32)
    col = (t % GRID_W).astype(F32)
    half = HEAD_DIM // 2
    inv_freq = ROPE_THETA ** (-jnp.arange(0, half, 2, dtype=F32) / half)
    ang_r = row[:, None] * inv_freq
    ang_c = col[:, None] * inv_freq

    def lanes(fr, fc, sign):
        per_map = jnp.concatenate([sign * fr, fr, sign * fc, fc], axis=1)
        return jnp.concatenate([per_map, per_map], axis=1)

    cos_t = lanes(jnp.cos(ang_r), jnp.cos(ang_c), 1.0)
    sin_t = lanes(jnp.sin(ang_r), jnp.sin(ang_c), -1.0)
    cos_t = jnp.concatenate([jnp.ones((n_ctx, LANES), F32), cos_t], axis=0)
    sin_t = jnp.concatenate([jnp.zeros((n_ctx, LANES), F32), sin_t], axis=0)
    return cos_t, sin_t


def kernel(x, c, ctx, c_ctx, ada_w, ada_b, norm1_w, w_in, b_gate, q_norm_w, k_norm_w, lam_q1, lam_k1, lam_q2, lam_k2, subln_w, s5_a_re, s5_a_im, s5_log_dt, s5_b_re, s5_b_im, s5_c_re, s5_c_im, s5_d, glu_w, glu_b, w_branch_s5, w_branch_attn, w_out, norm2_w, w_up, conv_w, conv_b, w_down):
    B, L, D = x.shape
    n_ctx = ctx.shape[1]
    assert ada_w.shape[0] == 1 and D == D_MODEL and B + 1 <= 8

    cc = jnp.zeros((8, D), F32).at[:B].set(c).at[B].set(c_ctx)
    mod3 = _mod_call(cc, ada_w[0], ada_b).reshape(8, 6, D)

    cos_t, sin_t = _rope_tables(n_ctx, L)
    lane = jnp.arange(LANES)
    seg = jnp.where((lane[:, None] // HEAD_DIM) == (lane[None, :] // HEAD_DIM), 1.0 / HEAD_DIM, 0.0).astype(BF16)
    qw = jnp.tile(q_norm_w[0], 2)[None]
    kw = jnp.tile(k_norm_w[0], 2)[None]
    qT, k, vT, uc, ul, g = _inproj_call(ctx, x, mod3, norm1_w, w_in[0].astype(BF16), qw, kw, b_gate,
                                        cos_t, sin_t, seg)

    o = _attn_call(qT, k, vT, lam_q1, lam_k1, lam_q2, lam_k2, subln_w, tq=256)

    m, bzx, ccx, a16x = _s5tab_call(s5_a_re[0], s5_a_im[0], s5_log_dt[0], s5_b_re[0], s5_b_im[0],
                                    s5_c_re[0], s5_c_im[0], s5_d[0])
    sin = _s5scan_call(uc, ul, bzx, a16x)
    y = _s5out_call(ul, m, ccx, sin, nb=512)

    x1 = _merge_call(y, o, g, x, mod3, glu_w[0].astype(BF16), glu_b, w_branch_s5[0].astype(BF16),
                     w_branch_attn[0].astype(BF16), w_out[0].astype(BF16), tm=512)

    nf = N_FF_CHUNKS
    wup = w_up[0].astype(BF16).reshape(D, 2, nf, FF_CHUNK).transpose(1, 2, 0, 3)
    cw = conv_w[0].reshape(3, 2, nf, FF_CHUNK).transpose(1, 2, 0, 3)
    cb = conv_b[0].reshape(2, nf, 1, FF_CHUNK)
    wdn = w_down[0].astype(BF16).reshape(nf, FF_CHUNK, D)
    return _ffn_call(x1, mod3, norm2_w, wup, cw, cb, wdn, tm=512)
```

```python
import functools
import math

import jax
import jax.numpy as jnp
from jax import lax
from jax.experimental import pallas as pl
from jax.experimental.pallas import tpu as pltpu

D_MODEL = 1024
GRID_W = 64
N_HEADS = 8
HEAD_DIM = 64
V_DIM = 2 * HEAD_DIM
QK_W = N_HEADS * 2 * HEAD_DIM
ATTN_W = N_HEADS * V_DIM
S5_W = 512
S5_GROUP = 16
S5_GROUPS = S5_W // S5_GROUP
S5_STATE = 64
D_FF = 2816
ROPE_THETA = 10000.0
EPS = 1e-6
MIN_NEG_RE = -1e-4
K_OFF = QK_W
V_OFF = 2 * QK_W
U_OFF = 2 * QK_W + ATTN_W
G_OFF = U_OFF + S5_W
N_IN = G_OFF + 2 * D_MODEL
LAM_INIT = 0.8 - 0.6 * math.exp(-0.3 * 0)
Q_SCALE = HEAD_DIM ** -0.5 * math.log2(math.e)

LANES = 128
TOK_TILE = 256
CHUNK = 16
SLAB = LANES
N_SLABS = S5_W // SLAB
GROUPS_PER_SLAB = SLAB // S5_GROUP
SLAB_IN = CHUNK * SLAB
SLAB_STATE = GROUPS_PER_SLAB * 2 * S5_STATE
FF_CHUNK = 256
N_FF_CHUNKS = D_FF // FF_CHUNK
FF_DOWN_GROUP = 4
HALO = 8
VMEM_LIMIT = 56 * 1024 * 1024

F32 = jnp.float32
BF16 = jnp.bfloat16


def _sigmoid(x):
    return 1.0 / (1.0 + jnp.exp(-x))


def _dot(a, b):
    return jnp.dot(a, b, preferred_element_type=F32)


def _params(n_parallel, n_arbitrary=0):
    return pltpu.CompilerParams(
        dimension_semantics=("parallel",) * n_parallel + ("arbitrary",) * n_arbitrary,
        vmem_limit_bytes=VMEM_LIMIT)


def _mod_body(c_ref, w_ref, b_ref, o_ref):
    c = c_ref[...]
    s = c * _sigmoid(c)
    o_ref[...] = jnp.dot(s, w_ref[...], preferred_element_type=F32,
                         precision=lax.Precision.HIGHEST) + b_ref[...]


def _mod_call(cc, ada_w, ada_b):
    n = ada_w.shape[1]
    return pl.pallas_call(
        _mod_body,
        grid=(n // D_MODEL,),
        in_specs=[pl.BlockSpec((8, D_MODEL), lambda j: (0, 0)),
                  pl.BlockSpec((D_MODEL, D_MODEL), lambda j: (0, j)),
                  pl.BlockSpec((1, D_MODEL), lambda j: (0, j))],
        out_specs=pl.BlockSpec((8, D_MODEL), lambda j: (0, j)),
        out_shape=jax.ShapeDtypeStruct((8, n), F32),
        compiler_params=_params(1),
        name="mod",
    )(cc, ada_w, ada_b)


def _inproj_body(ctx_ref, x_ref, mod_ref, n1w_ref, w_ref, qw_ref, kw_ref, bg_ref, cos_ref, sin_ref,
                 seg_ref, qT_ref, k_ref, vT_ref, u_ref, g_ref, u_scr):
    i = pl.program_id(1)
    is_ctx = i == 0
    xt = jnp.where(is_ctx, ctx_ref[...], x_ref[...])
    ms = jnp.mean(xt * xt, axis=-1, keepdims=True)
    mod = mod_ref[...]
    h = (xt * lax.rsqrt(ms + EPS)) * n1w_ref[...] * (1.0 + mod[1:2]) + mod[0:1]
    hb = h.astype(BF16)

    cos = cos_ref[...]
    sin = sin_ref[...]
    seg = seg_ref[...]
    lane = lax.broadcasted_iota(jnp.int32, cos.shape, 1)
    first_half = (lane % 32) < 16

    def norm_rope(zh, w):
        msq = _dot((zh * zh).astype(BF16), seg)
        y = zh * lax.rsqrt(msq + EPS) * w
        partner = jnp.where(first_half, pltpu.roll(y, LANES - 16, 1), pltpu.roll(y, 16, 1))
        return y * cos + partner * sin

    zk = _dot(hb, w_ref[:, K_OFF:V_OFF])
    kw = kw_ref[...]
    for hh in range(N_HEADS):
        k_ref[hh] = norm_rope(zk[:, hh * LANES:(hh + 1) * LANES], kw).astype(BF16)

    zv = _dot(hb, w_ref[:, V_OFF:U_OFF])
    for hh in range(N_HEADS):
        vT_ref[hh] = zv[:, hh * LANES:(hh + 1) * LANES].T.astype(BF16)

    zu = _dot(hb, w_ref[:, U_OFF:G_OFF])
    for s in range(N_SLABS):
        u_scr[s] = zu[:, s * SLAB:(s + 1) * SLAB]

    for s in range(N_SLABS):
        for t in range(CHUNK):
            piece = u_scr[s, pl.ds(t, TOK_TILE // CHUNK, stride=CHUNK), :]
            u_ref[s, :, t * SLAB:(t + 1) * SLAB] = piece.astype(BF16)

    zq = _dot(hb, w_ref[:, 0:K_OFF])
    qw = qw_ref[...]
    row = lax.broadcasted_iota(jnp.int32, (LANES, TOK_TILE), 0)
    for hh in range(N_HEADS):
        q = norm_rope(zq[:, hh * LANES:(hh + 1) * LANES], qw) * Q_SCALE
        qT = q.T
        qT_ref[hh, 0] = jnp.where(row < HEAD_DIM, qT, 0.0).astype(BF16)
        qT_ref[hh, 1] = jnp.where(row >= HEAD_DIM, qT, 0.0).astype(BF16)
    zg = _dot(hb, w_ref[:, G_OFF:N_IN]) + bg_ref[...]
    g_ref[...] = _sigmoid(zg)


def _inproj_call(ctx, x, mod3, n1w, w_in_b, qw, kw, bg, cos_t, sin_t, seg):
    B, L, D = x.shape
    n_ctx = ctx.shape[1]
    assert n_ctx == TOK_TILE and L % TOK_TILE == 0
    nt = L // TOK_TILE
    n_all = n_ctx + L
    T = TOK_TILE
    lat = lambda i: jnp.maximum(i - 1, 0)
    const2 = lambda b, i: (0, 0)
    return pl.pallas_call(
        _inproj_body,
        grid=(B, nt + 1),
        in_specs=[
            pl.BlockSpec((None, T, D), lambda b, i: (b, 0, 0)),
            pl.BlockSpec((None, T, D), lambda b, i: (b, lat(i), 0)),
            pl.BlockSpec((None, 6, D), lambda b, i: (jnp.where(i == 0, B, b), 0, 0)),
            pl.BlockSpec((1, D), const2),
            pl.BlockSpec((D, N_IN), const2),
            pl.BlockSpec((1, LANES), const2),
            pl.BlockSpec((1, LANES), const2),
            pl.BlockSpec((1, 2 * D), const2),
            pl.BlockSpec((T, LANES), lambda b, i: (i, 0)),
            pl.BlockSpec((T, LANES), lambda b, i: (i, 0)),
            pl.BlockSpec((LANES, LANES), const2),
        ],
        out_specs=[
            pl.BlockSpec((None, N_HEADS, 2, LANES, T), lambda b, i: (b, 0, 0, 0, lat(i))),
            pl.BlockSpec((None, N_HEADS, T, LANES), lambda b, i: (b, 0, i, 0)),
            pl.BlockSpec((None, N_HEADS, LANES, T), lambda b, i: (b, 0, 0, i)),
            pl.BlockSpec((N_SLABS, None, T // CHUNK, SLAB_IN), lambda b, i: (0, b, i, 0)),
            pl.BlockSpec((None, T, 2 * D), lambda b, i: (b, lat(i), 0)),
        ],
        out_shape=[
            jax.ShapeDtypeStruct((B, N_HEADS, 2, LANES, L), BF16),
            jax.ShapeDtypeStruct((B, N_HEADS, n_all, LANES), BF16),
            jax.ShapeDtypeStruct((B, N_HEADS, LANES, n_all), BF16),
            jax.ShapeDtypeStruct((N_SLABS, B, n_all // CHUNK, SLAB_IN), BF16),
            jax.ShapeDtypeStruct((B, L, 2 * D), F32),
        ],
        scratch_shapes=[pltpu.VMEM((N_SLABS, T, SLAB), F32)],
        compiler_params=_params(1, 1),
        name="in_proj",
    )(ctx, x, mod3, n1w, w_in_b, qw, kw, bg, cos_t, sin_t, seg)


def _attn_body(qT_ref, k_ref, vT_ref, lq1_ref, lk1_ref, lq2_ref, lk2_ref, sw_ref, o_ref,
               s0_ref, s1_ref, m0_ref, m1_ref, *, tq):
    g = pl.program_id(0)
    tk = TOK_TILE
    n_all = k_ref.shape[0]

    @pl.when(g == 0)
    def _():
        s1_ref[...] = jnp.zeros(s1_ref.shape, F32)
        m1_ref[...] = jnp.zeros(m1_ref.shape, F32)

    def step(s_w, m_w, s_r, m_r):
        qT = jnp.concatenate([qT_ref[0], qT_ref[1]], axis=1)
        m_prev = m_r[...]
        m8 = l8 = acc = None
        for lo in range(0, n_all, tk):
            s_new = _dot(k_ref[lo:lo + tk, :], qT)
            s_w[lo:lo + tk, :] = s_new
            part = jnp.max(s_new.reshape(tk // 8, 8, 2 * tq), axis=0)
            m8 = part if m8 is None else jnp.maximum(m8, part)

            p = jnp.exp2(s_r[lo:lo + tk, :] - m_prev)
            part = jnp.sum(p.reshape(tk // 8, 8, 2 * tq), axis=0)
            l8 = part if l8 is None else l8 + part
            pv = _dot(vT_ref[:, lo:lo + tk], p.astype(BF16))
            acc = pv if acc is None else acc + pv
        m_w[...] = jnp.max(m8, axis=0, keepdims=True)

        oT = acc / jnp.sum(l8, axis=0, keepdims=True)
        lam = (jnp.exp(jnp.sum(lq1_ref[...] * lk1_ref[...], axis=-1, keepdims=True))
               - jnp.exp(jnp.sum(lq2_ref[...] * lk2_ref[...], axis=-1, keepdims=True)) + LAM_INIT)
        od = (oT[:, :tq] - lam * oT[:, tq:]).T
        ms = jnp.mean(od * od, axis=-1, keepdims=True)
        o_ref[...] = (od * lax.rsqrt(ms + EPS)) * sw_ref[...] * (1.0 - LAM_INIT)

    @pl.when(g % 2 == 0)
    def _():
        step(s0_ref, m0_ref, s1_ref, m1_ref)

    @pl.when(g % 2 == 1)
    def _():
        step(s1_ref, m1_ref, s0_ref, m0_ref)


def _attn_call(qT, k, vT, lq1, lk1, lq2, lk2, sw, tq):
    B, H, _, _, L = qT.shape
    n_all = k.shape[2]
    nq = L // tq
    n_tiles = B * H * nq

    def tile(t):
        return t // (H * nq), (t // nq) % H, t % nq

    def scored(g):
        return tile(jnp.minimum(g, n_tiles - 1))

    def finished(g):
        return tile(jnp.maximum(g - 1, 0))

    def q_map(g):
        b, h, i = scored(g)
        return b, h, 0, 0, i

    def k_map(g):
        b, h, _ = scored(g)
        return b, h, 0, 0

    def v_map(g):
        b, h, _ = finished(g)
        return b, h, 0, 0

    def o_map(g):
        b, h, i = finished(g)
        return b, i, h

    vec = pl.BlockSpec((1, HEAD_DIM), lambda g: (0, 0))
    return pl.pallas_call(
        functools.partial(_attn_body, tq=tq),
        grid=(n_tiles + 1,),
        in_specs=[
            pl.BlockSpec((None, None, 2, LANES, tq), q_map),
            pl.BlockSpec((None, None, n_all, LANES), k_map),
            pl.BlockSpec((None, None, LANES, n_all), v_map),
            vec, vec, vec, vec,
            pl.BlockSpec((1, V_DIM), lambda g: (0, 0)),
        ],
        out_specs=pl.BlockSpec((None, tq, V_DIM), o_map),
        out_shape=jax.ShapeDtypeStruct((B, L, ATTN_W), F32),
        scratch_shapes=[pltpu.VMEM((n_all, 2 * tq), F32), pltpu.VMEM((n_all, 2 * tq), F32),
                        pltpu.VMEM((1, 2 * tq), F32), pltpu.VMEM((1, 2 * tq), F32)],
        compiler_params=_params(0, 1),
        name="attn",
    )(qT, k, vT, lq1, lk1, lq2, lk2, sw)


def _cmul(xr, xi, yr, yi):
    return xr * yr - xi * yi, xr * yi + xi * yr


def _a_bar(are, aim, ldt):
    lr = jnp.minimum(are, MIN_NEG_RE)
    dt = jnp.exp(ldt)
    mag = jnp.exp(lr * dt)
    return lr, aim, mag * jnp.cos(aim * dt), mag * jnp.sin(aim * dt)


def _powers(ar, ai, n):
    pows = [(jnp.ones_like(ar), jnp.zeros_like(ar))]
    for _ in range(n):
        pows.append(_cmul(pows[-1][0], pows[-1][1], ar, ai))
    return pows


def _s5tab_body(aar_ref, aai_ref, adt_ref, abr_ref, abi_ref, bar_ref, bai_ref, bdt_ref, bcr_ref, bci_ref,
                car_ref, cai_ref, cdt_ref, dsk_ref, m_ref, bz_ref, cc_ref, a16_ref, lag_ref):
    d = pl.program_id(1)
    fwd = d == 0
    half = SLAB_STATE // 2
    sel = lambda a, b: (jnp.where(fwd, a[0], b[0]), jnp.where(fwd, a[1], b[1]))

    lr, li, ar, ai = _a_bar(aar_ref[...], aai_ref[...], adt_ref[...])
    den = lr * lr + li * li
    nr = ar - 1.0
    cfr = (nr * lr + ai * li) / den
    cfi = (ai * lr - nr * li) / den
    bbr, bbi = _cmul(cfr, cfi, abr_ref[...], abi_ref[...])
    pa = _powers(ar, ai, CHUNK - 1)
    row_a = lax.broadcasted_iota(jnp.int32, (SLAB, half), 0) // S5_GROUP
    col_a = lax.broadcasted_iota(jnp.int32, (SLAB, half), 1) // S5_STATE
    diag_a = row_a == col_a

    def expand_a(v):
        return jnp.where(diag_a, jnp.concatenate([v] * (half // LANES), axis=1), 0.0)

    for t in range(CHUNK):
        pw = sel(pa[CHUNK - 1 - t], pa[t])
        zr, zi = _cmul(pw[0], pw[1], bbr, bbi)
        bz_ref[t * SLAB:(t + 1) * SLAB, :] = jnp.concatenate([expand_a(zr), expand_a(zi)], axis=1).astype(BF16)

    _, _, br_, bi_ = _a_bar(bar_ref[...], bai_ref[...], bdt_ref[...])
    pb = _powers(br_, bi_, CHUNK)
    cr = bcr_ref[...]
    ci = bci_ref[...]
    row_b = lax.broadcasted_iota(jnp.int32, (half, SLAB), 0) // S5_STATE
    col_b = lax.broadcasted_iota(jnp.int32, (half, SLAB), 1) // S5_GROUP
    diag_b = row_b == col_b

    def expand_b(v):
        return jnp.where(diag_b, jnp.concatenate([v] * GROUPS_PER_SLAB, axis=0), 0.0)

    for t in range(CHUNK):
        pw = sel(pb[t + 1], pb[CHUNK - t])
        ctr, cti = _cmul(cr, ci, pw[0], pw[1])
        cc_ref[:, t * SLAB:(t + 1) * SLAB] = jnp.concatenate([expand_b(ctr), expand_b(-cti)], axis=0).astype(BF16)

    lane = lax.broadcasted_iota(jnp.int32, (SLAB, LANES), 1)
    lhs = jnp.where(lane < S5_STATE, bbr, bbi)
    r_k = lax.broadcasted_iota(jnp.int32, (SLAB, SLAB), 0)
    c_k = lax.broadcasted_iota(jnp.int32, (SLAB, SLAB), 1)
    diag_k = (r_k // S5_GROUP) == (c_k // S5_GROUP)
    sign = 1 - 2 * d
    for tau in range(CHUNK):
        ctr, cti = _cmul(cr, ci, pb[tau][0], pb[tau][1])
        rhs = jnp.concatenate([ctr, -cti], axis=0)
        ktau = jnp.where(diag_k, jnp.dot(lhs, rhs, preferred_element_type=F32,
                                         precision=lax.Precision.HIGHEST), 0.0)
        if tau == 0:
            @pl.when(fwd)
            def _():
                lag_ref[CHUNK - 1] = ktau + jnp.where(r_k == c_k, dsk_ref[...], 0.0)

            @pl.when(jnp.logical_not(fwd))
            def _():
                lag_ref[CHUNK - 1] = lag_ref[CHUNK - 1] + ktau
        else:
            lag_ref[CHUNK - 1 + sign * tau] = ktau

    @pl.when(jnp.logical_not(fwd))
    def _():
        for tp in range(CHUNK):
            for t in range(CHUNK):
                m_ref[tp * SLAB:(tp + 1) * SLAB, t * SLAB:(t + 1) * SLAB] = lag_ref[t - tp + CHUNK - 1].astype(BF16)

    _, _, xr, xi = _a_bar(car_ref[...], cai_ref[...], cdt_ref[...])
    for _ in range(4):
        xr, xi = _cmul(xr, xi, xr, xi)
    a16_ref[...] = jnp.concatenate([xr, xi], axis=1)


def _s5tab_call(s5_a_re, s5_a_im, s5_log_dt, s5_b_re, s5_b_im, s5_c_re, s5_c_im, s5_d):
    S, GS, C, P = N_SLABS, GROUPS_PER_SLAB, S5_GROUP, S5_STATE
    half = SLAB_STATE // 2

    def lay_a(a):
        return jnp.broadcast_to(a.reshape(2, S, GS, 1, 1, P), (2, S, GS, C, 2, P)).reshape(2, S, SLAB, LANES)

    def lay_b(a):
        a = a.reshape(2, S, GS, P).transpose(0, 1, 3, 2)
        return jnp.broadcast_to(a[..., None], (2, S, P, GS, C)).reshape(2, S, P, SLAB)

    ldt = jnp.broadcast_to(s5_log_dt[..., None], (2, S5_GROUPS, P))
    b_a = lambda b: jnp.broadcast_to(b.transpose(0, 1, 3, 2).reshape(2, S, GS, C, 1, P),
                                     (2, S, GS, C, 2, P)).reshape(2, S, SLAB, LANES)
    c_b = lambda c: c.reshape(2, S, GS, C, P).transpose(0, 1, 4, 2, 3).reshape(2, S, P, SLAB)
    lay_c = lambda a: a.reshape(2, S, 1, half)
    blk = lambda r, c: pl.BlockSpec((None, None, r, c), lambda s, d: (d, s, 0, 0))
    return pl.pallas_call(
        _s5tab_body,
        grid=(S, 2),
        in_specs=[blk(SLAB, LANES)] * 5 + [blk(P, SLAB)] * 5 + [blk(1, half)] * 3
                 + [pl.BlockSpec((None, 1, SLAB), lambda s, d: (s, 0, 0))],
        out_specs=[pl.BlockSpec((None, SLAB_IN, SLAB_IN), lambda s, d: (s, 0, 0)),
                   blk(SLAB_IN, SLAB_STATE), blk(SLAB_STATE, SLAB_IN), blk(1, SLAB_STATE)],
        out_shape=[jax.ShapeDtypeStruct((S, SLAB_IN, SLAB_IN), BF16),
                   jax.ShapeDtypeStruct((2, S, SLAB_IN, SLAB_STATE), BF16),
                   jax.ShapeDtypeStruct((2, S, SLAB_STATE, SLAB_IN), BF16),
                   jax.ShapeDtypeStruct((2, S, 1, SLAB_STATE), F32)],
        scratch_shapes=[pltpu.VMEM((2 * CHUNK - 1, SLAB, SLAB), F32)],
        compiler_params=_params(0, 2),
        name="s5_tab",
    )(lay_a(s5_a_re), lay_a(s5_a_im), lay_a(ldt), b_a(s5_b_re), b_a(s5_b_im),
      lay_b(s5_a_re), lay_b(s5_a_im), lay_b(ldt), c_b(s5_c_re), c_b(s5_c_im),
      lay_c(s5_a_re), lay_c(s5_a_im), lay_c(ldt), s5_d.reshape(S, 1, SLAB))


def _s5scan_body(u_ref, bz_ref, a_ref, sin_ref, zc_ref, zl_ref, st_ref, *, n_cc, n_lc):
    d = pl.program_id(1)
    B = u_ref.shape[0]
    w = bz_ref[...]
    zc_ref[...] = _dot(u_ref[:, :n_cc, :].reshape(B * n_cc, SLAB_IN), w).reshape(B, n_cc, SLAB_STATE)
    zl_ref[...] = _dot(u_ref[:, n_cc:, :].reshape(B * n_lc, SLAB_IN), w).reshape(B, n_lc, SLAB_STATE)
    a = a_ref[...]
    half = SLAB_STATE // 2
    a_same = jnp.concatenate([a[:, :half], a[:, :half]], axis=1)
    a_swap = jnp.concatenate([-a[:, half:], a[:, half:]], axis=1)

    def advance(s, z):
        return a_same * s + a_swap * pltpu.roll(s, half, 1) + z

    def ctx_step(i, s):
        kk = jnp.where(d == 0, i, n_cc - 1 - i)
        return tuple(advance(s[b], zc_ref[b, pl.ds(kk, 1), :]) for b in range(B))

    def lat_step(i, s):
        kk = jnp.where(d == 0, i, n_lc - 1 - i)
        for b in range(B):
            st_ref[b, pl.ds(kk, 1), :] = s[b]
        return tuple(advance(s[b], zl_ref[b, pl.ds(kk, 1), :]) for b in range(B))

    s = lax.fori_loop(0, n_cc, ctx_step, tuple(jnp.zeros((1, SLAB_STATE), F32) for _ in range(B)))
    lax.fori_loop(0, n_lc, lat_step, s)
    sin_ref[...] = st_ref[...].astype(BF16)


def _s5scan_call(u16, bzx, a16x, n_cc):
    S, B, n_c, _ = u16.shape
    n_lc = n_c - n_cc
    return pl.pallas_call(
        functools.partial(_s5scan_body, n_cc=n_cc, n_lc=n_lc),
        grid=(S, 2),
        in_specs=[pl.BlockSpec((None, B, n_c, SLAB_IN), lambda s, d: (s, 0, 0, 0)),
                  pl.BlockSpec((None, None, SLAB_IN, SLAB_STATE), lambda s, d: (d, s, 0, 0)),
                  pl.BlockSpec((None, None, 1, SLAB_STATE), lambda s, d: (d, s, 0, 0))],
        out_specs=pl.BlockSpec((None, None, B, n_lc, SLAB_STATE), lambda s, d: (d, s, 0, 0, 0)),
        out_shape=jax.ShapeDtypeStruct((2, S, B, n_lc, SLAB_STATE), BF16),
        scratch_shapes=[pltpu.VMEM((B, n_cc, SLAB_STATE), F32),
                        pltpu.VMEM((B, n_lc, SLAB_STATE), F32),
                        pltpu.VMEM((B, n_lc, SLAB_STATE), F32)],
        compiler_params=_params(2),
        name="s5_scan",
    )(u16, bzx, a16x)


def _s5out_body(u_ref, m_ref, cc_ref, sin_ref, y_ref):
    B, n_lc, _ = y_ref.shape
    rows = B * n_lc
    y = _dot(u_ref[:, u_ref.shape[1] - n_lc:, :].reshape(rows, SLAB_IN), m_ref[...])
    y = y + _dot(sin_ref[0].reshape(rows, SLAB_STATE), cc_ref[0])
    y = y + _dot(sin_ref[1].reshape(rows, SLAB_STATE), cc_ref[1])
    y_ref[...] = y.reshape(B, n_lc, y.shape[-1])


def _s5out_call(u16, m, ccx, sin, nb):
    S, B, n_c, _ = u16.shape
    n_lc = sin.shape[3]
    return pl.pallas_call(
        _s5out_body,
        grid=(S, SLAB_IN // nb),
        in_specs=[pl.BlockSpec((None, B, n_c, SLAB_IN), lambda s, j: (s, 0, 0, 0)),
                  pl.BlockSpec((None, SLAB_IN, nb), lambda s, j: (s, 0, j)),
                  pl.BlockSpec((2, None, SLAB_STATE, nb), lambda s, j: (0, s, 0, j)),
                  pl.BlockSpec((2, None, B, n_lc, SLAB_STATE), lambda s, j: (0, s, 0, 0, 0))],
        out_specs=pl.BlockSpec((None, B, n_lc, nb), lambda s, j: (s, 0, 0, j)),
        out_shape=jax.ShapeDtypeStruct((S, B, n_lc, SLAB_IN), F32),
        compiler_params=_params(2),
        name="s5_out",
    )(u16, m, ccx, sin)


def _merge_body(y_ref, o_ref, g_ref, x_ref, mod_ref, gw_ref, gb_ref, wbs_ref, wba_ref, wo_ref, out_ref, y_scr):
    n_rows = y_ref.shape[1]
    for s in range(N_SLABS):
        for t in range(CHUNK):
            y_scr[s, pl.ds(t, n_rows, stride=CHUNK), :] = y_ref[s, :, t * SLAB:(t + 1) * SLAB]
    y = jnp.concatenate([y_scr[s] for s in range(N_SLABS)], axis=1)
    hh = 0.5 * y * (1.0 + jnp.tanh(math.sqrt(2.0 / math.pi) * (y + 0.044715 * (y * y * y))))
    gl = _dot(hh.astype(BF16), gw_ref[...]) + gb_ref[...]
    ys = _dot((hh * _sigmoid(gl)).astype(BF16), wbs_ref[...])
    ya = _dot(o_ref[...].astype(BF16), wba_ref[...])
    g = g_ref[...]
    mix = g[:, :D_MODEL] * ys + g[:, D_MODEL:] * ya
    out_ref[...] = x_ref[...] + mod_ref[2:3] * _dot(mix.astype(BF16), wo_ref[...])


def _merge_call(y, o, g, x, mod3, gw, gb, wbs, wba, wo, tm):
    B, L, D = x.shape
    c2 = lambda b, i: (0, 0)
    return pl.pallas_call(
        _merge_body,
        grid=(B, L // tm),
        in_specs=[pl.BlockSpec((N_SLABS, None, tm // CHUNK, SLAB_IN), lambda b, i: (0, b, i, 0)),
                  pl.BlockSpec((None, tm, ATTN_W), lambda b, i: (b, i, 0)),
                  pl.BlockSpec((None, tm, 2 * D), lambda b, i: (b, i, 0)),
                  pl.BlockSpec((None, tm, D), lambda b, i: (b, i, 0)),
                  pl.BlockSpec((None, 6, D), lambda b, i: (b, 0, 0)),
                  pl.BlockSpec((S5_W, S5_W), c2), pl.BlockSpec((1, S5_W), c2),
                  pl.BlockSpec((S5_W, D), c2), pl.BlockSpec((ATTN_W, D), c2), pl.BlockSpec((D, D), c2)],
        out_specs=pl.BlockSpec((None, tm, D), lambda b, i: (b, i, 0)),
        out_shape=jax.ShapeDtypeStruct((B, L, D), F32),
        scratch_shapes=[pltpu.VMEM((N_SLABS, tm, SLAB), F32)],
        compiler_params=_params(2),
        name="merge",
    )(y, o, g, x, mod3, gw, gb, wbs, wba, wo)


def _ffn_body(xp_ref, x_ref, xn_ref, mod_ref, n2w_ref, wup_ref, cw_ref, cb_ref, wdn_ref, out_ref, *, tm):
    i = pl.program_id(1)
    last = pl.num_programs(1) - 1
    mod = mod_ref[...]
    n2w = n2w_ref[...]

    def modulated(v):
        ms = jnp.mean(v * v, axis=-1, keepdims=True)
        return (v * lax.rsqrt(ms + EPS)) * n2w * (1.0 + mod[4:5]) + mod[3:4]

    x = x_ref[...]
    hp = modulated(xp_ref[...]) * (i > 0).astype(F32)
    hn = modulated(xn_ref[...]) * (i < last).astype(F32)
    hcat = jnp.concatenate([hp, modulated(x), hn], axis=0).astype(BF16)
    rows = tm + 2 * HALO

    def conv(u, cw, cb):
        prev = pltpu.roll(u, 1, 0)[HALO:HALO + tm]
        nxt = pltpu.roll(u, rows - 1, 0)[HALO:HALO + tm]
        return cb + prev * cw[0:1] + u[HALO:HALO + tm] * cw[1:2] + nxt * cw[2:3]

    def col(ref, part, f):
        lo = part * D_FF + f * FF_CHUNK
        return ref[:, lo:lo + FF_CHUNK]

    ups, acts, acc = {}, {}, None
    for f in range(N_FF_CHUNKS + 2):
        if f < N_FF_CHUNKS:
            ups[f] = (_dot(hcat, col(wup_ref, 0, f)), _dot(hcat, col(wup_ref, 1, f)))
        e = f - 1
        if 0 <= e < N_FF_CHUNKS:
            ua, ug = ups.pop(e)
            ya = conv(ua, col(cw_ref, 0, e), col(cb_ref, 0, e))
            half_g = 0.5 * conv(ug, col(cw_ref, 1, e), col(cb_ref, 1, e))
            acts[e] = ((half_g + half_g * jnp.tanh(half_g)) * ya).astype(BF16)
        e = f - 2
        if 0 <= e < N_FF_CHUNKS and ((e + 1) % FF_DOWN_GROUP == 0 or e == N_FF_CHUNKS - 1):
            lo = e // FF_DOWN_GROUP * FF_DOWN_GROUP
            group = jnp.concatenate([acts.pop(j) for j in range(lo, e + 1)], axis=1)
            down = _dot(group, wdn_ref[lo * FF_CHUNK:(e + 1) * FF_CHUNK, :])
            acc = down if acc is None else acc + down
    out_ref[...] = x + mod[5:6] * acc


def _ffn_call(x1, mod3, n2w, wup, cw, cb, wdn, tm):
    B, L, D = x1.shape
    nh = tm // HALO
    n_halo_blocks = L // HALO
    c2 = lambda b, i: (0, 0)
    return pl.pallas_call(
        functools.partial(_ffn_body, tm=tm),
        grid=(B, L // tm),
        in_specs=[pl.BlockSpec((None, HALO, D), lambda b, i: (b, jnp.maximum(i * nh - 1, 0), 0)),
                  pl.BlockSpec((None, tm, D), lambda b, i: (b, i, 0)),
                  pl.BlockSpec((None, HALO, D), lambda b, i: (b, jnp.minimum((i + 1) * nh, n_halo_blocks - 1), 0)),
                  pl.BlockSpec((None, 6, D), lambda b, i: (b, 0, 0)),
                  pl.BlockSpec((1, D), c2),
                  pl.BlockSpec((D, 2 * D_FF), c2),
                  pl.BlockSpec((3, 2 * D_FF), c2),
                  pl.BlockSpec((1, 2 * D_FF), c2),
                  pl.BlockSpec((D_FF, D), c2)],
        out_specs=pl.BlockSpec((None, tm, D), lambda b, i: (b, i, 0)),
        out_shape=jax.ShapeDtypeStruct((B, L, D), F32),
        compiler_params=_params(2),
        name="ffn",
    )(x1, x1, x1, mod3, n2w, wup, cw, cb, wdn)


def _rope_tables(n_ctx, L):
    t = jnp.arange(n_ctx + L) - n_ctx
    row = jnp.where(t < 0, 0, t // GRID_W).astype(F32)
    col = jnp.where(t < 0, 0, t % GRID_W).astype(F32)
    half = HEAD_DIM // 2
    inv_freq = ROPE_THETA ** (-jnp.arange(0, half, 2, dtype=F32) / half)
    lane = jnp.arange(LANES)
    freq = inv_freq[lane % (half // 2)]
    ang = jnp.where((lane % HEAD_DIM < half)[None, :], row[:, None], col[:, None]) * freq[None, :]
    sign = jnp.where(lane % half < half // 2, -1.0, 1.0)
    return jnp.cos(ang), jnp.sin(ang) * sign


def kernel(x, c, ctx, c_ctx, ada_w, ada_b, norm1_w, w_in, b_gate, q_norm_w, k_norm_w, lam_q1, lam_k1, lam_q2, lam_k2, subln_w, s5_a_re, s5_a_im, s5_log_dt, s5_b_re, s5_b_im, s5_c_re, s5_c_im, s5_d, glu_w, glu_b, w_branch_s5, w_branch_attn, w_out, norm2_w, w_up, conv_w, conv_b, w_down):
    B, L, D = x.shape
    n_ctx = ctx.shape[1]
    assert ada_w.shape[0] == 1 and D == D_MODEL and B + 1 <= 8

    cc = jnp.zeros((8, D), F32).at[:B].set(c).at[B].set(c_ctx)
    mod3 = _mod_call(cc, ada_w[0], ada_b).reshape(8, 6, D)

    cos_t, sin_t = _rope_tables(n_ctx, L)
    lane = jnp.arange(LANES)
    seg = jnp.where((lane[:, None] // HEAD_DIM) == (lane[None, :] // HEAD_DIM), 1.0 / HEAD_DIM, 0.0).astype(BF16)
    qw = jnp.tile(q_norm_w[0], 2)[None]
    kw = jnp.tile(k_norm_w[0], 2)[None]
    qT, k, vT, u16, g = _inproj_call(ctx, x, mod3, norm1_w, w_in[0].astype(BF16), qw, kw, b_gate,
                                        cos_t, sin_t, seg)

    o = _attn_call(qT, k, vT, lam_q1, lam_k1, lam_q2, lam_k2, subln_w, tq=512)

    m, bzx, ccx, a16x = _s5tab_call(s5_a_re[0], s5_a_im[0], s5_log_dt[0], s5_b_re[0], s5_b_im[0],
                                    s5_c_re[0], s5_c_im[0], s5_d[0])
    sin = _s5scan_call(u16, bzx, a16x, n_ctx // CHUNK)
    y = _s5out_call(u16, m, ccx, sin, nb=512)

    x1 = _merge_call(y, o, g, x, mod3, glu_w[0].astype(BF16), glu_b, w_branch_s5[0].astype(BF16),
                     w_branch_attn[0].astype(BF16), w_out[0].astype(BF16), tm=512)

    return _ffn_call(x1, mod3, norm2_w, w_up[0].astype(BF16), conv_w[0], conv_b, w_down[0].astype(BF16), tm=512)
```

```python
import functools
import math

import jax
import jax.numpy as jnp
from jax import lax
from jax.experimental import pallas as pl
from jax.experimental.pallas import tpu as pltpu

D_MODEL = 1024
GRID_W = 64
N_HEADS = 8
HEAD_DIM = 64
V_DIM = 2 * HEAD_DIM
V_ROWS = V_DIM + 16
QK_W = N_HEADS * 2 * HEAD_DIM
ATTN_W = N_HEADS * V_DIM
S5_W = 512
S5_GROUP = 16
S5_GROUPS = S5_W // S5_GROUP
S5_STATE = 64
D_FF = 2816
ROPE_THETA = 10000.0
EPS = 1e-6
MIN_NEG_RE = -1e-4
K_OFF = QK_W
V_OFF = 2 * QK_W
U_OFF = 2 * QK_W + ATTN_W
G_OFF = U_OFF + S5_W
N_IN = G_OFF + 2 * D_MODEL
LAM_INIT = 0.8 - 0.6 * math.exp(-0.3 * 0)
Q_SCALE = HEAD_DIM ** -0.5 * math.log2(math.e)

LANES = 128
TOK_TILE = 256
CHUNK = 16
SLAB = LANES
N_SLABS = S5_W // SLAB
GROUPS_PER_SLAB = SLAB // S5_GROUP
SLAB_IN = CHUNK * SLAB
SLAB_STATE = GROUPS_PER_SLAB * 2 * S5_STATE
FF_CHUNK = 256
N_FF_CHUNKS = D_FF // FF_CHUNK
FF_DOWN_GROUP = 4
HALO = 8
VMEM_LIMIT = 56 * 1024 * 1024

F32 = jnp.float32
BF16 = jnp.bfloat16
QK_DTYPE = jnp.float8_e4m3fn


def _sigmoid(x):
    return 1.0 / (1.0 + jnp.exp(-x))


def _dot(a, b):
    return jnp.dot(a, b, preferred_element_type=F32)


def _params(n_parallel, n_arbitrary=0):
    return pltpu.CompilerParams(
        dimension_semantics=("parallel",) * n_parallel + ("arbitrary",) * n_arbitrary,
        vmem_limit_bytes=VMEM_LIMIT)


def _mod_body(c_ref, w_ref, b_ref, o_ref):
    c = c_ref[...]
    s = c * _sigmoid(c)
    o_ref[...] = jnp.dot(s, w_ref[...], preferred_element_type=F32,
                         precision=lax.Precision.HIGHEST) + b_ref[...]


def _mod_call(cc, ada_w, ada_b):
    n = ada_w.shape[1]
    return pl.pallas_call(
        _mod_body,
        grid=(n // D_MODEL,),
        in_specs=[pl.BlockSpec((8, D_MODEL), lambda j: (0, 0)),
                  pl.BlockSpec((D_MODEL, D_MODEL), lambda j: (0, j)),
                  pl.BlockSpec((1, D_MODEL), lambda j: (0, j))],
        out_specs=pl.BlockSpec((8, D_MODEL), lambda j: (0, j)),
        out_shape=jax.ShapeDtypeStruct((8, n), F32),
        compiler_params=_params(1),
        name="mod",
    )(cc, ada_w, ada_b)


def _inproj_body(ctx_ref, x_ref, mod_ref, n1w_ref, w_ref, qw_ref, kw_ref, bg_ref, cos_ref, sin_ref,
                 seg_ref, qT_ref, k_ref, vT_ref, u_ref, g_ref, u_scr):
    i = pl.program_id(1)
    is_ctx = i == 0
    xt = jnp.where(is_ctx, ctx_ref[...], x_ref[...])
    ms = jnp.mean(xt * xt, axis=-1, keepdims=True)
    mod = mod_ref[...]
    h = (xt * lax.rsqrt(ms + EPS)) * n1w_ref[...] * (1.0 + mod[1:2]) + mod[0:1]
    hb = h.astype(BF16)

    cos = cos_ref[...]
    sin = sin_ref[...]
    seg = seg_ref[...]
    lane = lax.broadcasted_iota(jnp.int32, cos.shape, 1)
    first_half = (lane % 32) < 16

    def norm_rope(zh, w):
        msq = _dot((zh * zh).astype(BF16), seg)
        y = zh * lax.rsqrt(msq + EPS) * w
        partner = jnp.where(first_half, pltpu.roll(y, LANES - 16, 1), pltpu.roll(y, 16, 1))
        return y * cos + partner * sin

    zk = _dot(hb, w_ref[:, K_OFF:V_OFF])
    kw = kw_ref[...]
    for hh in range(N_HEADS):
        k_ref[hh] = norm_rope(zk[:, hh * LANES:(hh + 1) * LANES], kw).astype(QK_DTYPE)

    zv = _dot(hb, w_ref[:, V_OFF:U_OFF])
    for hh in range(N_HEADS):
        vT_ref[hh, :V_DIM] = zv[:, hh * LANES:(hh + 1) * LANES].T.astype(BF16)
        vT_ref[hh, V_DIM:] = jnp.ones((V_ROWS - V_DIM, TOK_TILE), BF16)

    zu = _dot(hb, w_ref[:, U_OFF:G_OFF])
    for s in range(N_SLABS):
        u_scr[s] = zu[:, s * SLAB:(s + 1) * SLAB]

    for s in range(N_SLABS):
        for t in range(CHUNK):
            piece = u_scr[s, pl.ds(t, TOK_TILE // CHUNK, stride=CHUNK), :]
            u_ref[s, :, t * SLAB:(t + 1) * SLAB] = piece.astype(BF16)

    zq = _dot(hb, w_ref[:, 0:K_OFF])
    qw = qw_ref[...]
    row = lax.broadcasted_iota(jnp.int32, (LANES, TOK_TILE), 0)
    for hh in range(N_HEADS):
        q = norm_rope(zq[:, hh * LANES:(hh + 1) * LANES], qw) * Q_SCALE
        qT = q.T
        qT_ref[hh, 0] = jnp.where(row < HEAD_DIM, qT, 0.0).astype(QK_DTYPE)
        qT_ref[hh, 1] = jnp.where(row >= HEAD_DIM, qT, 0.0).astype(QK_DTYPE)
    zg = _dot(hb, w_ref[:, G_OFF:N_IN]) + bg_ref[...]
    g_ref[...] = _sigmoid(zg).astype(BF16)


def _inproj_call(ctx, x, mod3, n1w, w_in_b, qw, kw, bg, cos_t, sin_t, seg):
    B, L, D = x.shape
    n_ctx = ctx.shape[1]
    assert n_ctx == TOK_TILE and L % TOK_TILE == 0
    nt = L // TOK_TILE
    n_all = n_ctx + L
    T = TOK_TILE
    lat = lambda i: jnp.maximum(i - 1, 0)
    const2 = lambda b, i: (0, 0)
    return pl.pallas_call(
        _inproj_body,
        grid=(B, nt + 1),
        in_specs=[
            pl.BlockSpec((None, T, D), lambda b, i: (b, 0, 0)),
            pl.BlockSpec((None, T, D), lambda b, i: (b, lat(i), 0)),
            pl.BlockSpec((None, 6, D), lambda b, i: (jnp.where(i == 0, B, b), 0, 0)),
            pl.BlockSpec((1, D), const2),
            pl.BlockSpec((D, N_IN), const2),
            pl.BlockSpec((1, LANES), const2),
            pl.BlockSpec((1, LANES), const2),
            pl.BlockSpec((1, 2 * D), const2),
            pl.BlockSpec((T, LANES), lambda b, i: (i, 0)),
            pl.BlockSpec((T, LANES), lambda b, i: (i, 0)),
            pl.BlockSpec((LANES, LANES), const2),
        ],
        out_specs=[
            pl.BlockSpec((None, N_HEADS, 2, LANES, T), lambda b, i: (b, 0, 0, 0, lat(i))),
            pl.BlockSpec((None, N_HEADS, T, LANES), lambda b, i: (b, 0, i, 0)),
            pl.BlockSpec((None, N_HEADS, V_ROWS, T), lambda b, i: (b, 0, 0, i)),
            pl.BlockSpec((N_SLABS, None, T // CHUNK, SLAB_IN), lambda b, i: (0, b, i, 0)),
            pl.BlockSpec((None, T, 2 * D), lambda b, i: (b, lat(i), 0)),
        ],
        out_shape=[
            jax.ShapeDtypeStruct((B, N_HEADS, 2, LANES, L), QK_DTYPE),
            jax.ShapeDtypeStruct((B, N_HEADS, n_all, LANES), QK_DTYPE),
            jax.ShapeDtypeStruct((B, N_HEADS, V_ROWS, n_all), BF16),
            jax.ShapeDtypeStruct((N_SLABS, B, n_all // CHUNK, SLAB_IN), BF16),
            jax.ShapeDtypeStruct((B, L, 2 * D), BF16),
        ],
        scratch_shapes=[pltpu.VMEM((N_SLABS, T, SLAB), F32)],
        compiler_params=_params(1, 1),
        name="in_proj",
    )(ctx, x, mod3, n1w, w_in_b, qw, kw, bg, cos_t, sin_t, seg)


def _attn_body(qT_ref, k_ref, vT_ref, lq1_ref, lk1_ref, lq2_ref, lk2_ref, sw_ref, o_ref,
               s0_ref, s1_ref, m0_ref, m1_ref, *, tq):
    g = pl.program_id(0)
    tk = TOK_TILE
    n_all = k_ref.shape[0]

    @pl.when(g == 0)
    def _():
        s1_ref[...] = jnp.zeros(s1_ref.shape, F32)
        m1_ref[...] = jnp.zeros(m1_ref.shape, F32)

    def step(s_w, m_w, s_r, m_r):
        qT = jnp.concatenate([qT_ref[0], qT_ref[1]], axis=1)
        m_prev = m_r[...]
        m8 = acc = None
        for lo in range(0, n_all, tk):
            s_new = _dot(k_ref[lo:lo + tk, :], qT)
            s_w[lo:lo + tk, :] = s_new
            part = jnp.max(s_new.reshape(tk // 8, 8, 2 * tq), axis=0)
            m8 = part if m8 is None else jnp.maximum(m8, part)

            p = jnp.exp2(s_r[lo:lo + tk, :] - m_prev)
            pv = _dot(vT_ref[:, lo:lo + tk], p.astype(BF16))
            acc = pv if acc is None else acc + pv
        m_w[...] = jnp.max(m8, axis=0, keepdims=True)

        oT = acc[:V_DIM] / acc[V_DIM:V_DIM + 1]
        lam = (jnp.exp(jnp.sum(lq1_ref[...] * lk1_ref[...], axis=-1, keepdims=True))
               - jnp.exp(jnp.sum(lq2_ref[...] * lk2_ref[...], axis=-1, keepdims=True)) + LAM_INIT)
        od = (oT[:, :tq] - lam * oT[:, tq:]).T
        ms = jnp.mean(od * od, axis=-1, keepdims=True)
        o_ref[...] = ((od * lax.rsqrt(ms + EPS)) * sw_ref[...] * (1.0 - LAM_INIT)).astype(BF16)

    @pl.when(g % 2 == 0)
    def _():
        step(s0_ref, m0_ref, s1_ref, m1_ref)

    @pl.when(g % 2 == 1)
    def _():
        step(s1_ref, m1_ref, s0_ref, m0_ref)


def _attn_call(qT, k, vT, lq1, lk1, lq2, lk2, sw, tq):
    B, H, _, _, L = qT.shape
    n_all = k.shape[2]
    nq = L // tq
    n_tiles = B * H * nq

    def tile(t):
        return t // (H * nq), (t // nq) % H, t % nq

    def scored(g):
        return tile(jnp.minimum(g, n_tiles - 1))

    def finished(g):
        return tile(jnp.maximum(g - 1, 0))

    def q_map(g):
        b, h, i = scored(g)
        return b, h, 0, 0, i

    def k_map(g):
        b, h, _ = scored(g)
        return b, h, 0, 0

    def v_map(g):
        b, h, _ = finished(g)
        return b, h, 0, 0

    def o_map(g):
        b, h, i = finished(g)
        return b, i, h

    vec = pl.BlockSpec((1, HEAD_DIM), lambda g: (0, 0))
    return pl.pallas_call(
        functools.partial(_attn_body, tq=tq),
        grid=(n_tiles + 1,),
        in_specs=[
            pl.BlockSpec((None, None, 2, LANES, tq), q_map),
            pl.BlockSpec((None, None, n_all, LANES), k_map),
            pl.BlockSpec((None, None, V_ROWS, n_all), v_map),
            vec, vec, vec, vec,
            pl.BlockSpec((1, V_DIM), lambda g: (0, 0)),
        ],
        out_specs=pl.BlockSpec((None, tq, V_DIM), o_map),
        out_shape=jax.ShapeDtypeStruct((B, L, ATTN_W), BF16),
        scratch_shapes=[pltpu.VMEM((n_all, 2 * tq), F32), pltpu.VMEM((n_all, 2 * tq), F32),
                        pltpu.VMEM((1, 2 * tq), F32), pltpu.VMEM((1, 2 * tq), F32)],
        compiler_params=_params(0, 1),
        name="attn",
    )(qT, k, vT, lq1, lk1, lq2, lk2, sw)


def _cmul(xr, xi, yr, yi):
    return xr * yr - xi * yi, xr * yi + xi * yr


def _a_bar(are, aim, ldt):
    lr = jnp.minimum(are, MIN_NEG_RE)
    dt = jnp.exp(ldt)
    mag = jnp.exp(lr * dt)
    return lr, aim, mag * jnp.cos(aim * dt), mag * jnp.sin(aim * dt)


def _powers(ar, ai, n):
    pows = [(jnp.ones_like(ar), jnp.zeros_like(ar))]
    for _ in range(n):
        pows.append(_cmul(pows[-1][0], pows[-1][1], ar, ai))
    return pows


def _s5tab_body(aar_ref, aai_ref, adt_ref, abr_ref, abi_ref, bar_ref, bai_ref, bdt_ref, bcr_ref, bci_ref,
                car_ref, cai_ref, cdt_ref, dsk_ref, m_ref, bz_ref, cc_ref, a16_ref, lag_ref):
    d = pl.program_id(1)
    fwd = d == 0
    half = SLAB_STATE // 2
    sel = lambda a, b: (jnp.where(fwd, a[0], b[0]), jnp.where(fwd, a[1], b[1]))

    lr, li, ar, ai = _a_bar(aar_ref[...], aai_ref[...], adt_ref[...])
    den = lr * lr + li * li
    nr = ar - 1.0
    cfr = (nr * lr + ai * li) / den
    cfi = (ai * lr - nr * li) / den
    bbr, bbi = _cmul(cfr, cfi, abr_ref[...], abi_ref[...])
    pa = _powers(ar, ai, CHUNK - 1)
    row_a = lax.broadcasted_iota(jnp.int32, (SLAB, half), 0) // S5_GROUP
    col_a = lax.broadcasted_iota(jnp.int32, (SLAB, half), 1) // S5_STATE
    diag_a = row_a == col_a

    def expand_a(v):
        return jnp.where(diag_a, jnp.concatenate([v] * (half // LANES), axis=1), 0.0)

    for t in range(CHUNK):
        pw = sel(pa[CHUNK - 1 - t], pa[t])
        zr, zi = _cmul(pw[0], pw[1], bbr, bbi)
        bz_ref[t * SLAB:(t + 1) * SLAB, :] = jnp.concatenate([expand_a(zr), expand_a(zi)], axis=1).astype(BF16)

    _, _, br_, bi_ = _a_bar(bar_ref[...], bai_ref[...], bdt_ref[...])
    pb = _powers(br_, bi_, CHUNK)
    cr = bcr_ref[...]
    ci = bci_ref[...]
    row_b = lax.broadcasted_iota(jnp.int32, (half, SLAB), 0) // S5_STATE
    col_b = lax.broadcasted_iota(jnp.int32, (half, SLAB), 1) // S5_GROUP
    diag_b = row_b == col_b

    def expand_b(v):
        return jnp.where(diag_b, jnp.concatenate([v] * GROUPS_PER_SLAB, axis=0), 0.0)

    for t in range(CHUNK):
        pw = sel(pb[t + 1], pb[CHUNK - t])
        ctr, cti = _cmul(cr, ci, pw[0], pw[1])
        cc_ref[:, t * SLAB:(t + 1) * SLAB] = jnp.concatenate([expand_b(ctr), expand_b(-cti)], axis=0).astype(BF16)

    lane = lax.broadcasted_iota(jnp.int32, (SLAB, LANES), 1)
    lhs = jnp.where(lane < S5_STATE, bbr, bbi)
    r_k = lax.broadcasted_iota(jnp.int32, (SLAB, SLAB), 0)
    c_k = lax.broadcasted_iota(jnp.int32, (SLAB, SLAB), 1)
    diag_k = (r_k // S5_GROUP) == (c_k // S5_GROUP)
    sign = 1 - 2 * d
    for tau in range(CHUNK):
        ctr, cti = _cmul(cr, ci, pb[tau][0], pb[tau][1])
        rhs = jnp.concatenate([ctr, -cti], axis=0)
        ktau = jnp.where(diag_k, jnp.dot(lhs, rhs, preferred_element_type=F32,
                                         precision=lax.Precision.HIGHEST), 0.0)
        if tau == 0:
            @pl.when(fwd)
            def _():
                lag_ref[CHUNK - 1] = ktau + jnp.where(r_k == c_k, dsk_ref[...], 0.0)

            @pl.when(jnp.logical_not(fwd))
            def _():
                lag_ref[CHUNK - 1] = lag_ref[CHUNK - 1] + ktau
        else:
            lag_ref[CHUNK - 1 + sign * tau] = ktau

    @pl.when(jnp.logical_not(fwd))
    def _():
        for tp in range(CHUNK):
            for t in range(CHUNK):
                m_ref[tp * SLAB:(tp + 1) * SLAB, t * SLAB:(t + 1) * SLAB] = lag_ref[t - tp + CHUNK - 1].astype(BF16)

    _, _, xr, xi = _a_bar(car_ref[...], cai_ref[...], cdt_ref[...])
    for _ in range(4):
        xr, xi = _cmul(xr, xi, xr, xi)
    a16_ref[...] = jnp.concatenate([xr, xi], axis=1)


def _s5tab_call(s5_a_re, s5_a_im, s5_log_dt, s5_b_re, s5_b_im, s5_c_re, s5_c_im, s5_d):
    S, GS, C, P = N_SLABS, GROUPS_PER_SLAB, S5_GROUP, S5_STATE
    half = SLAB_STATE // 2

    def lay_a(a):
        return jnp.broadcast_to(a.reshape(2, S, GS, 1, 1, P), (2, S, GS, C, 2, P)).reshape(2, S, SLAB, LANES)

    def lay_b(a):
        a = a.reshape(2, S, GS, P).transpose(0, 1, 3, 2)
        return jnp.broadcast_to(a[..., None], (2, S, P, GS, C)).reshape(2, S, P, SLAB)

    ldt = jnp.broadcast_to(s5_log_dt[..., None], (2, S5_GROUPS, P))
    b_a = lambda b: jnp.broadcast_to(b.transpose(0, 1, 3, 2).reshape(2, S, GS, C, 1, P),
                                     (2, S, GS, C, 2, P)).reshape(2, S, SLAB, LANES)
    c_b = lambda c: c.reshape(2, S, GS, C, P).transpose(0, 1, 4, 2, 3).reshape(2, S, P, SLAB)
    lay_c = lambda a: a.reshape(2, S, 1, half)
    blk = lambda r, c: pl.BlockSpec((None, None, r, c), lambda s, d: (d, s, 0, 0))
    return pl.pallas_call(
        _s5tab_body,
        grid=(S, 2),
        in_specs=[blk(SLAB, LANES)] * 5 + [blk(P, SLAB)] * 5 + [blk(1, half)] * 3
                 + [pl.BlockSpec((None, 1, SLAB), lambda s, d: (s, 0, 0))],
        out_specs=[pl.BlockSpec((None, SLAB_IN, SLAB_IN), lambda s, d: (s, 0, 0)),
                   blk(SLAB_IN, SLAB_STATE), blk(SLAB_STATE, SLAB_IN), blk(1, SLAB_STATE)],
        out_shape=[jax.ShapeDtypeStruct((S, SLAB_IN, SLAB_IN), BF16),
                   jax.ShapeDtypeStruct((2, S, SLAB_IN, SLAB_STATE), BF16),
                   jax.ShapeDtypeStruct((2, S, SLAB_STATE, SLAB_IN), BF16),
                   jax.ShapeDtypeStruct((2, S, 1, SLAB_STATE), F32)],
        scratch_shapes=[pltpu.VMEM((2 * CHUNK - 1, SLAB, SLAB), F32)],
        compiler_params=_params(0, 2),
        name="s5_tab",
    )(lay_a(s5_a_re), lay_a(s5_a_im), lay_a(ldt), b_a(s5_b_re), b_a(s5_b_im),
      lay_b(s5_a_re), lay_b(s5_a_im), lay_b(ldt), c_b(s5_c_re), c_b(s5_c_im),
      lay_c(s5_a_re), lay_c(s5_a_im), lay_c(ldt), s5_d.reshape(S, 1, SLAB))


def _s5scan_body(u_ref, bz_ref, a_ref, sin_ref, zc_ref, zl_ref, st_ref, *, n_cc, n_lc):
    d = pl.program_id(1)
    B = u_ref.shape[0]
    w = bz_ref[...]
    zc_ref[...] = _dot(u_ref[:, :n_cc, :].reshape(B * n_cc, SLAB_IN), w).reshape(B, n_cc, SLAB_STATE)
    zl_ref[...] = _dot(u_ref[:, n_cc:, :].reshape(B * n_lc, SLAB_IN), w).reshape(B, n_lc, SLAB_STATE)
    a = a_ref[...]
    half = SLAB_STATE // 2
    a_same = jnp.concatenate([a[:, :half], a[:, :half]], axis=1)
    a_swap = jnp.concatenate([-a[:, half:], a[:, half:]], axis=1)

    def advance(s, z):
        return a_same * s + a_swap * pltpu.roll(s, half, 1) + z

    def ctx_step(i, s):
        kk = jnp.where(d == 0, i, n_cc - 1 - i)
        return tuple(advance(s[b], zc_ref[b, pl.ds(kk, 1), :]) for b in range(B))

    def lat_step(i, s):
        kk = jnp.where(d == 0, i, n_lc - 1 - i)
        for b in range(B):
            st_ref[b, pl.ds(kk, 1), :] = s[b]
        return tuple(advance(s[b], zl_ref[b, pl.ds(kk, 1), :]) for b in range(B))

    s = lax.fori_loop(0, n_cc, ctx_step, tuple(jnp.zeros((1, SLAB_STATE), F32) for _ in range(B)))
    lax.fori_loop(0, n_lc, lat_step, s)
    sin_ref[...] = st_ref[...].astype(BF16)


def _s5scan_call(u16, bzx, a16x, n_cc):
    S, B, n_c, _ = u16.shape
    n_lc = n_c - n_cc
    return pl.pallas_call(
        functools.partial(_s5scan_body, n_cc=n_cc, n_lc=n_lc),
        grid=(S, 2),
        in_specs=[pl.BlockSpec((None, B, n_c, SLAB_IN), lambda s, d: (s, 0, 0, 0)),
                  pl.BlockSpec((None, None, SLAB_IN, SLAB_STATE), lambda s, d: (d, s, 0, 0)),
                  pl.BlockSpec((None, None, 1, SLAB_STATE), lambda s, d: (d, s, 0, 0))],
        out_specs=pl.BlockSpec((None, None, B, n_lc, SLAB_STATE), lambda s, d: (d, s, 0, 0, 0)),
        out_shape=jax.ShapeDtypeStruct((2, S, B, n_lc, SLAB_STATE), BF16),
        scratch_shapes=[pltpu.VMEM((B, n_cc, SLAB_STATE), F32),
                        pltpu.VMEM((B, n_lc, SLAB_STATE), F32),
                        pltpu.VMEM((B, n_lc, SLAB_STATE), F32)],
        compiler_params=_params(2),
        name="s5_scan",
    )(u16, bzx, a16x)


def _s5out_body(u_ref, m_ref, cc_ref, sin_ref, y_ref):
    B, n_lc, _ = y_ref.shape
    rows = B * n_lc
    y = _dot(u_ref[:, u_ref.shape[1] - n_lc:, :].reshape(rows, SLAB_IN), m_ref[...])
    y = y + _dot(sin_ref[0].reshape(rows, SLAB_STATE), cc_ref[0])
    y = y + _dot(sin_ref[1].reshape(rows, SLAB_STATE), cc_ref[1])
    y_ref[...] = y.reshape(B, n_lc, y.shape[-1])


def _s5out_call(u16, m, ccx, sin, nb):
    S, B, n_c, _ = u16.shape
    n_lc = sin.shape[3]
    return pl.pallas_call(
        _s5out_body,
        grid=(S, SLAB_IN // nb),
        in_specs=[pl.BlockSpec((None, B, n_c, SLAB_IN), lambda s, j: (s, 0, 0, 0)),
                  pl.BlockSpec((None, SLAB_IN, nb), lambda s, j: (s, 0, j)),
                  pl.BlockSpec((2, None, SLAB_STATE, nb), lambda s, j: (0, s, 0, j)),
                  pl.BlockSpec((2, None, B, n_lc, SLAB_STATE), lambda s, j: (0, s, 0, 0, 0))],
        out_specs=pl.BlockSpec((None, B, n_lc, nb), lambda s, j: (s, 0, 0, j)),
        out_shape=jax.ShapeDtypeStruct((S, B, n_lc, SLAB_IN), F32),
        compiler_params=_params(2),
        name="s5_out",
    )(u16, m, ccx, sin)


def _merge_body(y_ref, o_ref, g_ref, x_ref, mod_ref, gw_ref, gb_ref, wbs_ref, wba_ref, wo_ref, out_ref, y_scr):
    n_rows = y_ref.shape[1]
    for s in range(N_SLABS):
        for t in range(CHUNK):
            y_scr[s, pl.ds(t, n_rows, stride=CHUNK), :] = y_ref[s, :, t * SLAB:(t + 1) * SLAB]
    y = jnp.concatenate([y_scr[s] for s in range(N_SLABS)], axis=1)
    hh = 0.5 * y * (1.0 + jnp.tanh(math.sqrt(2.0 / math.pi) * (y + 0.044715 * (y * y * y))))
    gl = _dot(hh.astype(BF16), gw_ref[...]) + gb_ref[...]
    ys = _dot((hh * _sigmoid(gl)).astype(BF16), wbs_ref[...])
    ya = _dot(o_ref[...], wba_ref[...])
    g = g_ref[...].astype(F32)
    mix = g[:, :D_MODEL] * ys + g[:, D_MODEL:] * ya
    out_ref[...] = x_ref[...] + mod_ref[2:3] * _dot(mix.astype(BF16), wo_ref[...])


def _merge_call(y, o, g, x, mod3, gw, gb, wbs, wba, wo, tm):
    B, L, D = x.shape
    c2 = lambda b, i: (0, 0)
    return pl.pallas_call(
        _merge_body,
        grid=(B, L // tm),
        in_specs=[pl.BlockSpec((N_SLABS, None, tm // CHUNK, SLAB_IN), lambda b, i: (0, b, i, 0)),
                  pl.BlockSpec((None, tm, ATTN_W), lambda b, i: (b, i, 0)),
                  pl.BlockSpec((None, tm, 2 * D), lambda b, i: (b, i, 0)),
                  pl.BlockSpec((None, tm, D), lambda b, i: (b, i, 0)),
                  pl.BlockSpec((None, 6, D), lambda b, i: (b, 0, 0)),
                  pl.BlockSpec((S5_W, S5_W), c2), pl.BlockSpec((1, S5_W), c2),
                  pl.BlockSpec((S5_W, D), c2), pl.BlockSpec((ATTN_W, D), c2), pl.BlockSpec((D, D), c2)],
        out_specs=pl.BlockSpec((None, tm, D), lambda b, i: (b, i, 0)),
        out_shape=jax.ShapeDtypeStruct((B, L, D), F32),
        scratch_shapes=[pltpu.VMEM((N_SLABS, tm, SLAB), F32)],
        compiler_params=_params(2),
        name="merge",
    )(y, o, g, x, mod3, gw, gb, wbs, wba, wo)


def _ffn_body(xp_ref, x_ref, xn_ref, mod_ref, n2w_ref, wup_ref, cw_ref, cb_ref, wdn_ref, out_ref, *, tm):
    i = pl.program_id(1)
    last = pl.num_programs(1) - 1
    mod = mod_ref[...]
    n2w = n2w_ref[...]

    def modulated(v):
        ms = jnp.mean(v * v, axis=-1, keepdims=True)
        return (v * lax.rsqrt(ms + EPS)) * n2w * (1.0 + mod[4:5]) + mod[3:4]

    x = x_ref[...]
    hp = modulated(xp_ref[...]) * (i > 0).astype(F32)
    hn = modulated(xn_ref[...]) * (i < last).astype(F32)
    hcat = jnp.concatenate([hp, modulated(x), hn], axis=0).astype(BF16)
    rows = tm + 2 * HALO

    def conv(u, cw, cb):
        prev = pltpu.roll(u, 1, 0)[HALO:HALO + tm]
        nxt = pltpu.roll(u, rows - 1, 0)[HALO:HALO + tm]
        return cb + prev * cw[0:1] + u[HALO:HALO + tm] * cw[1:2] + nxt * cw[2:3]

    def col(ref, part, f):
        lo = part * D_FF + f * FF_CHUNK
        return ref[:, lo:lo + FF_CHUNK]

    ups, acts, acc = {}, {}, None
    for f in range(N_FF_CHUNKS + 2):
        if f < N_FF_CHUNKS:
            ups[f] = (_dot(hcat, col(wup_ref, 0, f)), _dot(hcat, col(wup_ref, 1, f)))
        e = f - 1
        if 0 <= e < N_FF_CHUNKS:
            ua, ug = ups.pop(e)
            ya = conv(ua, col(cw_ref, 0, e), col(cb_ref, 0, e))
            half_g = 0.5 * conv(ug, col(cw_ref, 1, e), col(cb_ref, 1, e))
            acts[e] = ((half_g + half_g * jnp.tanh(half_g)) * ya).astype(BF16)
        e = f - 2
        if 0 <= e < N_FF_CHUNKS and ((e + 1) % FF_DOWN_GROUP == 0 or e == N_FF_CHUNKS - 1):
            lo = e // FF_DOWN_GROUP * FF_DOWN_GROUP
            group = jnp.concatenate([acts.pop(j) for j in range(lo, e + 1)], axis=1)
            down = _dot(group, wdn_ref[lo * FF_CHUNK:(e + 1) * FF_CHUNK, :])
            acc = down if acc is None else acc + down
    out_ref[...] = x + mod[5:6] * acc


def _ffn_call(x1, mod3, n2w, wup, cw, cb, wdn, tm):
    B, L, D = x1.shape
    nh = tm // HALO
    n_halo_blocks = L // HALO
    c2 = lambda b, i: (0, 0)
    return pl.pallas_call(
        functools.partial(_ffn_body, tm=tm),
        grid=(B, L // tm),
        in_specs=[pl.BlockSpec((None, HALO, D), lambda b, i: (b, jnp.maximum(i * nh - 1, 0), 0)),
                  pl.BlockSpec((None, tm, D), lambda b, i: (b, i, 0)),
                  pl.BlockSpec((None, HALO, D), lambda b, i: (b, jnp.minimum((i + 1) * nh, n_halo_blocks - 1), 0)),
                  pl.BlockSpec((None, 6, D), lambda b, i: (b, 0, 0)),
                  pl.BlockSpec((1, D), c2),
                  pl.BlockSpec((D, 2 * D_FF), c2),
                  pl.BlockSpec((3, 2 * D_FF), c2),
                  pl.BlockSpec((1, 2 * D_FF), c2),
                  pl.BlockSpec((D_FF, D), c2)],
        out_specs=pl.BlockSpec((None, tm, D), lambda b, i: (b, i, 0)),
        out_shape=jax.ShapeDtypeStruct((B, L, D), F32),
        compiler_params=_params(2),
        name="ffn",
    )(x1, x1, x1, mod3, n2w, wup, cw, cb, wdn)


def _rope_tables(n_ctx, L):
    t = jnp.arange(n_ctx + L) - n_ctx
    row = jnp.where(t < 0, 0, t // GRID_W).astype(F32)
    col = jnp.where(t < 0, 0, t % GRID_W).astype(F32)
    half = HEAD_DIM // 2
    inv_freq = ROPE_THETA ** (-jnp.arange(0, half, 2, dtype=F32) / half)
    lane = jnp.arange(LANES)
    freq = inv_freq[lane % (half // 2)]
    ang = jnp.where((lane % HEAD_DIM < half)[None, :], row[:, None], col[:, None]) * freq[None, :]
    sign = jnp.where(lane % half < half // 2, -1.0, 1.0)
    return jnp.cos(ang), jnp.sin(ang) * sign


def kernel(x, c, ctx, c_ctx, ada_w, ada_b, norm1_w, w_in, b_gate, q_norm_w, k_norm_w, lam_q1, lam_k1, lam_q2, lam_k2, subln_w, s5_a_re, s5_a_im, s5_log_dt, s5_b_re, s5_b_im, s5_c_re, s5_c_im, s5_d, glu_w, glu_b, w_branch_s5, w_branch_attn, w_out, norm2_w, w_up, conv_w, conv_b, w_down):
    B, L, D = x.shape
    n_ctx = ctx.shape[1]
    assert ada_w.shape[0] == 1 and D == D_MODEL and B + 1 <= 8

    cc = jnp.zeros((8, D), F32).at[:B].set(c).at[B].set(c_ctx)
    mod3 = _mod_call(cc, ada_w[0], ada_b).reshape(8, 6, D)

    cos_t, sin_t = _rope_tables(n_ctx, L)
    lane = jnp.arange(LANES)
    seg = jnp.where((lane[:, None] // HEAD_DIM) == (lane[None, :] // HEAD_DIM), 1.0 / HEAD_DIM, 0.0).astype(BF16)
    qw = jnp.tile(q_norm_w[0], 2)[None]
    kw = jnp.tile(k_norm_w[0], 2)[None]
    qT, k, vT, u16, g = _inproj_call(ctx, x, mod3, norm1_w, w_in[0].astype(BF16), qw, kw, b_gate,
                                        cos_t, sin_t, seg)

    o = _attn_call(qT, k, vT, lam_q1, lam_k1, lam_q2, lam_k2, subln_w, tq=512)

    m, bzx, ccx, a16x = _s5tab_call(s5_a_re[0], s5_a_im[0], s5_log_dt[0], s5_b_re[0], s5_b_im[0],
                                    s5_c_re[0], s5_c_im[0], s5_d[0])
    sin = _s5scan_call(u16, bzx, a16x, n_ctx // CHUNK)
    y = _s5out_call(u16, m, ccx, sin, nb=512)

    x1 = _merge_call(y, o, g, x, mod3, glu_w[0].astype(BF16), glu_b, w_branch_s5[0].astype(BF16),
                     w_branch_attn[0].astype(BF16), w_out[0].astype(BF16), tm=512)

    return _ffn_call(x1, mod3, norm2_w, w_up[0].astype(BF16), conv_w[0], conv_b, w_down[0].astype(BF16), tm=512)
```

```python
import functools
import math

import jax
import jax.numpy as jnp
from jax import lax
from jax.experimental import pallas as pl
from jax.experimental.pallas import tpu as pltpu

D_MODEL = 1024
GRID_W = 64
N_HEADS = 8
HEAD_DIM = 64
V_DIM = 2 * HEAD_DIM
QK_W = N_HEADS * 2 * HEAD_DIM
ATTN_W = N_HEADS * V_DIM
S5_W = 512
S5_GROUP = 16
S5_GROUPS = S5_W // S5_GROUP
S5_STATE = 64
D_FF = 2816
ROPE_THETA = 10000.0
EPS = 1e-6
MIN_NEG_RE = -1e-4
K_OFF = QK_W
V_OFF = 2 * QK_W
U_OFF = 2 * QK_W + ATTN_W
G_OFF = U_OFF + S5_W
N_IN = G_OFF + 2 * D_MODEL
LAM_INIT = 0.8 - 0.6 * math.exp(-0.3 * 0)
Q_SCALE = HEAD_DIM ** -0.5 * math.log2(math.e)

LANES = 128
TOK_TILE = 256
IN_TILE = 512
CHUNK = 16
SLAB = LANES
N_SLABS = S5_W // SLAB
GROUPS_PER_SLAB = SLAB // S5_GROUP
SLAB_IN = CHUNK * SLAB
SLAB_STATE = GROUPS_PER_SLAB * 2 * S5_STATE
FF_CHUNK = 256
N_FF_CHUNKS = D_FF // FF_CHUNK
FF_DOWN_GROUP = 4
HALO = 8
VMEM_LIMIT = 56 * 1024 * 1024

F32 = jnp.float32
BF16 = jnp.bfloat16


def _sigmoid(x):
    return 1.0 / (1.0 + jnp.exp(-x))


def _dot(a, b):
    return jnp.dot(a, b, preferred_element_type=F32)


def _params(n_parallel, n_arbitrary=0):
    return pltpu.CompilerParams(
        dimension_semantics=("parallel",) * n_parallel + ("arbitrary",) * n_arbitrary,
        vmem_limit_bytes=VMEM_LIMIT)


def _mod_body(c_ref, w_ref, b_ref, o_ref):
    c = c_ref[...]
    s = c * _sigmoid(c)
    o_ref[...] = jnp.dot(s, w_ref[...], preferred_element_type=F32,
                         precision=lax.Precision.HIGHEST) + b_ref[...]


def _mod_call(cc, ada_w, ada_b):
    n = ada_w.shape[1]
    return pl.pallas_call(
        _mod_body,
        grid=(n // D_MODEL,),
        in_specs=[pl.BlockSpec((8, D_MODEL), lambda j: (0, 0)),
                  pl.BlockSpec((D_MODEL, D_MODEL), lambda j: (0, j)),
                  pl.BlockSpec((1, D_MODEL), lambda j: (0, j))],
        out_specs=pl.BlockSpec((8, D_MODEL), lambda j: (0, j)),
        out_shape=jax.ShapeDtypeStruct((8, n), F32),
        compiler_params=_params(1),
        name="mod",
    )(cc, ada_w, ada_b)


def _inproj_body(*refs, latent):
    if latent:
        (x_ref, mod_ref, n1w_ref, w_ref, kw_ref, seg_ref, qw_ref, bg_ref, cos_ref, sin_ref,
         k_ref, vT_ref, u_ref, qT_ref, g_ref, u_scr) = refs
    else:
        x_ref, mod_ref, n1w_ref, w_ref, kw_ref, seg_ref, k_ref, vT_ref, u_ref, u_scr = refs
    T = x_ref.shape[0]
    xt = x_ref[...]
    ms = jnp.mean(xt * xt, axis=-1, keepdims=True)
    mod = mod_ref[...]
    h = (xt * lax.rsqrt(ms + EPS)) * n1w_ref[...] * (1.0 + mod[1:2]) + mod[0:1]
    hb = h.astype(BF16)
    seg = seg_ref[...]
    if latent:
        cos = cos_ref[...]
        sin = sin_ref[...]
        first_half = (lax.broadcasted_iota(jnp.int32, cos.shape, 1) % 32) < 16

    def norm_rope(zh, w):
        y = zh * lax.rsqrt(_dot((zh * zh).astype(BF16), seg) + EPS) * w
        if not latent:
            return y
        partner = jnp.where(first_half, pltpu.roll(y, LANES - 16, 1), pltpu.roll(y, 16, 1))
        return y * cos + partner * sin

    zk = _dot(hb, w_ref[:, K_OFF:V_OFF])
    kw = kw_ref[...]
    for hh in range(N_HEADS):
        k_ref[hh] = norm_rope(zk[:, hh * LANES:(hh + 1) * LANES], kw).astype(BF16)

    zv = _dot(hb, w_ref[:, V_OFF:U_OFF])
    for hh in range(N_HEADS):
        vT_ref[hh] = zv[:, hh * LANES:(hh + 1) * LANES].T.astype(BF16)

    zu = _dot(hb, w_ref[:, U_OFF:G_OFF])
    for s in range(N_SLABS):
        u_scr[s] = zu[:, s * SLAB:(s + 1) * SLAB]
    for s in range(N_SLABS):
        for t in range(CHUNK):
            piece = u_scr[s, pl.ds(t, T // CHUNK, stride=CHUNK), :]
            u_ref[s, :, t * SLAB:(t + 1) * SLAB] = piece.astype(BF16)

    if latent:
        zq = _dot(hb, w_ref[:, 0:K_OFF])
        qw = qw_ref[...]
        row = lax.broadcasted_iota(jnp.int32, (LANES, T), 0)
        for hh in range(N_HEADS):
            q = norm_rope(zq[:, hh * LANES:(hh + 1) * LANES], qw) * Q_SCALE
            qT = q.T
            qT_ref[hh, 0] = jnp.where(row < HEAD_DIM, qT, 0.0).astype(BF16)
            qT_ref[hh, 1] = jnp.where(row >= HEAD_DIM, qT, 0.0).astype(BF16)
        zg = _dot(hb, w_ref[:, G_OFF:N_IN]) + bg_ref[...]
        g_ref[...] = _sigmoid(zg).astype(BF16)


def _inproj_call(x, mod3, mod_row, n1w, w_in_b, kw, seg, latent_args=None):
    B, n, D = x.shape
    latent = latent_args is not None
    T = latent_args[4] if latent else n
    assert n % T == 0
    const2 = lambda b, i: (0, 0)
    in_specs = [
        pl.BlockSpec((None, T, D), lambda b, i: (b, i, 0)),
        pl.BlockSpec((None, 6, D), lambda b, i: (mod_row(b), 0, 0)),
        pl.BlockSpec((1, D), const2),
        pl.BlockSpec((D, N_IN), const2),
        pl.BlockSpec((1, LANES), const2),
        pl.BlockSpec((LANES, LANES), const2),
    ]
    out_specs = [
        pl.BlockSpec((None, N_HEADS, T, LANES), lambda b, i: (b, 0, i, 0)),
        pl.BlockSpec((None, N_HEADS, V_DIM, T), lambda b, i: (b, 0, 0, i)),
        pl.BlockSpec((N_SLABS, None, T // CHUNK, SLAB_IN), lambda b, i: (0, b, i, 0)),
    ]
    out_shape = [
        jax.ShapeDtypeStruct((B, N_HEADS, n, LANES), BF16),
        jax.ShapeDtypeStruct((B, N_HEADS, V_DIM, n), BF16),
        jax.ShapeDtypeStruct((N_SLABS, B, n // CHUNK, SLAB_IN), BF16),
    ]
    args = [x, mod3, n1w, w_in_b, kw, seg]
    if latent:
        qw, bg, cos_t, sin_t, _ = latent_args
        in_specs += [pl.BlockSpec((1, LANES), const2), pl.BlockSpec((1, 2 * D), const2),
                     pl.BlockSpec((T, LANES), lambda b, i: (i, 0)), pl.BlockSpec((T, LANES), lambda b, i: (i, 0))]
        out_specs += [pl.BlockSpec((None, N_HEADS, 2, LANES, T), lambda b, i: (b, 0, 0, 0, i)),
                      pl.BlockSpec((None, T, 2 * D), lambda b, i: (b, i, 0))]
        out_shape += [jax.ShapeDtypeStruct((B, N_HEADS, 2, LANES, n), BF16),
                      jax.ShapeDtypeStruct((B, n, 2 * D), BF16)]
        args += [qw, bg, cos_t, sin_t]
    return pl.pallas_call(
        functools.partial(_inproj_body, latent=latent),
        grid=(B, n // T),
        in_specs=in_specs,
        out_specs=out_specs,
        out_shape=out_shape,
        scratch_shapes=[pltpu.VMEM((N_SLABS, T, SLAB), F32)],
        compiler_params=_params(2),
        name="in_proj_latent" if latent else "in_proj_ctx",
    )(*args)


def _attn_body(qT_ref, kc_ref, k_ref, vTc_ref, vT_ref, lq1_ref, lk1_ref, lq2_ref, lk2_ref, sw_ref, o_ref,
               s0_ref, s1_ref, m0_ref, m1_ref, *, tq):
    g = pl.program_id(0)
    tk = TOK_TILE
    n_ctx = kc_ref.shape[0]
    n_all = n_ctx + k_ref.shape[0]
    assert n_ctx == tk

    @pl.when(g == 0)
    def _():
        s1_ref[...] = jnp.zeros(s1_ref.shape, F32)
        m1_ref[...] = jnp.zeros(m1_ref.shape, F32)

    def step(s_w, m_w, s_r, m_r):
        qT = jnp.concatenate([qT_ref[0], qT_ref[1]], axis=1)
        m_prev = m_r[...]
        m8 = l8 = acc = None
        for lo in range(0, n_all, tk):
            k_tile = kc_ref[...] if lo == 0 else k_ref[lo - n_ctx:lo - n_ctx + tk, :]
            v_tile = vTc_ref[...] if lo == 0 else vT_ref[:, lo - n_ctx:lo - n_ctx + tk]
            s_new = _dot(k_tile, qT)
            s_w[lo:lo + tk, :] = s_new
            part = jnp.max(s_new.reshape(tk // 8, 8, 2 * tq), axis=0)
            m8 = part if m8 is None else jnp.maximum(m8, part)

            p = jnp.exp2(s_r[lo:lo + tk, :] - m_prev)
            part = jnp.sum(p.reshape(tk // 8, 8, 2 * tq), axis=0)
            l8 = part if l8 is None else l8 + part
            pv = _dot(v_tile, p.astype(BF16))
            acc = pv if acc is None else acc + pv
        m_w[...] = jnp.max(m8, axis=0, keepdims=True)

        oT = acc / jnp.sum(l8, axis=0, keepdims=True)
        lam = (jnp.exp(jnp.sum(lq1_ref[...] * lk1_ref[...], axis=-1, keepdims=True))
               - jnp.exp(jnp.sum(lq2_ref[...] * lk2_ref[...], axis=-1, keepdims=True)) + LAM_INIT)
        od = (oT[:, :tq] - lam * oT[:, tq:]).T
        ms = jnp.mean(od * od, axis=-1, keepdims=True)
        o_ref[...] = ((od * lax.rsqrt(ms + EPS)) * sw_ref[...] * (1.0 - LAM_INIT)).astype(BF16)

    @pl.when(g % 2 == 0)
    def _():
        step(s0_ref, m0_ref, s1_ref, m1_ref)

    @pl.when(g % 2 == 1)
    def _():
        step(s1_ref, m1_ref, s0_ref, m0_ref)


def _attn_call(qT, kc, k, vTc, vT, lq1, lk1, lq2, lk2, sw, tq):
    B, H, _, _, L = qT.shape
    n_ctx = kc.shape[2]
    n_all = n_ctx + L
    nq = L // tq
    n_tiles = B * H * nq

    def tile(t):
        return t // (H * nq), (t // nq) % H, t % nq

    def scored(g):
        return tile(jnp.minimum(g, n_tiles - 1))

    def finished(g):
        return tile(jnp.maximum(g - 1, 0))

    def q_map(g):
        b, h, i = scored(g)
        return b, h, 0, 0, i

    def k_map(g):
        b, h, _ = scored(g)
        return b, h, 0, 0

    def v_map(g):
        b, h, _ = finished(g)
        return b, h, 0, 0

    def o_map(g):
        b, h, i = finished(g)
        return b, i, h

    vec = pl.BlockSpec((1, HEAD_DIM), lambda g: (0, 0))
    return pl.pallas_call(
        functools.partial(_attn_body, tq=tq),
        grid=(n_tiles + 1,),
        in_specs=[
            pl.BlockSpec((None, None, 2, LANES, tq), q_map),
            pl.BlockSpec((None, None, n_ctx, LANES), k_map),
            pl.BlockSpec((None, None, L, LANES), k_map),
            pl.BlockSpec((None, None, V_DIM, n_ctx), v_map),
            pl.BlockSpec((None, None, V_DIM, L), v_map),
            vec, vec, vec, vec,
            pl.BlockSpec((1, V_DIM), lambda g: (0, 0)),
        ],
        out_specs=pl.BlockSpec((None, tq, V_DIM), o_map),
        out_shape=jax.ShapeDtypeStruct((B, L, ATTN_W), BF16),
        scratch_shapes=[pltpu.VMEM((n_all, 2 * tq), F32), pltpu.VMEM((n_all, 2 * tq), F32),
                        pltpu.VMEM((1, 2 * tq), F32), pltpu.VMEM((1, 2 * tq), F32)],
        compiler_params=_params(0, 1),
        name="attn",
    )(qT, kc, k, vTc, vT, lq1, lk1, lq2, lk2, sw)


def _cmul(xr, xi, yr, yi):
    return xr * yr - xi * yi, xr * yi + xi * yr


def _a_bar(are, aim, ldt):
    lr = jnp.minimum(are, MIN_NEG_RE)
    dt = jnp.exp(ldt)
    mag = jnp.exp(lr * dt)
    return lr, aim, mag * jnp.cos(aim * dt), mag * jnp.sin(aim * dt)


def _powers(ar, ai, n):
    pows = [(jnp.ones_like(ar), jnp.zeros_like(ar))]
    for _ in range(n):
        pows.append(_cmul(pows[-1][0], pows[-1][1], ar, ai))
    return pows


def _s5tab_body(aar_ref, aai_ref, adt_ref, abr_ref, abi_ref, bar_ref, bai_ref, bdt_ref, bcr_ref, bci_ref,
                car_ref, cai_ref, cdt_ref, dsk_ref, m_ref, bz_ref, cc_ref, a16_ref, lag_ref):
    d = pl.program_id(1)
    fwd = d == 0
    half = SLAB_STATE // 2
    sel = lambda a, b: (jnp.where(fwd, a[0], b[0]), jnp.where(fwd, a[1], b[1]))

    lr, li, ar, ai = _a_bar(aar_ref[...], aai_ref[...], adt_ref[...])
    den = lr * lr + li * li
    nr = ar - 1.0
    cfr = (nr * lr + ai * li) / den
    cfi = (ai * lr - nr * li) / den
    bbr, bbi = _cmul(cfr, cfi, abr_ref[...], abi_ref[...])
    pa = _powers(ar, ai, CHUNK - 1)
    row_a = lax.broadcasted_iota(jnp.int32, (SLAB, half), 0) // S5_GROUP
    col_a = lax.broadcasted_iota(jnp.int32, (SLAB, half), 1) // S5_STATE
    diag_a = row_a == col_a

    def expand_a(v):
        return jnp.where(diag_a, jnp.concatenate([v] * (half // LANES), axis=1), 0.0)

    for t in range(CHUNK):
        pw = sel(pa[CHUNK - 1 - t], pa[t])
        zr, zi = _cmul(pw[0], pw[1], bbr, bbi)
        bz_ref[t * SLAB:(t + 1) * SLAB, :] = jnp.concatenate([expand_a(zr), expand_a(zi)], axis=1).astype(BF16)

    _, _, br_, bi_ = _a_bar(bar_ref[...], bai_ref[...], bdt_ref[...])
    pb = _powers(br_, bi_, CHUNK)
    cr = bcr_ref[...]
    ci = bci_ref[...]
    row_b = lax.broadcasted_iota(jnp.int32, (half, SLAB), 0) // S5_STATE
    col_b = lax.broadcasted_iota(jnp.int32, (half, SLAB), 1) // S5_GROUP
    diag_b = row_b == col_b

    def expand_b(v):
        return jnp.where(diag_b, jnp.concatenate([v] * GROUPS_PER_SLAB, axis=0), 0.0)

    for t in range(CHUNK):
        pw = sel(pb[t + 1], pb[CHUNK - t])
        ctr, cti = _cmul(cr, ci, pw[0], pw[1])
        cc_ref[:, t * SLAB:(t + 1) * SLAB] = jnp.concatenate([expand_b(ctr), expand_b(-cti)], axis=0).astype(BF16)

    lane = lax.broadcasted_iota(jnp.int32, (SLAB, LANES), 1)
    lhs = jnp.where(lane < S5_STATE, bbr, bbi)
    r_k = lax.broadcasted_iota(jnp.int32, (SLAB, SLAB), 0)
    c_k = lax.broadcasted_iota(jnp.int32, (SLAB, SLAB), 1)
    diag_k = (r_k // S5_GROUP) == (c_k // S5_GROUP)
    sign = 1 - 2 * d
    for tau in range(CHUNK):
        ctr, cti = _cmul(cr, ci, pb[tau][0], pb[tau][1])
        rhs = jnp.concatenate([ctr, -cti], axis=0)
        ktau = jnp.where(diag_k, jnp.dot(lhs, rhs, preferred_element_type=F32,
                                         precision=lax.Precision.HIGHEST), 0.0)
        if tau == 0:
            @pl.when(fwd)
            def _():
                lag_ref[CHUNK - 1] = ktau + jnp.where(r_k == c_k, dsk_ref[...], 0.0)

            @pl.when(jnp.logical_not(fwd))
            def _():
                lag_ref[CHUNK - 1] = lag_ref[CHUNK - 1] + ktau
        else:
            lag_ref[CHUNK - 1 + sign * tau] = ktau

    @pl.when(jnp.logical_not(fwd))
    def _():
        for tp in range(CHUNK):
            for t in range(CHUNK):
                m_ref[tp * SLAB:(tp + 1) * SLAB, t * SLAB:(t + 1) * SLAB] = lag_ref[t - tp + CHUNK - 1].astype(BF16)

    _, _, xr, xi = _a_bar(car_ref[...], cai_ref[...], cdt_ref[...])
    for _ in range(4):
        xr, xi = _cmul(xr, xi, xr, xi)
    a16_ref[...] = jnp.concatenate([xr, xi], axis=1)


def _s5tab_call(s5_a_re, s5_a_im, s5_log_dt, s5_b_re, s5_b_im, s5_c_re, s5_c_im, s5_d):
    S, GS, C, P = N_SLABS, GROUPS_PER_SLAB, S5_GROUP, S5_STATE
    half = SLAB_STATE // 2

    def lay_a(a):
        return jnp.broadcast_to(a.reshape(2, S, GS, 1, 1, P), (2, S, GS, C, 2, P)).reshape(2, S, SLAB, LANES)

    def lay_b(a):
        a = a.reshape(2, S, GS, P).transpose(0, 1, 3, 2)
        return jnp.broadcast_to(a[..., None], (2, S, P, GS, C)).reshape(2, S, P, SLAB)

    ldt = jnp.broadcast_to(s5_log_dt[..., None], (2, S5_GROUPS, P))
    b_a = lambda b: jnp.broadcast_to(b.transpose(0, 1, 3, 2).reshape(2, S, GS, C, 1, P),
                                     (2, S, GS, C, 2, P)).reshape(2, S, SLAB, LANES)
    c_b = lambda c: c.reshape(2, S, GS, C, P).transpose(0, 1, 4, 2, 3).reshape(2, S, P, SLAB)
    lay_c = lambda a: a.reshape(2, S, 1, half)
    blk = lambda r, c: pl.BlockSpec((None, None, r, c), lambda s, d: (d, s, 0, 0))
    return pl.pallas_call(
        _s5tab_body,
        grid=(S, 2),
        in_specs=[blk(SLAB, LANES)] * 5 + [blk(P, SLAB)] * 5 + [blk(1, half)] * 3
                 + [pl.BlockSpec((None, 1, SLAB), lambda s, d: (s, 0, 0))],
        out_specs=[pl.BlockSpec((None, SLAB_IN, SLAB_IN), lambda s, d: (s, 0, 0)),
                   blk(SLAB_IN, SLAB_STATE), blk(SLAB_STATE, SLAB_IN), blk(1, SLAB_STATE)],
        out_shape=[jax.ShapeDtypeStruct((S, SLAB_IN, SLAB_IN), BF16),
                   jax.ShapeDtypeStruct((2, S, SLAB_IN, SLAB_STATE), BF16),
                   jax.ShapeDtypeStruct((2, S, SLAB_STATE, SLAB_IN), BF16),
                   jax.ShapeDtypeStruct((2, S, 1, SLAB_STATE), F32)],
        scratch_shapes=[pltpu.VMEM((2 * CHUNK - 1, SLAB, SLAB), F32)],
        compiler_params=_params(0, 2),
        name="s5_tab",
    )(lay_a(s5_a_re), lay_a(s5_a_im), lay_a(ldt), b_a(s5_b_re), b_a(s5_b_im),
      lay_b(s5_a_re), lay_b(s5_a_im), lay_b(ldt), c_b(s5_c_re), c_b(s5_c_im),
      lay_c(s5_a_re), lay_c(s5_a_im), lay_c(ldt), s5_d.reshape(S, 1, SLAB))


def _s5scan_body(uc_ref, ul_ref, bz_ref, a_ref, sin_ref, zc_ref, zl_ref, st_ref, *, n_cc, n_lc):
    d = pl.program_id(1)
    B = uc_ref.shape[0]
    w = bz_ref[...]
    zc_ref[...] = _dot(uc_ref[...].reshape(B * n_cc, SLAB_IN), w).reshape(B, n_cc, SLAB_STATE)
    zl_ref[...] = _dot(ul_ref[...].reshape(B * n_lc, SLAB_IN), w).reshape(B, n_lc, SLAB_STATE)
    a = a_ref[...]
    half = SLAB_STATE // 2
    a_same = jnp.concatenate([a[:, :half], a[:, :half]], axis=1)
    a_swap = jnp.concatenate([-a[:, half:], a[:, half:]], axis=1)

    def advance(s, z):
        return a_same * s + a_swap * pltpu.roll(s, half, 1) + z

    def ctx_step(i, s):
        kk = jnp.where(d == 0, i, n_cc - 1 - i)
        return tuple(advance(s[b], zc_ref[b, pl.ds(kk, 1), :]) for b in range(B))

    def lat_step(i, s):
        kk = jnp.where(d == 0, i, n_lc - 1 - i)
        for b in range(B):
            st_ref[b, pl.ds(kk, 1), :] = s[b]
        return tuple(advance(s[b], zl_ref[b, pl.ds(kk, 1), :]) for b in range(B))

    s = lax.fori_loop(0, n_cc, ctx_step, tuple(jnp.zeros((1, SLAB_STATE), F32) for _ in range(B)))
    lax.fori_loop(0, n_lc, lat_step, s)
    sin_ref[...] = st_ref[...].astype(BF16)


def _s5scan_call(uc, ul, bzx, a16x):
    S, B, n_cc, _ = uc.shape
    n_lc = ul.shape[2]
    return pl.pallas_call(
        functools.partial(_s5scan_body, n_cc=n_cc, n_lc=n_lc),
        grid=(S, 2),
        in_specs=[pl.BlockSpec((None, B, n_cc, SLAB_IN), lambda s, d: (s, 0, 0, 0)),
                  pl.BlockSpec((None, B, n_lc, SLAB_IN), lambda s, d: (s, 0, 0, 0)),
                  pl.BlockSpec((None, None, SLAB_IN, SLAB_STATE), lambda s, d: (d, s, 0, 0)),
                  pl.BlockSpec((None, None, 1, SLAB_STATE), lambda s, d: (d, s, 0, 0))],
        out_specs=pl.BlockSpec((None, None, B, n_lc, SLAB_STATE), lambda s, d: (d, s, 0, 0, 0)),
        out_shape=jax.ShapeDtypeStruct((2, S, B, n_lc, SLAB_STATE), BF16),
        scratch_shapes=[pltpu.VMEM((B, n_cc, SLAB_STATE), F32),
                        pltpu.VMEM((B, n_lc, SLAB_STATE), F32),
                        pltpu.VMEM((B, n_lc, SLAB_STATE), F32)],
        compiler_params=_params(2),
        name="s5_scan",
    )(uc, ul, bzx, a16x)


def _s5out_body(ul_ref, m_ref, cc_ref, sin_ref, y_ref):
    B, n_lc, _ = ul_ref.shape
    rows = B * n_lc
    y = _dot(ul_ref[...].reshape(rows, SLAB_IN), m_ref[...])
    y = y + _dot(sin_ref[0].reshape(rows, SLAB_STATE), cc_ref[0])
    y = y + _dot(sin_ref[1].reshape(rows, SLAB_STATE), cc_ref[1])
    y_ref[...] = y.reshape(B, n_lc, y.shape[-1])


def _s5out_call(ul, m, ccx, sin, nb):
    S, B, n_lc, _ = ul.shape
    return pl.pallas_call(
        _s5out_body,
        grid=(S, SLAB_IN // nb),
        in_specs=[pl.BlockSpec((None, B, n_lc, SLAB_IN), lambda s, j: (s, 0, 0, 0)),
                  pl.BlockSpec((None, SLAB_IN, nb), lambda s, j: (s, 0, j)),
                  pl.BlockSpec((2, None, SLAB_STATE, nb), lambda s, j: (0, s, 0, j)),
                  pl.BlockSpec((2, None, B, n_lc, SLAB_STATE), lambda s, j: (0, s, 0, 0, 0))],
        out_specs=pl.BlockSpec((None, B, n_lc, nb), lambda s, j: (s, 0, 0, j)),
        out_shape=jax.ShapeDtypeStruct((S, B, n_lc, SLAB_IN), F32),
        compiler_params=_params(2),
        name="s5_out",
    )(ul, m, ccx, sin)


def _merge_body(y_ref, o_ref, g_ref, x_ref, mod_ref, gw_ref, gb_ref, wbs_ref, wba_ref, wo_ref, out_ref, y_scr):
    n_rows = y_ref.shape[1]
    for s in range(N_SLABS):
        for t in range(CHUNK):
            y_scr[s, pl.ds(t, n_rows, stride=CHUNK), :] = y_ref[s, :, t * SLAB:(t + 1) * SLAB]
    y = jnp.concatenate([y_scr[s] for s in range(N_SLABS)], axis=1)
    hh = 0.5 * y * (1.0 + jnp.tanh(math.sqrt(2.0 / math.pi) * (y + 0.044715 * (y * y * y))))
    gl = _dot(hh.astype(BF16), gw_ref[...]) + gb_ref[...]
    ys = _dot((hh * _sigmoid(gl)).astype(BF16), wbs_ref[...])
    ya = _dot(o_ref[...], wba_ref[...])
    g = g_ref[...].astype(F32)
    mix = g[:, :D_MODEL] * ys + g[:, D_MODEL:] * ya
    out_ref[...] = x_ref[...] + mod_ref[2:3] * _dot(mix.astype(BF16), wo_ref[...])


def _merge_call(y, o, g, x, mod3, gw, gb, wbs, wba, wo, tm):
    B, L, D = x.shape
    c2 = lambda b, i: (0, 0)
    return pl.pallas_call(
        _merge_body,
        grid=(B, L // tm),
        in_specs=[pl.BlockSpec((N_SLABS, None, tm // CHUNK, SLAB_IN), lambda b, i: (0, b, i, 0)),
                  pl.BlockSpec((None, tm, ATTN_W), lambda b, i: (b, i, 0)),
                  pl.BlockSpec((None, tm, 2 * D), lambda b, i: (b, i, 0)),
                  pl.BlockSpec((None, tm, D), lambda b, i: (b, i, 0)),
                  pl.BlockSpec((None, 6, D), lambda b, i: (b, 0, 0)),
                  pl.BlockSpec((S5_W, S5_W), c2), pl.BlockSpec((1, S5_W), c2),
                  pl.BlockSpec((S5_W, D), c2), pl.BlockSpec((ATTN_W, D), c2), pl.BlockSpec((D, D), c2)],
        out_specs=pl.BlockSpec((None, tm, D), lambda b, i: (b, i, 0)),
        out_shape=jax.ShapeDtypeStruct((B, L, D), F32),
        scratch_shapes=[pltpu.VMEM((N_SLABS, tm, SLAB), F32)],
        compiler_params=_params(2),
        name="merge",
    )(y, o, g, x, mod3, gw, gb, wbs, wba, wo)


def _ffn_body(xp_ref, x_ref, xn_ref, mod_ref, n2w_ref, wup_ref, cw_ref, cb_ref, wdn_ref, out_ref, *, tm):
    i = pl.program_id(1)
    last = pl.num_programs(1) - 1
    mod = mod_ref[...]
    n2w = n2w_ref[...]

    def modulated(v):
        ms = jnp.mean(v * v, axis=-1, keepdims=True)
        return (v * lax.rsqrt(ms + EPS)) * n2w * (1.0 + mod[4:5]) + mod[3:4]

    x = x_ref[...]
    hp = modulated(xp_ref[...]) * (i > 0).astype(F32)
    hn = modulated(xn_ref[...]) * (i < last).astype(F32)
    hcat = jnp.concatenate([hp, modulated(x), hn], axis=0).astype(BF16)
    rows = tm + 2 * HALO

    def conv(u, cw, cb):
        prev = pltpu.roll(u, 1, 0)[HALO:HALO + tm]
        nxt = pltpu.roll(u, rows - 1, 0)[HALO:HALO + tm]
        return cb + prev * cw[0:1] + u[HALO:HALO + tm] * cw[1:2] + nxt * cw[2:3]

    def col(ref, part, f):
        lo = part * D_FF + f * FF_CHUNK
        return ref[:, lo:lo + FF_CHUNK]

    ups, acts, acc = {}, {}, None
    for f in range(N_FF_CHUNKS + 2):
        if f < N_FF_CHUNKS:
            ups[f] = (_dot(hcat, col(wup_ref, 0, f)), _dot(hcat, col(wup_ref, 1, f)))
        e = f - 1
        if 0 <= e < N_FF_CHUNKS:
            ua, ug = ups.pop(e)
            ya = conv(ua, col(cw_ref, 0, e), col(cb_ref, 0, e))
            half_g = 0.5 * conv(ug, col(cw_ref, 1, e), col(cb_ref, 1, e))
            acts[e] = ((half_g + half_g * jnp.tanh(half_g)) * ya).astype(BF16)
        e = f - 2
        if 0 <= e < N_FF_CHUNKS and ((e + 1) % FF_DOWN_GROUP == 0 or e == N_FF_CHUNKS - 1):
            lo = e // FF_DOWN_GROUP * FF_DOWN_GROUP
            group = jnp.concatenate([acts.pop(j) for j in range(lo, e + 1)], axis=1)
            down = _dot(group, wdn_ref[lo * FF_CHUNK:(e + 1) * FF_CHUNK, :])
            acc = down if acc is None else acc + down
    out_ref[...] = x + mod[5:6] * acc


def _ffn_call(x1, mod3, n2w, wup, cw, cb, wdn, tm):
    B, L, D = x1.shape
    nh = tm // HALO
    n_halo_blocks = L // HALO
    c2 = lambda b, i: (0, 0)
    return pl.pallas_call(
        functools.partial(_ffn_body, tm=tm),
        grid=(B, L // tm),
        in_specs=[pl.BlockSpec((None, HALO, D), lambda b, i: (b, jnp.maximum(i * nh - 1, 0), 0)),
                  pl.BlockSpec((None, tm, D), lambda b, i: (b, i, 0)),
                  pl.BlockSpec((None, HALO, D), lambda b, i: (b, jnp.minimum((i + 1) * nh, n_halo_blocks - 1), 0)),
                  pl.BlockSpec((None, 6, D), lambda b, i: (b, 0, 0)),
                  pl.BlockSpec((1, D), c2),
                  pl.BlockSpec((D, 2 * D_FF), c2),
                  pl.BlockSpec((3, 2 * D_FF), c2),
                  pl.BlockSpec((1, 2 * D_FF), c2),
                  pl.BlockSpec((D_FF, D), c2)],
        out_specs=pl.BlockSpec((None, tm, D), lambda b, i: (b, i, 0)),
        out_shape=jax.ShapeDtypeStruct((B, L, D), F32),
        compiler_params=_params(2),
        name="ffn",
    )(x1, x1, x1, mod3, n2w, wup, cw, cb, wdn)


def _rope_tables(L):
    t = jnp.arange(L)
    row = (t // GRID_W).astype(F32)
    col = (t % GRID_W).astype(F32)
    half = HEAD_DIM // 2
    inv_freq = ROPE_THETA ** (-jnp.arange(0, half, 2, dtype=F32) / half)
    lane = jnp.arange(LANES)
    freq = inv_freq[lane % (half // 2)]
    ang = jnp.where((lane % HEAD_DIM < half)[None, :], row[:, None], col[:, None]) * freq[None, :]
    sign = jnp.where(lane % half < half // 2, -1.0, 1.0)
    return jnp.cos(ang), jnp.sin(ang) * sign


def kernel(x, c, ctx, c_ctx, ada_w, ada_b, norm1_w, w_in, b_gate, q_norm_w, k_norm_w, lam_q1, lam_k1, lam_q2, lam_k2, subln_w, s5_a_re, s5_a_im, s5_log_dt, s5_b_re, s5_b_im, s5_c_re, s5_c_im, s5_d, glu_w, glu_b, w_branch_s5, w_branch_attn, w_out, norm2_w, w_up, conv_w, conv_b, w_down):
    B, L, D = x.shape
    assert ada_w.shape[0] == 1 and D == D_MODEL and B + 1 <= 8

    cc = jnp.zeros((8, D), F32).at[:B].set(c).at[B].set(c_ctx)
    mod3 = _mod_call(cc, ada_w[0], ada_b).reshape(8, 6, D)

    cos_t, sin_t = _rope_tables(L)
    lane = jnp.arange(LANES)
    seg = jnp.where((lane[:, None] // HEAD_DIM) == (lane[None, :] // HEAD_DIM), 1.0 / HEAD_DIM, 0.0).astype(BF16)
    qw = jnp.tile(q_norm_w[0], 2)[None]
    kw = jnp.tile(k_norm_w[0], 2)[None]
    w_in_b = w_in[0].astype(BF16)
    kc, vTc, uc = _inproj_call(ctx, mod3, lambda b: B, norm1_w, w_in_b, kw, seg)
    k, vT, ul, qT, g = _inproj_call(x, mod3, lambda b: b, norm1_w, w_in_b, kw, seg,
                                    latent_args=(qw, b_gate, cos_t, sin_t, IN_TILE))

    o = _attn_call(qT, kc, k, vTc, vT, lam_q1, lam_k1, lam_q2, lam_k2, subln_w, tq=512)

    m, bzx, ccx, a16x = _s5tab_call(s5_a_re[0], s5_a_im[0], s5_log_dt[0], s5_b_re[0], s5_b_im[0],
                                    s5_c_re[0], s5_c_im[0], s5_d[0])
    sin = _s5scan_call(uc, ul, bzx, a16x)
    y = _s5out_call(ul, m, ccx, sin, nb=512)

    x1 = _merge_call(y, o, g, x, mod3, glu_w[0].astype(BF16), glu_b, w_branch_s5[0].astype(BF16),
                     w_branch_attn[0].astype(BF16), w_out[0].astype(BF16), tm=512)

    return _ffn_call(x1, mod3, norm2_w, w_up[0].astype(BF16), conv_w[0], conv_b, w_down[0].astype(BF16), tm=512)
```

```python
import functools
import math

import jax
import jax.numpy as jnp
from jax import lax
from jax.experimental import pallas as pl
from jax.experimental.pallas import tpu as pltpu

D_MODEL = 1024
GRID_W = 64
N_HEADS = 8
HEAD_DIM = 64
V_DIM = 2 * HEAD_DIM
QK_W = N_HEADS * 2 * HEAD_DIM
ATTN_W = N_HEADS * V_DIM
S5_W = 512
S5_GROUP = 16
S5_GROUPS = S5_W // S5_GROUP
S5_STATE = 64
D_FF = 2816
ROPE_THETA = 10000.0
EPS = 1e-6
MIN_NEG_RE = -1e-4
K_OFF = QK_W
V_OFF = 2 * QK_W
U_OFF = 2 * QK_W + ATTN_W
G_OFF = U_OFF + S5_W
N_IN = G_OFF + 2 * D_MODEL
LAM_INIT = 0.8 - 0.6 * math.exp(-0.3 * 0)
Q_SCALE = HEAD_DIM ** -0.5 * math.log2(math.e)

LANES = 128
PAIR = 2 * LANES
TOK_TILE = 256
IN_TILE = 512
CHUNK = 16
SLAB = LANES
N_SLABS = S5_W // SLAB
GROUPS_PER_SLAB = SLAB // S5_GROUP
SLAB_IN = CHUNK * SLAB
SLAB_STATE = GROUPS_PER_SLAB * 2 * S5_STATE
FF_CHUNK = 256
N_FF_CHUNKS = D_FF // FF_CHUNK
FF_DOWN_GROUP = 4
HALO = 8
VMEM_LIMIT = 56 * 1024 * 1024

F32 = jnp.float32
BF16 = jnp.bfloat16


def _sigmoid(x):
    return 1.0 / (1.0 + jnp.exp(-x))


def _dot(a, b):
    return jnp.dot(a, b, preferred_element_type=F32)


def _params(n_parallel, n_arbitrary=0):
    return pltpu.CompilerParams(
        dimension_semantics=("parallel",) * n_parallel + ("arbitrary",) * n_arbitrary,
        vmem_limit_bytes=VMEM_LIMIT)


def _mod_body(c_ref, w_ref, b_ref, o_ref):
    c = c_ref[...]
    s = c * _sigmoid(c)
    o_ref[...] = jnp.dot(s, w_ref[...], preferred_element_type=F32,
                         precision=lax.Precision.HIGHEST) + b_ref[...]


def _mod_call(cc, ada_w, ada_b):
    n = ada_w.shape[1]
    return pl.pallas_call(
        _mod_body,
        grid=(n // D_MODEL,),
        in_specs=[pl.BlockSpec((8, D_MODEL), lambda j: (0, 0)),
                  pl.BlockSpec((D_MODEL, D_MODEL), lambda j: (0, j)),
                  pl.BlockSpec((1, D_MODEL), lambda j: (0, j))],
        out_specs=pl.BlockSpec((8, D_MODEL), lambda j: (0, j)),
        out_shape=jax.ShapeDtypeStruct((8, n), F32),
        compiler_params=_params(1),
        name="mod",
    )(cc, ada_w, ada_b)


def _inproj_body(*refs, latent):
    if latent:
        (x_ref, mod_ref, n1w_ref, w_ref, kw_ref, seg_ref, qw_ref, bg_ref, cos_ref, sin_ref,
         k_ref, vT_ref, u_ref, qT_ref, g_ref, u_scr) = refs
    else:
        x_ref, mod_ref, n1w_ref, w_ref, kw_ref, seg_ref, k_ref, vT_ref, u_ref, u_scr = refs
    T = x_ref.shape[0]
    xt = x_ref[...]
    ms = jnp.mean(xt * xt, axis=-1, keepdims=True)
    mod = mod_ref[...]
    h = (xt * lax.rsqrt(ms + EPS)) * n1w_ref[...] * (1.0 + mod[1:2]) + mod[0:1]
    hb = h.astype(BF16)
    seg = seg_ref[...]
    if latent:
        cos = cos_ref[...]
        sin = sin_ref[...]
        first_half = (lax.broadcasted_iota(jnp.int32, cos.shape, 1) % 32) < 16

    def norm_rope(z, w):
        y = z * lax.rsqrt(_dot((z * z).astype(BF16), seg) + EPS) * w
        if not latent:
            return y
        partner = jnp.where(first_half, pltpu.roll(y, PAIR - 16, 1), pltpu.roll(y, 16, 1))
        return y * cos + partner * sin

    def put_k(c, z):
        y = norm_rope(z, kw_ref[...]).astype(BF16)
        k_ref[2 * c] = y[:, :LANES]
        k_ref[2 * c + 1] = y[:, LANES:]

    def put_v(c, z):
        vT_ref[2 * c] = z[:, :LANES].T.astype(BF16)
        vT_ref[2 * c + 1] = z[:, LANES:].T.astype(BF16)

    def put_u(c, z):
        u_scr[2 * c] = z[:, :SLAB]
        u_scr[2 * c + 1] = z[:, SLAB:]
        if c == S5_W // PAIR - 1:
            for s in range(N_SLABS):
                for t in range(CHUNK):
                    piece = u_scr[s, pl.ds(t, T // CHUNK, stride=CHUNK), :]
                    u_ref[s, :, t * SLAB:(t + 1) * SLAB] = piece.astype(BF16)

    def put_q(c, z):
        y = norm_rope(z, qw_ref[...]) * Q_SCALE
        row = lax.broadcasted_iota(jnp.int32, (LANES, T), 0)
        for hh, qT in ((2 * c, y[:, :LANES].T), (2 * c + 1, y[:, LANES:].T)):
            qT_ref[hh, 0] = jnp.where(row < HEAD_DIM, qT, 0.0).astype(BF16)
            qT_ref[hh, 1] = jnp.where(row >= HEAD_DIM, qT, 0.0).astype(BF16)

    def put_g(c, z):
        lo = c * PAIR
        g_ref[:, lo:lo + PAIR] = _sigmoid(z + bg_ref[:, lo:lo + PAIR]).astype(BF16)

    sections = [(K_OFF, QK_W, put_k), (V_OFF, ATTN_W, put_v), (U_OFF, S5_W, put_u)]
    if latent:
        sections += [(0, QK_W, put_q), (G_OFF, 2 * D_MODEL, put_g)]
    tasks = [(off + c * PAIR, c, put) for off, width, put in sections for c in range(width // PAIR)]
    project = lambda lo: _dot(hb, w_ref[:, lo:lo + PAIR])
    z_next = project(tasks[0][0])
    for n, (_, c, put) in enumerate(tasks):
        z = z_next
        if n + 1 < len(tasks):
            z_next = project(tasks[n + 1][0])
        put(c, z)


def _inproj_call(x, mod3, mod_row, n1w, w_in_b, kw, seg, latent_args=None):
    B, n, D = x.shape
    latent = latent_args is not None
    T = latent_args[4] if latent else n
    assert n % T == 0
    const2 = lambda b, i: (0, 0)
    in_specs = [
        pl.BlockSpec((None, T, D), lambda b, i: (b, i, 0)),
        pl.BlockSpec((None, 6, D), lambda b, i: (mod_row(b), 0, 0)),
        pl.BlockSpec((1, D), const2),
        pl.BlockSpec((D, N_IN), const2),
        pl.BlockSpec((1, PAIR), const2),
        pl.BlockSpec((PAIR, PAIR), const2),
    ]
    out_specs = [
        pl.BlockSpec((None, N_HEADS, T, LANES), lambda b, i: (b, 0, i, 0)),
        pl.BlockSpec((None, N_HEADS, V_DIM, T), lambda b, i: (b, 0, 0, i)),
        pl.BlockSpec((N_SLABS, None, T // CHUNK, SLAB_IN), lambda b, i: (0, b, i, 0)),
    ]
    out_shape = [
        jax.ShapeDtypeStruct((B, N_HEADS, n, LANES), BF16),
        jax.ShapeDtypeStruct((B, N_HEADS, V_DIM, n), BF16),
        jax.ShapeDtypeStruct((N_SLABS, B, n // CHUNK, SLAB_IN), BF16),
    ]
    args = [x, mod3, n1w, w_in_b, kw, seg]
    if latent:
        qw, bg, cos_t, sin_t, _ = latent_args
        in_specs += [pl.BlockSpec((1, PAIR), const2), pl.BlockSpec((1, 2 * D), const2),
                     pl.BlockSpec((T, PAIR), lambda b, i: (i, 0)), pl.BlockSpec((T, PAIR), lambda b, i: (i, 0))]
        out_specs += [pl.BlockSpec((None, N_HEADS, 2, LANES, T), lambda b, i: (b, 0, 0, 0, i)),
                      pl.BlockSpec((None, T, 2 * D), lambda b, i: (b, i, 0))]
        out_shape += [jax.ShapeDtypeStruct((B, N_HEADS, 2, LANES, n), BF16),
                      jax.ShapeDtypeStruct((B, n, 2 * D), BF16)]
        args += [qw, bg, cos_t, sin_t]
    return pl.pallas_call(
        functools.partial(_inproj_body, latent=latent),
        grid=(B, n // T),
        in_specs=in_specs,
        out_specs=out_specs,
        out_shape=out_shape,
        scratch_shapes=[pltpu.VMEM((N_SLABS, T, SLAB), F32)],
        compiler_params=_params(2),
        name="in_proj_latent" if latent else "in_proj_ctx",
    )(*args)


def _attn_body(qT_ref, kc_ref, k_ref, vTc_ref, vT_ref, lq1_ref, lk1_ref, lq2_ref, lk2_ref, sw_ref, o_ref,
               s0_ref, s1_ref, m0_ref, m1_ref, acc_ref, l_ref, *, tq):
    g = pl.program_id(0)
    tk = TOK_TILE
    n_ctx = kc_ref.shape[0]
    n_all = n_ctx + k_ref.shape[0]
    assert n_ctx == tk

    @pl.when(g == 0)
    def _():
        s1_ref[...] = jnp.zeros(s1_ref.shape, F32)
        m1_ref[...] = jnp.zeros(m1_ref.shape, F32)
        acc_ref[...] = jnp.zeros(acc_ref.shape, F32)
        l_ref[...] = jnp.ones(l_ref.shape, F32)

    def step(s_w, m_w, s_r, m_r):
        oT = acc_ref[...] / jnp.sum(l_ref[...], axis=0, keepdims=True)
        lam = (jnp.exp(jnp.sum(lq1_ref[...] * lk1_ref[...], axis=-1, keepdims=True))
               - jnp.exp(jnp.sum(lq2_ref[...] * lk2_ref[...], axis=-1, keepdims=True)) + LAM_INIT)
        od = (oT[:, :tq] - lam * oT[:, tq:]).T
        ms = jnp.mean(od * od, axis=-1, keepdims=True)
        o_ref[...] = ((od * lax.rsqrt(ms + EPS)) * sw_ref[...] * (1.0 - LAM_INIT)).astype(BF16)

        qT = jnp.concatenate([qT_ref[0], qT_ref[1]], axis=1)
        m_prev = m_r[...]
        m8 = l8 = acc = None
        for lo in range(0, n_all, tk):
            k_tile = kc_ref[...] if lo == 0 else k_ref[lo - n_ctx:lo - n_ctx + tk, :]
            v_tile = vTc_ref[...] if lo == 0 else vT_ref[:, lo - n_ctx:lo - n_ctx + tk]
            s_new = _dot(k_tile, qT)
            s_w[lo:lo + tk, :] = s_new
            part = jnp.max(s_new.reshape(tk // 8, 8, 2 * tq), axis=0)
            m8 = part if m8 is None else jnp.maximum(m8, part)

            p = jnp.exp2(s_r[lo:lo + tk, :] - m_prev)
            part = jnp.sum(p.reshape(tk // 8, 8, 2 * tq), axis=0)
            l8 = part if l8 is None else l8 + part
            pv = _dot(v_tile, p.astype(BF16))
            acc = pv if acc is None else acc + pv
        m_w[...] = jnp.max(m8, axis=0, keepdims=True)
        acc_ref[...] = acc
        l_ref[...] = l8

    @pl.when(g % 2 == 0)
    def _():
        step(s0_ref, m0_ref, s1_ref, m1_ref)

    @pl.when(g % 2 == 1)
    def _():
        step(s1_ref, m1_ref, s0_ref, m0_ref)


def _attn_call(qT, kc, k, vTc, vT, lq1, lk1, lq2, lk2, sw, tq):
    B, H, _, _, L = qT.shape
    n_ctx = kc.shape[2]
    n_all = n_ctx + L
    nq = L // tq
    n_tiles = B * H * nq

    def tile(t):
        return t // (H * nq), (t // nq) % H, t % nq

    def scored(g):
        return tile(jnp.minimum(g, n_tiles - 1))

    def finished(g):
        return tile(jnp.clip(g - 1, 0, n_tiles - 1))

    def written(g):
        return tile(jnp.maximum(g - 2, 0))

    def q_map(g):
        b, h, i = scored(g)
        return b, h, 0, 0, i

    def k_map(g):
        b, h, _ = scored(g)
        return b, h, 0, 0

    def v_map(g):
        b, h, _ = finished(g)
        return b, h, 0, 0

    def o_map(g):
        b, h, i = written(g)
        return b, i, h

    vec = pl.BlockSpec((1, HEAD_DIM), lambda g: (0, 0))
    return pl.pallas_call(
        functools.partial(_attn_body, tq=tq),
        grid=(n_tiles + 2,),
        in_specs=[
            pl.BlockSpec((None, None, 2, LANES, tq), q_map),
            pl.BlockSpec((None, None, n_ctx, LANES), k_map),
            pl.BlockSpec((None, None, L, LANES), k_map),
            pl.BlockSpec((None, None, V_DIM, n_ctx), v_map),
            pl.BlockSpec((None, None, V_DIM, L), v_map),
            vec, vec, vec, vec,
            pl.BlockSpec((1, V_DIM), lambda g: (0, 0)),
        ],
        out_specs=pl.BlockSpec((None, tq, V_DIM), o_map),
        out_shape=jax.ShapeDtypeStruct((B, L, ATTN_W), BF16),
        scratch_shapes=[pltpu.VMEM((n_all, 2 * tq), F32), pltpu.VMEM((n_all, 2 * tq), F32),
                        pltpu.VMEM((1, 2 * tq), F32), pltpu.VMEM((1, 2 * tq), F32),
                        pltpu.VMEM((V_DIM, 2 * tq), F32), pltpu.VMEM((8, 2 * tq), F32)],
        compiler_params=_params(0, 1),
        name="attn",
    )(qT, kc, k, vTc, vT, lq1, lk1, lq2, lk2, sw)


def _cmul(xr, xi, yr, yi):
    return xr * yr - xi * yi, xr * yi + xi * yr


def _a_bar(are, aim, ldt):
    lr = jnp.minimum(are, MIN_NEG_RE)
    dt = jnp.exp(ldt)
    mag = jnp.exp(lr * dt)
    return lr, aim, mag * jnp.cos(aim * dt), mag * jnp.sin(aim * dt)


def _powers(ar, ai, n):
    pows = [(jnp.ones_like(ar), jnp.zeros_like(ar))]
    for _ in range(n):
        pows.append(_cmul(pows[-1][0], pows[-1][1], ar, ai))
    return pows


def _s5tab_body(aar_ref, aai_ref, adt_ref, abr_ref, abi_ref, bar_ref, bai_ref, bdt_ref, bcr_ref, bci_ref,
                car_ref, cai_ref, cdt_ref, dsk_ref, m_ref, bz_ref, cc_ref, a16_ref, lag_ref):
    d = pl.program_id(1)
    fwd = d == 0
    half = SLAB_STATE // 2
    sel = lambda a, b: (jnp.where(fwd, a[0], b[0]), jnp.where(fwd, a[1], b[1]))

    lr, li, ar, ai = _a_bar(aar_ref[...], aai_ref[...], adt_ref[...])
    den = lr * lr + li * li
    nr = ar - 1.0
    cfr = (nr * lr + ai * li) / den
    cfi = (ai * lr - nr * li) / den
    bbr, bbi = _cmul(cfr, cfi, abr_ref[...], abi_ref[...])
    pa = _powers(ar, ai, CHUNK - 1)
    row_a = lax.broadcasted_iota(jnp.int32, (SLAB, half), 0) // S5_GROUP
    col_a = lax.broadcasted_iota(jnp.int32, (SLAB, half), 1) // S5_STATE
    diag_a = row_a == col_a

    def expand_a(v):
        return jnp.where(diag_a, jnp.concatenate([v] * (half // LANES), axis=1), 0.0)

    for t in range(CHUNK):
        pw = sel(pa[CHUNK - 1 - t], pa[t])
        zr, zi = _cmul(pw[0], pw[1], bbr, bbi)
        bz_ref[t * SLAB:(t + 1) * SLAB, :] = jnp.concatenate([expand_a(zr), expand_a(zi)], axis=1).astype(BF16)

    _, _, br_, bi_ = _a_bar(bar_ref[...], bai_ref[...], bdt_ref[...])
    pb = _powers(br_, bi_, CHUNK)
    cr = bcr_ref[...]
    ci = bci_ref[...]
    row_b = lax.broadcasted_iota(jnp.int32, (half, SLAB), 0) // S5_STATE
    col_b = lax.broadcasted_iota(jnp.int32, (half, SLAB), 1) // S5_GROUP
    diag_b = row_b == col_b

    def expand_b(v):
        return jnp.where(diag_b, jnp.concatenate([v] * GROUPS_PER_SLAB, axis=0), 0.0)

    for t in range(CHUNK):
        pw = sel(pb[t + 1], pb[CHUNK - t])
        ctr, cti = _cmul(cr, ci, pw[0], pw[1])
        cc_ref[:, t * SLAB:(t + 1) * SLAB] = jnp.concatenate([expand_b(ctr), expand_b(-cti)], axis=0).astype(BF16)

    lane = lax.broadcasted_iota(jnp.int32, (SLAB, LANES), 1)
    lhs = jnp.where(lane < S5_STATE, bbr, bbi)
    r_k = lax.broadcasted_iota(jnp.int32, (SLAB, SLAB), 0)
    c_k = lax.broadcasted_iota(jnp.int32, (SLAB, SLAB), 1)
    diag_k = (r_k // S5_GROUP) == (c_k // S5_GROUP)
    sign = 1 - 2 * d
    for tau in range(CHUNK):
        ctr, cti = _cmul(cr, ci, pb[tau][0], pb[tau][1])
        rhs = jnp.concatenate([ctr, -cti], axis=0)
        ktau = jnp.where(diag_k, jnp.dot(lhs, rhs, preferred_element_type=F32,
                                         precision=lax.Precision.HIGHEST), 0.0)
        if tau == 0:
            @pl.when(fwd)
            def _():
                lag_ref[CHUNK - 1] = ktau + jnp.where(r_k == c_k, dsk_ref[...], 0.0)

            @pl.when(jnp.logical_not(fwd))
            def _():
                lag_ref[CHUNK - 1] = lag_ref[CHUNK - 1] + ktau
        else:
            lag_ref[CHUNK - 1 + sign * tau] = ktau

    @pl.when(jnp.logical_not(fwd))
    def _():
        for tp in range(CHUNK):
            for t in range(CHUNK):
                m_ref[tp * SLAB:(tp + 1) * SLAB, t * SLAB:(t + 1) * SLAB] = lag_ref[t - tp + CHUNK - 1].astype(BF16)

    _, _, xr, xi = _a_bar(car_ref[...], cai_ref[...], cdt_ref[...])
    for _ in range(4):
        xr, xi = _cmul(xr, xi, xr, xi)
    a16_ref[...] = jnp.concatenate([xr, xi], axis=1)


def _s5tab_call(s5_a_re, s5_a_im, s5_log_dt, s5_b_re, s5_b_im, s5_c_re, s5_c_im, s5_d):
    S, GS, C, P = N_SLABS, GROUPS_PER_SLAB, S5_GROUP, S5_STATE
    half = SLAB_STATE // 2

    def lay_a(a):
        return jnp.broadcast_to(a.reshape(2, S, GS, 1, 1, P), (2, S, GS, C, 2, P)).reshape(2, S, SLAB, LANES)

    def lay_b(a):
        a = a.reshape(2, S, GS, P).transpose(0, 1, 3, 2)
        return jnp.broadcast_to(a[..., None], (2, S, P, GS, C)).reshape(2, S, P, SLAB)

    ldt = jnp.broadcast_to(s5_log_dt[..., None], (2, S5_GROUPS, P))
    b_a = lambda b: jnp.broadcast_to(b.transpose(0, 1, 3, 2).reshape(2, S, GS, C, 1, P),
                                     (2, S, GS, C, 2, P)).reshape(2, S, SLAB, LANES)
    c_b = lambda c: c.reshape(2, S, GS, C, P).transpose(0, 1, 4, 2, 3).reshape(2, S, P, SLAB)
    lay_c = lambda a: a.reshape(2, S, 1, half)
    blk = lambda r, c: pl.BlockSpec((None, None, r, c), lambda s, d: (d, s, 0, 0))
    return pl.pallas_call(
        _s5tab_body,
        grid=(S, 2),
        in_specs=[blk(SLAB, LANES)] * 5 + [blk(P, SLAB)] * 5 + [blk(1, half)] * 3
                 + [pl.BlockSpec((None, 1, SLAB), lambda s, d: (s, 0, 0))],
        out_specs=[pl.BlockSpec((None, SLAB_IN, SLAB_IN), lambda s, d: (s, 0, 0)),
                   blk(SLAB_IN, SLAB_STATE), blk(SLAB_STATE, SLAB_IN), blk(1, SLAB_STATE)],
        out_shape=[jax.ShapeDtypeStruct((S, SLAB_IN, SLAB_IN), BF16),
                   jax.ShapeDtypeStruct((2, S, SLAB_IN, SLAB_STATE), BF16),
                   jax.ShapeDtypeStruct((2, S, SLAB_STATE, SLAB_IN), BF16),
                   jax.ShapeDtypeStruct((2, S, 1, SLAB_STATE), F32)],
        scratch_shapes=[pltpu.VMEM((2 * CHUNK - 1, SLAB, SLAB), F32)],
        compiler_params=_params(0, 2),
        name="s5_tab",
    )(lay_a(s5_a_re), lay_a(s5_a_im), lay_a(ldt), b_a(s5_b_re), b_a(s5_b_im),
      lay_b(s5_a_re), lay_b(s5_a_im), lay_b(ldt), c_b(s5_c_re), c_b(s5_c_im),
      lay_c(s5_a_re), lay_c(s5_a_im), lay_c(ldt), s5_d.reshape(S, 1, SLAB))


def _s5scan_body(uc_ref, ul_ref, bz_ref, a_ref, sin_ref, zc_ref, zl_ref, st_ref, *, n_cc, n_lc):
    d = pl.program_id(1)
    B = uc_ref.shape[0]
    w = bz_ref[...]
    zc_ref[...] = _dot(uc_ref[...].reshape(B * n_cc, SLAB_IN), w).reshape(B, n_cc, SLAB_STATE)
    zl_ref[...] = _dot(ul_ref[...].reshape(B * n_lc, SLAB_IN), w).reshape(B, n_lc, SLAB_STATE)
    a = a_ref[...]
    half = SLAB_STATE // 2
    a_same = jnp.concatenate([a[:, :half], a[:, :half]], axis=1)
    a_swap = jnp.concatenate([-a[:, half:], a[:, half:]], axis=1)

    def advance(s, z):
        return a_same * s + a_swap * pltpu.roll(s, half, 1) + z

    def ctx_step(i, s):
        kk = jnp.where(d == 0, i, n_cc - 1 - i)
        return tuple(advance(s[b], zc_ref[b, pl.ds(kk, 1), :]) for b in range(B))

    def lat_step(i, s):
        kk = jnp.where(d == 0, i, n_lc - 1 - i)
        for b in range(B):
            st_ref[b, pl.ds(kk, 1), :] = s[b]
        return tuple(advance(s[b], zl_ref[b, pl.ds(kk, 1), :]) for b in range(B))

    s = lax.fori_loop(0, n_cc, ctx_step, tuple(jnp.zeros((1, SLAB_STATE), F32) for _ in range(B)))
    lax.fori_loop(0, n_lc, lat_step, s)
    sin_ref[...] = st_ref[...].astype(BF16)


def _s5scan_call(uc, ul, bzx, a16x):
    S, B, n_cc, _ = uc.shape
    n_lc = ul.shape[2]
    return pl.pallas_call(
        functools.partial(_s5scan_body, n_cc=n_cc, n_lc=n_lc),
        grid=(S, 2),
        in_specs=[pl.BlockSpec((None, B, n_cc, SLAB_IN), lambda s, d: (s, 0, 0, 0)),
                  pl.BlockSpec((None, B, n_lc, SLAB_IN), lambda s, d: (s, 0, 0, 0)),
                  pl.BlockSpec((None, None, SLAB_IN, SLAB_STATE), lambda s, d: (d, s, 0, 0)),
                  pl.BlockSpec((None, None, 1, SLAB_STATE), lambda s, d: (d, s, 0, 0))],
        out_specs=pl.BlockSpec((None, None, B, n_lc, SLAB_STATE), lambda s, d: (d, s, 0, 0, 0)),
        out_shape=jax.ShapeDtypeStruct((2, S, B, n_lc, SLAB_STATE), BF16),
        scratch_shapes=[pltpu.VMEM((B, n_cc, SLAB_STATE), F32),
                        pltpu.VMEM((B, n_lc, SLAB_STATE), F32),
                        pltpu.VMEM((B, n_lc, SLAB_STATE), F32)],
        compiler_params=_params(2),
        name="s5_scan",
    )(uc, ul, bzx, a16x)


def _s5out_body(ul_ref, m_ref, cc_ref, sin_ref, y_ref):
    B, n_lc, _ = ul_ref.shape
    rows = B * n_lc
    y = _dot(ul_ref[...].reshape(rows, SLAB_IN), m_ref[...])
    y = y + _dot(sin_ref[0].reshape(rows, SLAB_STATE), cc_ref[0])
    y = y + _dot(sin_ref[1].reshape(rows, SLAB_STATE), cc_ref[1])
    y_ref[...] = y.reshape(B, n_lc, y.shape[-1])


def _s5out_call(ul, m, ccx, sin, nb):
    S, B, n_lc, _ = ul.shape
    return pl.pallas_call(
        _s5out_body,
        grid=(S, SLAB_IN // nb),
        in_specs=[pl.BlockSpec((None, B, n_lc, SLAB_IN), lambda s, j: (s, 0, 0, 0)),
                  pl.BlockSpec((None, SLAB_IN, nb), lambda s, j: (s, 0, j)),
                  pl.BlockSpec((2, None, SLAB_STATE, nb), lambda s, j: (0, s, 0, j)),
                  pl.BlockSpec((2, None, B, n_lc, SLAB_STATE), lambda s, j: (0, s, 0, 0, 0))],
        out_specs=pl.BlockSpec((None, B, n_lc, nb), lambda s, j: (s, 0, 0, j)),
        out_shape=jax.ShapeDtypeStruct((S, B, n_lc, SLAB_IN), F32),
        compiler_params=_params(2),
        name="s5_out",
    )(ul, m, ccx, sin)


def _merge_body(y_ref, o_ref, g_ref, x_ref, mod_ref, gw_ref, gb_ref, wbs_ref, wba_ref, wo_ref, out_ref, y_scr):
    ya = _dot(o_ref[...], wba_ref[...])
    n_rows = y_ref.shape[1]
    for s in range(N_SLABS):
        for t in range(CHUNK):
            y_scr[s, pl.ds(t, n_rows, stride=CHUNK), :] = y_ref[s, :, t * SLAB:(t + 1) * SLAB]
    y = jnp.concatenate([y_scr[s] for s in range(N_SLABS)], axis=1)
    hh = 0.5 * y * (1.0 + jnp.tanh(math.sqrt(2.0 / math.pi) * (y + 0.044715 * (y * y * y))))
    gl = _dot(hh.astype(BF16), gw_ref[...]) + gb_ref[...]
    ys = _dot((hh * _sigmoid(gl)).astype(BF16), wbs_ref[...])
    g = g_ref[...].astype(F32)
    mix = g[:, :D_MODEL] * ys + g[:, D_MODEL:] * ya
    out_ref[...] = x_ref[...] + mod_ref[2:3] * _dot(mix.astype(BF16), wo_ref[...])


def _merge_call(y, o, g, x, mod3, gw, gb, wbs, wba, wo, tm):
    B, L, D = x.shape
    c2 = lambda b, i: (0, 0)
    return pl.pallas_call(
        _merge_body,
        grid=(B, L // tm),
        in_specs=[pl.BlockSpec((N_SLABS, None, tm // CHUNK, SLAB_IN), lambda b, i: (0, b, i, 0)),
                  pl.BlockSpec((None, tm, ATTN_W), lambda b, i: (b, i, 0)),
                  pl.BlockSpec((None, tm, 2 * D), lambda b, i: (b, i, 0)),
                  pl.BlockSpec((None, tm, D), lambda b, i: (b, i, 0)),
                  pl.BlockSpec((None, 6, D), lambda b, i: (b, 0, 0)),
                  pl.BlockSpec((S5_W, S5_W), c2), pl.BlockSpec((1, S5_W), c2),
                  pl.BlockSpec((S5_W, D), c2), pl.BlockSpec((ATTN_W, D), c2), pl.BlockSpec((D, D), c2)],
        out_specs=pl.BlockSpec((None, tm, D), lambda b, i: (b, i, 0)),
        out_shape=jax.ShapeDtypeStruct((B, L, D), F32),
        scratch_shapes=[pltpu.VMEM((N_SLABS, tm, SLAB), F32)],
        compiler_params=_params(2),
        name="merge",
    )(y, o, g, x, mod3, gw, gb, wbs, wba, wo)


def _ffn_body(xp_ref, x_ref, xn_ref, mod_ref, n2w_ref, wup_ref, cw_ref, cb_ref, wdn_ref, out_ref, *, tm):
    i = pl.program_id(1)
    last = pl.num_programs(1) - 1
    mod = mod_ref[...]
    n2w = n2w_ref[...]

    def modulated(v):
        ms = jnp.mean(v * v, axis=-1, keepdims=True)
        return (v * lax.rsqrt(ms + EPS)) * n2w * (1.0 + mod[4:5]) + mod[3:4]

    x = x_ref[...]
    hp = modulated(xp_ref[...]) * (i > 0).astype(F32)
    hn = modulated(xn_ref[...]) * (i < last).astype(F32)
    hcat = jnp.concatenate([hp, modulated(x), hn], axis=0).astype(BF16)
    rows = tm + 2 * HALO

    def conv(u, cw, cb):
        prev = pltpu.roll(u, 1, 0)[HALO:HALO + tm]
        nxt = pltpu.roll(u, rows - 1, 0)[HALO:HALO + tm]
        return cb + prev * cw[0:1] + u[HALO:HALO + tm] * cw[1:2] + nxt * cw[2:3]

    def col(ref, part, f):
        lo = part * D_FF + f * FF_CHUNK
        return ref[:, lo:lo + FF_CHUNK]

    ups, acts, acc = {}, {}, None
    for f in range(N_FF_CHUNKS + 2):
        if f < N_FF_CHUNKS:
            ups[f] = (_dot(hcat, col(wup_ref, 0, f)), _dot(hcat, col(wup_ref, 1, f)))
        e = f - 1
        if 0 <= e < N_FF_CHUNKS:
            ua, ug = ups.pop(e)
            ya = conv(ua, col(cw_ref, 0, e), col(cb_ref, 0, e))
            half_g = 0.5 * conv(ug, col(cw_ref, 1, e), col(cb_ref, 1, e))
            acts[e] = ((half_g + half_g * jnp.tanh(half_g)) * ya).astype(BF16)
        e = f - 2
        if 0 <= e < N_FF_CHUNKS and ((e + 1) % FF_DOWN_GROUP == 0 or e == N_FF_CHUNKS - 1):
            lo = e // FF_DOWN_GROUP * FF_DOWN_GROUP
            group = jnp.concatenate([acts.pop(j) for j in range(lo, e + 1)], axis=1)
            down = _dot(group, wdn_ref[lo * FF_CHUNK:(e + 1) * FF_CHUNK, :])
            acc = down if acc is None else acc + down
    out_ref[...] = x + mod[5:6] * acc


def _ffn_call(x1, mod3, n2w, wup, cw, cb, wdn, tm):
    B, L, D = x1.shape
    nh = tm // HALO
    n_halo_blocks = L // HALO
    c2 = lambda b, i: (0, 0)
    return pl.pallas_call(
        functools.partial(_ffn_body, tm=tm),
        grid=(B, L // tm),
        in_specs=[pl.BlockSpec((None, HALO, D), lambda b, i: (b, jnp.maximum(i * nh - 1, 0), 0)),
                  pl.BlockSpec((None, tm, D), lambda b, i: (b, i, 0)),
                  pl.BlockSpec((None, HALO, D), lambda b, i: (b, jnp.minimum((i + 1) * nh, n_halo_blocks - 1), 0)),
                  pl.BlockSpec((None, 6, D), lambda b, i: (b, 0, 0)),
                  pl.BlockSpec((1, D), c2),
                  pl.BlockSpec((D, 2 * D_FF), c2),
                  pl.BlockSpec((3, 2 * D_FF), c2),
                  pl.BlockSpec((1, 2 * D_FF), c2),
                  pl.BlockSpec((D_FF, D), c2)],
        out_specs=pl.BlockSpec((None, tm, D), lambda b, i: (b, i, 0)),
        out_shape=jax.ShapeDtypeStruct((B, L, D), F32),
        compiler_params=_params(2),
        name="ffn",
    )(x1, x1, x1, mod3, n2w, wup, cw, cb, wdn)


def _rope_tables(L):
    t = jnp.arange(L)
    row = (t // GRID_W).astype(F32)
    col = (t % GRID_W).astype(F32)
    half = HEAD_DIM // 2
    inv_freq = ROPE_THETA ** (-jnp.arange(0, half, 2, dtype=F32) / half)
    lane = jnp.arange(PAIR)
    freq = inv_freq[lane % (half // 2)]
    ang = jnp.where((lane % HEAD_DIM < half)[None, :], row[:, None], col[:, None]) * freq[None, :]
    sign = jnp.where(lane % half < half // 2, -1.0, 1.0)
    return jnp.cos(ang), jnp.sin(ang) * sign


def kernel(x, c, ctx, c_ctx, ada_w, ada_b, norm1_w, w_in, b_gate, q_norm_w, k_norm_w, lam_q1, lam_k1, lam_q2, lam_k2, subln_w, s5_a_re, s5_a_im, s5_log_dt, s5_b_re, s5_b_im, s5_c_re, s5_c_im, s5_d, glu_w, glu_b, w_branch_s5, w_branch_attn, w_out, norm2_w, w_up, conv_w, conv_b, w_down):
    B, L, D = x.shape
    assert ada_w.shape[0] == 1 and D == D_MODEL and B + 1 <= 8

    cc = jnp.zeros((8, D), F32).at[:B].set(c).at[B].set(c_ctx)
    mod3 = _mod_call(cc, ada_w[0], ada_b).reshape(8, 6, D)

    cos_t, sin_t = _rope_tables(L)
    lane = jnp.arange(PAIR)
    seg = jnp.where((lane[:, None] // HEAD_DIM) == (lane[None, :] // HEAD_DIM), 1.0 / HEAD_DIM, 0.0).astype(BF16)
    qw = jnp.tile(q_norm_w[0], PAIR // HEAD_DIM)[None]
    kw = jnp.tile(k_norm_w[0], PAIR // HEAD_DIM)[None]
    w_in_b = w_in[0].astype(BF16)
    kc, vTc, uc = _inproj_call(ctx, mod3, lambda b: B, norm1_w, w_in_b, kw, seg)
    k, vT, ul, qT, g = _inproj_call(x, mod3, lambda b: b, norm1_w, w_in_b, kw, seg,
                                    latent_args=(qw, b_gate, cos_t, sin_t, IN_TILE))

    o = _attn_call(qT, kc, k, vTc, vT, lam_q1, lam_k1, lam_q2, lam_k2, subln_w, tq=512)

    m, bzx, ccx, a16x = _s5tab_call(s5_a_re[0], s5_a_im[0], s5_log_dt[0], s5_b_re[0], s5_b_im[0],
                                    s5_c_re[0], s5_c_im[0], s5_d[0])
    sin = _s5scan_call(uc, ul, bzx, a16x)
    y = _s5out_call(ul, m, ccx, sin, nb=512)

    x1 = _merge_call(y, o, g, x, mod3, glu_w[0].astype(BF16), glu_b, w_branch_s5[0].astype(BF16),
                     w_branch_attn[0].astype(BF16), w_out[0].astype(BF16), tm=512)

    return _ffn_call(x1, mod3, norm2_w, w_up[0].astype(BF16), conv_w[0], conv_b, w_down[0].astype(BF16), tm=512)
```

```python
import functools
import math

import jax
import jax.numpy as jnp
from jax import lax
from jax.experimental import pallas as pl
from jax.experimental.pallas import tpu as pltpu

D_MODEL = 1024
GRID_W = 64
N_HEADS = 8
HEAD_DIM = 64
V_DIM = 2 * HEAD_DIM
QK_W = N_HEADS * 2 * HEAD_DIM
ATTN_W = N_HEADS * V_DIM
S5_W = 512
S5_GROUP = 16
S5_GROUPS = S5_W // S5_GROUP
S5_STATE = 64
D_FF = 2816
ROPE_THETA = 10000.0
EPS = 1e-6
MIN_NEG_RE = -1e-4
K_OFF = QK_W
V_OFF = 2 * QK_W
U_OFF = 2 * QK_W + ATTN_W
G_OFF = U_OFF + S5_W
N_IN = G_OFF + 2 * D_MODEL
LAM_INIT = 0.8 - 0.6 * math.exp(-0.3 * 0)
Q_SCALE = HEAD_DIM ** -0.5 * math.log2(math.e)

LANES = 128
PAIR = 2 * LANES
TOK_TILE = 256
IN_TILE = 512
CHUNK = 16
SLAB = LANES
N_SLABS = S5_W // SLAB
GROUPS_PER_SLAB = SLAB // S5_GROUP
SLAB_IN = CHUNK * SLAB
SLAB_STATE = GROUPS_PER_SLAB * 2 * S5_STATE
FF_CHUNK = 256
N_FF_CHUNKS = D_FF // FF_CHUNK
FF_DOWN_GROUP = 4
HALO = 8
VMEM_LIMIT = 56 * 1024 * 1024

F32 = jnp.float32
BF16 = jnp.bfloat16


def _sigmoid(x):
    return 1.0 / (1.0 + jnp.exp(-x))


def _dot(a, b):
    return jnp.dot(a, b, preferred_element_type=F32)


def _params(n_parallel, n_arbitrary=0):
    return pltpu.CompilerParams(
        dimension_semantics=("parallel",) * n_parallel + ("arbitrary",) * n_arbitrary,
        vmem_limit_bytes=VMEM_LIMIT)


def _mod_body(c_ref, w_ref, b_ref, o_ref):
    c = c_ref[...]
    s = c * _sigmoid(c)
    o_ref[...] = jnp.dot(s, w_ref[...], preferred_element_type=F32,
                         precision=lax.Precision.HIGHEST) + b_ref[...]


def _mod_call(cc, ada_w, ada_b):
    n = ada_w.shape[1]
    return pl.pallas_call(
        _mod_body,
        grid=(n // D_MODEL,),
        in_specs=[pl.BlockSpec((8, D_MODEL), lambda j: (0, 0)),
                  pl.BlockSpec((D_MODEL, D_MODEL), lambda j: (0, j)),
                  pl.BlockSpec((1, D_MODEL), lambda j: (0, j))],
        out_specs=pl.BlockSpec((8, D_MODEL), lambda j: (0, j)),
        out_shape=jax.ShapeDtypeStruct((8, n), F32),
        compiler_params=_params(1),
        name="mod",
    )(cc, ada_w, ada_b)


def _inproj_body(*refs, latent):
    if latent:
        (x_ref, mod_ref, n1w_ref, w_ref, kw_ref, seg_ref, qw_ref, bg_ref, cos_ref, sin_ref,
         k_ref, vT_ref, u_ref, qT_ref, g_ref, u_scr) = refs
    else:
        x_ref, mod_ref, n1w_ref, w_ref, kw_ref, seg_ref, k_ref, vT_ref, u_ref, u_scr = refs
    T = x_ref.shape[0]
    xt = x_ref[...]
    ms = jnp.mean(xt * xt, axis=-1, keepdims=True)
    mod = mod_ref[...]
    h = (xt * lax.rsqrt(ms + EPS)) * n1w_ref[...] * (1.0 + mod[1:2]) + mod[0:1]
    hb = h.astype(BF16)
    seg = seg_ref[...]
    if latent:
        cos = cos_ref[...]
        sin = sin_ref[...]
        first_half = (lax.broadcasted_iota(jnp.int32, cos.shape, 1) % 32) < 16

    def norm_rope(z, w):
        y = z * lax.rsqrt(_dot((z * z).astype(BF16), seg) + EPS) * w
        if not latent:
            return y
        partner = jnp.where(first_half, pltpu.roll(y, PAIR - 16, 1), pltpu.roll(y, 16, 1))
        return y * cos + partner * sin

    def put_k(c, z):
        y = norm_rope(z, kw_ref[...]).astype(BF16)
        k_ref[2 * c] = y[:, :LANES]
        k_ref[2 * c + 1] = y[:, LANES:]

    def put_v(c, z):
        vT_ref[2 * c] = z[:, :LANES].T.astype(BF16)
        vT_ref[2 * c + 1] = z[:, LANES:].T.astype(BF16)

    def put_u(c, z):
        u_scr[2 * c] = z[:, :SLAB]
        u_scr[2 * c + 1] = z[:, SLAB:]
        if c == S5_W // PAIR - 1:
            for s in range(N_SLABS):
                for t in range(CHUNK):
                    piece = u_scr[s, pl.ds(t, T // CHUNK, stride=CHUNK), :]
                    u_ref[s, :, t * SLAB:(t + 1) * SLAB] = piece.astype(BF16)

    def put_q(c, z):
        y = norm_rope(z, qw_ref[...]) * Q_SCALE
        row = lax.broadcasted_iota(jnp.int32, (LANES, T), 0)
        for hh, qT in ((2 * c, y[:, :LANES].T), (2 * c + 1, y[:, LANES:].T)):
            qT_ref[hh, 0] = jnp.where(row < HEAD_DIM, qT, 0.0).astype(BF16)
            qT_ref[hh, 1] = jnp.where(row >= HEAD_DIM, qT, 0.0).astype(BF16)

    def put_g(c, z):
        lo = c * PAIR
        g_ref[:, lo:lo + PAIR] = _sigmoid(z + bg_ref[:, lo:lo + PAIR]).astype(BF16)

    sections = [(K_OFF, QK_W, put_k), (V_OFF, ATTN_W, put_v), (U_OFF, S5_W, put_u)]
    if latent:
        sections += [(0, QK_W, put_q), (G_OFF, 2 * D_MODEL, put_g)]
    tasks = [(off + c * PAIR, c, put) for off, width, put in sections for c in range(width // PAIR)]
    project = lambda lo: _dot(hb, w_ref[:, lo:lo + PAIR])
    z_next = project(tasks[0][0])
    for n, (_, c, put) in enumerate(tasks):
        z = z_next
        if n + 1 < len(tasks):
            z_next = project(tasks[n + 1][0])
        put(c, z)


def _inproj_call(x, mod3, mod_row, n1w, w_in_b, kw, seg, latent_args=None):
    B, n, D = x.shape
    latent = latent_args is not None
    T = latent_args[4] if latent else n
    assert n % T == 0
    const2 = lambda b, i: (0, 0)
    in_specs = [
        pl.BlockSpec((None, T, D), lambda b, i: (b, i, 0)),
        pl.BlockSpec((None, 6, D), lambda b, i: (mod_row(b), 0, 0)),
        pl.BlockSpec((1, D), const2),
        pl.BlockSpec((D, N_IN), const2),
        pl.BlockSpec((1, PAIR), const2),
        pl.BlockSpec((PAIR, PAIR), const2),
    ]
    out_specs = [
        pl.BlockSpec((None, N_HEADS, T, LANES), lambda b, i: (b, 0, i, 0)),
        pl.BlockSpec((None, N_HEADS, V_DIM, T), lambda b, i: (b, 0, 0, i)),
        pl.BlockSpec((N_SLABS, None, T // CHUNK, SLAB_IN), lambda b, i: (0, b, i, 0)),
    ]
    out_shape = [
        jax.ShapeDtypeStruct((B, N_HEADS, n, LANES), BF16),
        jax.ShapeDtypeStruct((B, N_HEADS, V_DIM, n), BF16),
        jax.ShapeDtypeStruct((N_SLABS, B, n // CHUNK, SLAB_IN), BF16),
    ]
    args = [x, mod3, n1w, w_in_b, kw, seg]
    if latent:
        qw, bg, cos_t, sin_t, _ = latent_args
        in_specs += [pl.BlockSpec((1, PAIR), const2), pl.BlockSpec((1, 2 * D), const2),
                     pl.BlockSpec((T, PAIR), lambda b, i: (i, 0)), pl.BlockSpec((T, PAIR), lambda b, i: (i, 0))]
        out_specs += [pl.BlockSpec((None, N_HEADS, 2, LANES, T), lambda b, i: (b, 0, 0, 0, i)),
                      pl.BlockSpec((None, T, 2 * D), lambda b, i: (b, i, 0))]
        out_shape += [jax.ShapeDtypeStruct((B, N_HEADS, 2, LANES, n), BF16),
                      jax.ShapeDtypeStruct((B, n, 2 * D), BF16)]
        args += [qw, bg, cos_t, sin_t]
    return pl.pallas_call(
        functools.partial(_inproj_body, latent=latent),
        grid=(B, n // T),
        in_specs=in_specs,
        out_specs=out_specs,
        out_shape=out_shape,
        scratch_shapes=[pltpu.VMEM((N_SLABS, T, SLAB), F32)],
        compiler_params=_params(2),
        name="in_proj_latent" if latent else "in_proj_ctx",
    )(*args)


def _attn_body(qT_ref, kc_ref, k_ref, vTc_ref, vT_ref, lq1_ref, lk1_ref, lq2_ref, lk2_ref, sw_ref, o_ref,
               s0_ref, s1_ref, m0_ref, m1_ref, acc_ref, l_ref, *, tq):
    g = pl.program_id(0)
    tk = TOK_TILE
    n_ctx = kc_ref.shape[0]
    n_all = n_ctx + k_ref.shape[0]
    assert n_ctx == tk

    @pl.when(g == 0)
    def _():
        s1_ref[...] = jnp.zeros(s1_ref.shape, F32)
        m1_ref[...] = jnp.zeros(m1_ref.shape, F32)
        acc_ref[...] = jnp.zeros(acc_ref.shape, F32)
        l_ref[...] = jnp.ones(l_ref.shape, F32)

    def step(s_w, m_w, s_r, m_r):
        oT = acc_ref[...] / jnp.sum(l_ref[...], axis=0, keepdims=True)
        lam = (jnp.exp(jnp.sum(lq1_ref[...] * lk1_ref[...], axis=-1, keepdims=True))
               - jnp.exp(jnp.sum(lq2_ref[...] * lk2_ref[...], axis=-1, keepdims=True)) + LAM_INIT)
        od = (oT[:, :tq] - lam * oT[:, tq:]).T
        ms = jnp.mean(od * od, axis=-1, keepdims=True)
        o_ref[...] = ((od * lax.rsqrt(ms + EPS)) * sw_ref[...] * (1.0 - LAM_INIT)).astype(BF16)

        qT = jnp.concatenate([qT_ref[0], qT_ref[1]], axis=1)
        m_prev = m_r[...]
        m8 = l8 = acc = None
        for lo in range(0, n_all, tk):
            k_tile = kc_ref[...] if lo == 0 else k_ref[lo - n_ctx:lo - n_ctx + tk, :]
            v_tile = vTc_ref[...] if lo == 0 else vT_ref[:, lo - n_ctx:lo - n_ctx + tk]
            s_new = _dot(k_tile, qT)
            s_w[lo:lo + tk, :] = s_new
            part = jnp.max(s_new.reshape(tk // 8, 8, 2 * tq), axis=0)
            m8 = part if m8 is None else jnp.maximum(m8, part)

            p = jnp.exp2(s_r[lo:lo + tk, :] - m_prev)
            part = jnp.sum(p.reshape(tk // 8, 8, 2 * tq), axis=0)
            l8 = part if l8 is None else l8 + part
            pv = _dot(v_tile, p.astype(BF16))
            acc = pv if acc is None else acc + pv
        m_w[...] = jnp.max(m8, axis=0, keepdims=True)
        acc_ref[...] = acc
        l_ref[...] = l8

    @pl.when(g % 2 == 0)
    def _():
        step(s0_ref, m0_ref, s1_ref, m1_ref)

    @pl.when(g % 2 == 1)
    def _():
        step(s1_ref, m1_ref, s0_ref, m0_ref)


def _attn_call(qT, kc, k, vTc, vT, lq1, lk1, lq2, lk2, sw, tq):
    B, H, _, _, L = qT.shape
    n_ctx = kc.shape[2]
    n_all = n_ctx + L
    nq = L // tq
    n_tiles = B * H * nq

    def tile(t):
        return t // (H * nq), (t // nq) % H, t % nq

    def scored(g):
        return tile(jnp.minimum(g, n_tiles - 1))

    def finished(g):
        return tile(jnp.clip(g - 1, 0, n_tiles - 1))

    def written(g):
        return tile(jnp.maximum(g - 2, 0))

    def q_map(g):
        b, h, i = scored(g)
        return b, h, 0, 0, i

    def k_map(g):
        b, h, _ = scored(g)
        return b, h, 0, 0

    def v_map(g):
        b, h, _ = finished(g)
        return b, h, 0, 0

    def o_map(g):
        b, h, i = written(g)
        return b, i, h

    vec = pl.BlockSpec((1, HEAD_DIM), lambda g: (0, 0))
    return pl.pallas_call(
        functools.partial(_attn_body, tq=tq),
        grid=(n_tiles + 2,),
        in_specs=[
            pl.BlockSpec((None, None, 2, LANES, tq), q_map),
            pl.BlockSpec((None, None, n_ctx, LANES), k_map),
            pl.BlockSpec((None, None, L, LANES), k_map),
            pl.BlockSpec((None, None, V_DIM, n_ctx), v_map),
            pl.BlockSpec((None, None, V_DIM, L), v_map),
            vec, vec, vec, vec,
            pl.BlockSpec((1, V_DIM), lambda g: (0, 0)),
        ],
        out_specs=pl.BlockSpec((None, tq, V_DIM), o_map),
        out_shape=jax.ShapeDtypeStruct((B, L, ATTN_W), BF16),
        scratch_shapes=[pltpu.VMEM((n_all, 2 * tq), F32), pltpu.VMEM((n_all, 2 * tq), F32),
                        pltpu.VMEM((1, 2 * tq), F32), pltpu.VMEM((1, 2 * tq), F32),
                        pltpu.VMEM((V_DIM, 2 * tq), F32), pltpu.VMEM((8, 2 * tq), F32)],
        compiler_params=_params(0, 1),
        name="attn",
    )(qT, kc, k, vTc, vT, lq1, lk1, lq2, lk2, sw)


def _cmul(xr, xi, yr, yi):
    return xr * yr - xi * yi, xr * yi + xi * yr


def _a_bar(are, aim, ldt):
    lr = jnp.minimum(are, MIN_NEG_RE)
    dt = jnp.exp(ldt)
    mag = jnp.exp(lr * dt)
    return lr, aim, mag * jnp.cos(aim * dt), mag * jnp.sin(aim * dt)


def _powers(ar, ai, n):
    pows = [(jnp.ones_like(ar), jnp.zeros_like(ar))]
    for _ in range(n):
        pows.append(_cmul(pows[-1][0], pows[-1][1], ar, ai))
    return pows


def _s5tab_body(aar_ref, aai_ref, adt_ref, abr_ref, abi_ref, bar_ref, bai_ref, bdt_ref, bcr_ref, bci_ref,
                car_ref, cai_ref, cdt_ref, dsk_ref, m_ref, bz_ref, cc_ref, a16_ref, lag_ref):
    d = pl.program_id(1)
    fwd = d == 0
    half = SLAB_STATE // 2
    sel = lambda a, b: (jnp.where(fwd, a[0], b[0]), jnp.where(fwd, a[1], b[1]))

    lr, li, ar, ai = _a_bar(aar_ref[...], aai_ref[...], adt_ref[...])
    den = lr * lr + li * li
    nr = ar - 1.0
    cfr = (nr * lr + ai * li) / den
    cfi = (ai * lr - nr * li) / den
    bbr, bbi = _cmul(cfr, cfi, abr_ref[...], abi_ref[...])
    pa = _powers(ar, ai, CHUNK - 1)
    row_a = lax.broadcasted_iota(jnp.int32, (SLAB, half), 0) // S5_GROUP
    col_a = lax.broadcasted_iota(jnp.int32, (SLAB, half), 1) // S5_STATE
    diag_a = row_a == col_a

    def expand_a(v):
        return jnp.where(diag_a, jnp.concatenate([v] * (half // LANES), axis=1), 0.0)

    for t in range(CHUNK):
        pw = sel(pa[CHUNK - 1 - t], pa[t])
        zr, zi = _cmul(pw[0], pw[1], bbr, bbi)
        bz_ref[t * SLAB:(t + 1) * SLAB, :] = jnp.concatenate([expand_a(zr), expand_a(zi)], axis=1).astype(BF16)

    _, _, br_, bi_ = _a_bar(bar_ref[...], bai_ref[...], bdt_ref[...])
    pb = _powers(br_, bi_, CHUNK)
    cr = bcr_ref[...]
    ci = bci_ref[...]
    row_b = lax.broadcasted_iota(jnp.int32, (half, SLAB), 0) // S5_STATE
    col_b = lax.broadcasted_iota(jnp.int32, (half, SLAB), 1) // S5_GROUP
    diag_b = row_b == col_b

    def expand_b(v):
        return jnp.where(diag_b, jnp.concatenate([v] * GROUPS_PER_SLAB, axis=0), 0.0)

    for t in range(CHUNK):
        pw = sel(pb[t + 1], pb[CHUNK - t])
        ctr, cti = _cmul(cr, ci, pw[0], pw[1])
        cc_ref[:, t * SLAB:(t + 1) * SLAB] = jnp.concatenate([expand_b(ctr), expand_b(-cti)], axis=0).astype(BF16)

    lane = lax.broadcasted_iota(jnp.int32, (SLAB, LANES), 1)
    lhs = jnp.where(lane < S5_STATE, bbr, bbi)
    r_k = lax.broadcasted_iota(jnp.int32, (SLAB, SLAB), 0)
    c_k = lax.broadcasted_iota(jnp.int32, (SLAB, SLAB), 1)
    diag_k = (r_k // S5_GROUP) == (c_k // S5_GROUP)
    sign = 1 - 2 * d
    for tau in range(CHUNK):
        ctr, cti = _cmul(cr, ci, pb[tau][0], pb[tau][1])
        rhs = jnp.concatenate([ctr, -cti], axis=0)
        ktau = jnp.where(diag_k, jnp.dot(lhs, rhs, preferred_element_type=F32,
                                         precision=lax.Precision.HIGHEST), 0.0)
        if tau == 0:
            @pl.when(fwd)
            def _():
                lag_ref[CHUNK - 1] = ktau + jnp.where(r_k == c_k, dsk_ref[...], 0.0)

            @pl.when(jnp.logical_not(fwd))
            def _():
                lag_ref[CHUNK - 1] = lag_ref[CHUNK - 1] + ktau
        else:
            lag_ref[CHUNK - 1 + sign * tau] = ktau

    @pl.when(jnp.logical_not(fwd))
    def _():
        for tp in range(CHUNK):
            for t in range(CHUNK):
                m_ref[tp * SLAB:(tp + 1) * SLAB, t * SLAB:(t + 1) * SLAB] = lag_ref[t - tp + CHUNK - 1].astype(BF16)

    _, _, xr, xi = _a_bar(car_ref[...], cai_ref[...], cdt_ref[...])
    for _ in range(4):
        xr, xi = _cmul(xr, xi, xr, xi)
    a16_ref[...] = jnp.concatenate([xr, xi], axis=1)


def _s5tab_call(s5_a_re, s5_a_im, s5_log_dt, s5_b_re, s5_b_im, s5_c_re, s5_c_im, s5_d):
    S, GS, C, P = N_SLABS, GROUPS_PER_SLAB, S5_GROUP, S5_STATE
    half = SLAB_STATE // 2

    def lay_a(a):
        return jnp.broadcast_to(a.reshape(2, S, GS, 1, 1, P), (2, S, GS, C, 2, P)).reshape(2, S, SLAB, LANES)

    def lay_b(a):
        a = a.reshape(2, S, GS, P).transpose(0, 1, 3, 2)
        return jnp.broadcast_to(a[..., None], (2, S, P, GS, C)).reshape(2, S, P, SLAB)

    ldt = jnp.broadcast_to(s5_log_dt[..., None], (2, S5_GROUPS, P))
    b_a = lambda b: jnp.broadcast_to(b.transpose(0, 1, 3, 2).reshape(2, S, GS, C, 1, P),
                                     (2, S, GS, C, 2, P)).reshape(2, S, SLAB, LANES)
    c_b = lambda c: c.reshape(2, S, GS, C, P).transpose(0, 1, 4, 2, 3).reshape(2, S, P, SLAB)
    lay_c = lambda a: a.reshape(2, S, 1, half)
    blk = lambda r, c: pl.BlockSpec((None, None, r, c), lambda s, d: (d, s, 0, 0))
    return pl.pallas_call(
        _s5tab_body,
        grid=(S, 2),
        in_specs=[blk(SLAB, LANES)] * 5 + [blk(P, SLAB)] * 5 + [blk(1, half)] * 3
                 + [pl.BlockSpec((None, 1, SLAB), lambda s, d: (s, 0, 0))],
        out_specs=[pl.BlockSpec((None, SLAB_IN, SLAB_IN), lambda s, d: (s, 0, 0)),
                   blk(SLAB_IN, SLAB_STATE), blk(SLAB_STATE, SLAB_IN), blk(1, SLAB_STATE)],
        out_shape=[jax.ShapeDtypeStruct((S, SLAB_IN, SLAB_IN), BF16),
                   jax.ShapeDtypeStruct((2, S, SLAB_IN, SLAB_STATE), BF16),
                   jax.ShapeDtypeStruct((2, S, SLAB_STATE, SLAB_IN), BF16),
                   jax.ShapeDtypeStruct((2, S, 1, SLAB_STATE), F32)],
        scratch_shapes=[pltpu.VMEM((2 * CHUNK - 1, SLAB, SLAB), F32)],
        compiler_params=_params(0, 2),
        name="s5_tab",
    )(lay_a(s5_a_re), lay_a(s5_a_im), lay_a(ldt), b_a(s5_b_re), b_a(s5_b_im),
      lay_b(s5_a_re), lay_b(s5_a_im), lay_b(ldt), c_b(s5_c_re), c_b(s5_c_im),
      lay_c(s5_a_re), lay_c(s5_a_im), lay_c(ldt), s5_d.reshape(S, 1, SLAB))


def _s5scan_body(uc_ref, ul_ref, bz_ref, a_ref, sin_ref, zc_ref, zl_ref, st_ref, *, n_cc, n_lc):
    d = pl.program_id(1)
    B = uc_ref.shape[0]
    n_lt = SLAB_STATE // LANES
    w = bz_ref[...]

    def chunk_inputs(u_ref, n, z_ref):
        z = _dot(u_ref[...].reshape(B * n, SLAB_IN), w)
        for b in range(B):
            for t in range(n_lt):
                z_ref[t, pl.ds(b, n, stride=B), :] = z[b * n:(b + 1) * n, t * LANES:(t + 1) * LANES]

    chunk_inputs(uc_ref, n_cc, zc_ref)
    chunk_inputs(ul_ref, n_lc, zl_ref)
    a = a_ref[...]
    half = n_lt // 2
    a_re = [a[:, (t % half) * LANES:(t % half + 1) * LANES] for t in range(n_lt)]
    a_im = [a[:, (t % half + half) * LANES:(t % half + half + 1) * LANES] for t in range(n_lt)]

    def advance(s, z_ref, rows):
        out = []
        for t in range(n_lt):
            other = s[(t + half) % n_lt]
            cross = -(a_im[t] * other) if t < half else a_im[t] * other
            out.append(a_re[t] * s[t] + cross + z_ref[t, rows, :])
        return tuple(out)

    def ctx_step(i, s):
        kk = jnp.where(d == 0, i, n_cc - 1 - i)
        return advance(s, zc_ref, pl.ds(pl.multiple_of(kk * B, B), B))

    def lat_step(i, s):
        kk = jnp.where(d == 0, i, n_lc - 1 - i)
        rows = pl.ds(pl.multiple_of(kk * B, B), B)
        for t in range(n_lt):
            st_ref[t, rows, :] = s[t]
        return advance(s, zl_ref, rows)

    s = lax.fori_loop(0, n_cc, ctx_step, tuple(jnp.zeros((B, LANES), F32) for _ in range(n_lt)))
    lax.fori_loop(0, n_lc, lat_step, s)
    for b in range(B):
        for t in range(n_lt):
            sin_ref[b, :, t * LANES:(t + 1) * LANES] = st_ref[t, pl.ds(b, n_lc, stride=B), :].astype(BF16)


def _s5scan_call(uc, ul, bzx, a16x):
    S, B, n_cc, _ = uc.shape
    n_lc = ul.shape[2]
    n_lt = SLAB_STATE // LANES
    return pl.pallas_call(
        functools.partial(_s5scan_body, n_cc=n_cc, n_lc=n_lc),
        grid=(S, 2),
        in_specs=[pl.BlockSpec((None, B, n_cc, SLAB_IN), lambda s, d: (s, 0, 0, 0)),
                  pl.BlockSpec((None, B, n_lc, SLAB_IN), lambda s, d: (s, 0, 0, 0)),
                  pl.BlockSpec((None, None, SLAB_IN, SLAB_STATE), lambda s, d: (d, s, 0, 0)),
                  pl.BlockSpec((None, None, 1, SLAB_STATE), lambda s, d: (d, s, 0, 0))],
        out_specs=pl.BlockSpec((None, None, B, n_lc, SLAB_STATE), lambda s, d: (d, s, 0, 0, 0)),
        out_shape=jax.ShapeDtypeStruct((2, S, B, n_lc, SLAB_STATE), BF16),
        scratch_shapes=[pltpu.VMEM((n_lt, B * n_cc, LANES), F32),
                        pltpu.VMEM((n_lt, B * n_lc, LANES), F32),
                        pltpu.VMEM((n_lt, B * n_lc, LANES), F32)],
        compiler_params=_params(2),
        name="s5_scan",
    )(uc, ul, bzx, a16x)


def _s5out_body(ul_ref, m_ref, cc_ref, sin_ref, y_ref):
    B, n_lc, _ = ul_ref.shape
    rows = B * n_lc
    y = _dot(ul_ref[...].reshape(rows, SLAB_IN), m_ref[...])
    y = y + _dot(sin_ref[0].reshape(rows, SLAB_STATE), cc_ref[0])
    y = y + _dot(sin_ref[1].reshape(rows, SLAB_STATE), cc_ref[1])
    y_ref[...] = y.reshape(B, n_lc, y.shape[-1])


def _s5out_call(ul, m, ccx, sin, nb):
    S, B, n_lc, _ = ul.shape
    return pl.pallas_call(
        _s5out_body,
        grid=(S, SLAB_IN // nb),
        in_specs=[pl.BlockSpec((None, B, n_lc, SLAB_IN), lambda s, j: (s, 0, 0, 0)),
                  pl.BlockSpec((None, SLAB_IN, nb), lambda s, j: (s, 0, j)),
                  pl.BlockSpec((2, None, SLAB_STATE, nb), lambda s, j: (0, s, 0, j)),
                  pl.BlockSpec((2, None, B, n_lc, SLAB_STATE), lambda s, j: (0, s, 0, 0, 0))],
        out_specs=pl.BlockSpec((None, B, n_lc, nb), lambda s, j: (s, 0, 0, j)),
        out_shape=jax.ShapeDtypeStruct((S, B, n_lc, SLAB_IN), F32),
        compiler_params=_params(2),
        name="s5_out",
    )(ul, m, ccx, sin)


def _merge_body(y_ref, o_ref, g_ref, x_ref, mod_ref, gw_ref, gb_ref, wbs_ref, wba_ref, wo_ref, out_ref, y_scr):
    ya = _dot(o_ref[...], wba_ref[...])
    n_rows = y_ref.shape[1]
    for s in range(N_SLABS):
        for t in range(CHUNK):
            y_scr[s, pl.ds(t, n_rows, stride=CHUNK), :] = y_ref[s, :, t * SLAB:(t + 1) * SLAB]
    y = jnp.concatenate([y_scr[s] for s in range(N_SLABS)], axis=1)
    hh = 0.5 * y * (1.0 + jnp.tanh(math.sqrt(2.0 / math.pi) * (y + 0.044715 * (y * y * y))))
    gl = _dot(hh.astype(BF16), gw_ref[...]) + gb_ref[...]
    ys = _dot((hh * _sigmoid(gl)).astype(BF16), wbs_ref[...])
    g = g_ref[...].astype(F32)
    mix = g[:, :D_MODEL] * ys + g[:, D_MODEL:] * ya
    out_ref[...] = x_ref[...] + mod_ref[2:3] * _dot(mix.astype(BF16), wo_ref[...])


def _merge_call(y, o, g, x, mod3, gw, gb, wbs, wba, wo, tm):
    B, L, D = x.shape
    c2 = lambda b, i: (0, 0)
    return pl.pallas_call(
        _merge_body,
        grid=(B, L // tm),
        in_specs=[pl.BlockSpec((N_SLABS, None, tm // CHUNK, SLAB_IN), lambda b, i: (0, b, i, 0)),
                  pl.BlockSpec((None, tm, ATTN_W), lambda b, i: (b, i, 0)),
                  pl.BlockSpec((None, tm, 2 * D), lambda b, i: (b, i, 0)),
                  pl.BlockSpec((None, tm, D), lambda b, i: (b, i, 0)),
                  pl.BlockSpec((None, 6, D), lambda b, i: (b, 0, 0)),
                  pl.BlockSpec((S5_W, S5_W), c2), pl.BlockSpec((1, S5_W), c2),
                  pl.BlockSpec((S5_W, D), c2), pl.BlockSpec((ATTN_W, D), c2), pl.BlockSpec((D, D), c2)],
        out_specs=pl.BlockSpec((None, tm, D), lambda b, i: (b, i, 0)),
        out_shape=jax.ShapeDtypeStruct((B, L, D), F32),
        scratch_shapes=[pltpu.VMEM((N_SLABS, tm, SLAB), F32)],
        compiler_params=_params(2),
        name="merge",
    )(y, o, g, x, mod3, gw, gb, wbs, wba, wo)


def _ffn_body(xp_ref, x_ref, xn_ref, mod_ref, n2w_ref, wup_ref, cw_ref, cb_ref, wdn_ref, out_ref, *, tm):
    i = pl.program_id(1)
    last = pl.num_programs(1) - 1
    mod = mod_ref[...]
    n2w = n2w_ref[...]

    def modulated(v):
        ms = jnp.mean(v * v, axis=-1, keepdims=True)
        return (v * lax.rsqrt(ms + EPS)) * n2w * (1.0 + mod[4:5]) + mod[3:4]

    x = x_ref[...]
    hp = modulated(xp_ref[...]) * (i > 0).astype(F32)
    hn = modulated(xn_ref[...]) * (i < last).astype(F32)
    hcat = jnp.concatenate([hp, modulated(x), hn], axis=0).astype(BF16)
    rows = tm + 2 * HALO

    def conv(u, cw, cb):
        prev = pltpu.roll(u, 1, 0)[HALO:HALO + tm]
        nxt = pltpu.roll(u, rows - 1, 0)[HALO:HALO + tm]
        return cb + prev * cw[0:1] + u[HALO:HALO + tm] * cw[1:2] + nxt * cw[2:3]

    def col(ref, part, f):
        lo = part * D_FF + f * FF_CHUNK
        return ref[:, lo:lo + FF_CHUNK]

    ups, acts, acc = {}, {}, None
    for f in range(N_FF_CHUNKS + 2):
        if f < N_FF_CHUNKS:
            ups[f] = (_dot(hcat, col(wup_ref, 0, f)), _dot(hcat, col(wup_ref, 1, f)))
        e = f - 1
        if 0 <= e < N_FF_CHUNKS:
            ua, ug = ups.pop(e)
            ya = conv(ua, col(cw_ref, 0, e), col(cb_ref, 0, e))
            half_g = 0.5 * conv(ug, col(cw_ref, 1, e), col(cb_ref, 1, e))
            acts[e] = ((half_g + half_g * jnp.tanh(half_g)) * ya).astype(BF16)
        e = f - 2
        if 0 <= e < N_FF_CHUNKS and ((e + 1) % FF_DOWN_GROUP == 0 or e == N_FF_CHUNKS - 1):
            lo = e // FF_DOWN_GROUP * FF_DOWN_GROUP
            group = jnp.concatenate([acts.pop(j) for j in range(lo, e + 1)], axis=1)
            down = _dot(group, wdn_ref[lo * FF_CHUNK:(e + 1) * FF_CHUNK, :])
            acc = down if acc is None else acc + down
    out_ref[...] = x + mod[5:6] * acc


def _ffn_call(x1, mod3, n2w, wup, cw, cb, wdn, tm):
    B, L, D = x1.shape
    nh = tm // HALO
    n_halo_blocks = L // HALO
    c2 = lambda b, i: (0, 0)
    return pl.pallas_call(
        functools.partial(_ffn_body, tm=tm),
        grid=(B, L // tm),
        in_specs=[pl.BlockSpec((None, HALO, D), lambda b, i: (b, jnp.maximum(i * nh - 1, 0), 0)),
                  pl.BlockSpec((None, tm, D), lambda b, i: (b, i, 0)),
                  pl.BlockSpec((None, HALO, D), lambda b, i: (b, jnp.minimum((i + 1) * nh, n_halo_blocks - 1), 0)),
                  pl.BlockSpec((None, 6, D), lambda b, i: (b, 0, 0)),
                  pl.BlockSpec((1, D), c2),
                  pl.BlockSpec((D, 2 * D_FF), c2),
                  pl.BlockSpec((3, 2 * D_FF), c2),
                  pl.BlockSpec((1, 2 * D_FF), c2),
                  pl.BlockSpec((D_FF, D), c2)],
        out_specs=pl.BlockSpec((None, tm, D), lambda b, i: (b, i, 0)),
        out_shape=jax.ShapeDtypeStruct((B, L, D), F32),
        compiler_params=_params(2),
        name="ffn",
    )(x1, x1, x1, mod3, n2w, wup, cw, cb, wdn)


def _rope_tables(L):
    t = jnp.arange(L)
    row = (t // GRID_W).astype(F32)
    col = (t % GRID_W).astype(F32)
    half = HEAD_DIM // 2
    inv_freq = ROPE_THETA ** (-jnp.arange(0, half, 2, dtype=F32) / half)
    lane = jnp.arange(PAIR)
    freq = inv_freq[lane % (half // 2)]
    ang = jnp.where((lane % HEAD_DIM < half)[None, :], row[:, None], col[:, None]) * freq[None, :]
    sign = jnp.where(lane % half < half // 2, -1.0, 1.0)
    return jnp.cos(ang), jnp.sin(ang) * sign


def kernel(x, c, ctx, c_ctx, ada_w, ada_b, norm1_w, w_in, b_gate, q_norm_w, k_norm_w, lam_q1, lam_k1, lam_q2, lam_k2, subln_w, s5_a_re, s5_a_im, s5_log_dt, s5_b_re, s5_b_im, s5_c_re, s5_c_im, s5_d, glu_w, glu_b, w_branch_s5, w_branch_attn, w_out, norm2_w, w_up, conv_w, conv_b, w_down):
    B, L, D = x.shape
    assert ada_w.shape[0] == 1 and D == D_MODEL and B + 1 <= 8

    cc = jnp.zeros((8, D), F32).at[:B].set(c).at[B].set(c_ctx)
    mod3 = _mod_call(cc, ada_w[0], ada_b).reshape(8, 6, D)

    cos_t, sin_t = _rope_tables(L)
    lane = jnp.arange(PAIR)
    seg = jnp.where((lane[:, None] // HEAD_DIM) == (lane[None, :] // HEAD_DIM), 1.0 / HEAD_DIM, 0.0).astype(BF16)
    qw = jnp.tile(q_norm_w[0], PAIR // HEAD_DIM)[None]
    kw = jnp.tile(k_norm_w[0], PAIR // HEAD_DIM)[None]
    w_in_b = w_in[0].astype(BF16)
    kc, vTc, uc = _inproj_call(ctx, mod3, lambda b: B, norm1_w, w_in_b, kw, seg)
    k, vT, ul, qT, g = _inproj_call(x, mod3, lambda b: b, norm1_w, w_in_b, kw, seg,
                                    latent_args=(qw, b_gate, cos_t, sin_t, IN_TILE))

    o = _attn_call(qT, kc, k, vTc, vT, lam_q1, lam_k1, lam_q2, lam_k2, subln_w, tq=512)

    m, bzx, ccx, a16x = _s5tab_call(s5_a_re[0], s5_a_im[0], s5_log_dt[0], s5_b_re[0], s5_b_im[0],
                                    s5_c_re[0], s5_c_im[0], s5_d[0])
    sin = _s5scan_call(uc, ul, bzx, a16x)
    y = _s5out_call(ul, m, ccx, sin, nb=512)

    x1 = _merge_call(y, o, g, x, mod3, glu_w[0].astype(BF16), glu_b, w_branch_s5[0].astype(BF16),
                     w_branch_attn[0].astype(BF16), w_out[0].astype(BF16), tm=512)

    return _ffn_call(x1, mod3, norm2_w, w_up[0].astype(BF16), conv_w[0], conv_b, w_down[0].astype(BF16), tm=512)
```

```python
import functools
import math

import jax
import jax.numpy as jnp
from jax import lax
from jax.experimental import pallas as pl
from jax.experimental.pallas import tpu as pltpu

D_MODEL = 1024
GRID_W = 64
N_HEADS = 8
HEAD_DIM = 64
V_DIM = 2 * HEAD_DIM
QK_W = N_HEADS * 2 * HEAD_DIM
ATTN_W = N_HEADS * V_DIM
S5_W = 512
S5_GROUP = 16
S5_GROUPS = S5_W // S5_GROUP
S5_STATE = 64
D_FF = 2816
ROPE_THETA = 10000.0
EPS = 1e-6
MIN_NEG_RE = -1e-4
K_OFF = QK_W
V_OFF = 2 * QK_W
U_OFF = 2 * QK_W + ATTN_W
G_OFF = U_OFF + S5_W
N_IN = G_OFF + 2 * D_MODEL
LAM_INIT = 0.8 - 0.6 * math.exp(-0.3 * 0)
Q_SCALE = HEAD_DIM ** -0.5 * math.log2(math.e)

LANES = 128
PAIR = 2 * LANES
TOK_TILE = 256
IN_TILE = 512
CHUNK = 16
SLAB = LANES
N_SLABS = S5_W // SLAB
GROUPS_PER_SLAB = SLAB // S5_GROUP
SLAB_IN = CHUNK * SLAB
SLAB_STATE = GROUPS_PER_SLAB * 2 * S5_STATE
FF_CHUNK = 256
N_FF_CHUNKS = D_FF // FF_CHUNK
FF_DOWN_GROUP = 4
HALO = 8
VMEM_LIMIT = 56 * 1024 * 1024

F32 = jnp.float32
BF16 = jnp.bfloat16


def _sigmoid(x):
    return 1.0 / (1.0 + jnp.exp(-x))


def _dot(a, b):
    return jnp.dot(a, b, preferred_element_type=F32)


def _params(n_parallel, n_arbitrary=0):
    return pltpu.CompilerParams(
        dimension_semantics=("parallel",) * n_parallel + ("arbitrary",) * n_arbitrary,
        vmem_limit_bytes=VMEM_LIMIT)


def _mod_body(c_ref, w_ref, b_ref, o_ref):
    c = c_ref[...]
    s = c * _sigmoid(c)
    o_ref[...] = jnp.dot(s, w_ref[...], preferred_element_type=F32,
                         precision=lax.Precision.HIGHEST) + b_ref[...]


def _mod_call(cc, ada_w, ada_b):
    n = ada_w.shape[1]
    return pl.pallas_call(
        _mod_body,
        grid=(n // D_MODEL,),
        in_specs=[pl.BlockSpec((8, D_MODEL), lambda j: (0, 0)),
                  pl.BlockSpec((D_MODEL, D_MODEL), lambda j: (0, j)),
                  pl.BlockSpec((1, D_MODEL), lambda j: (0, j))],
        out_specs=pl.BlockSpec((8, D_MODEL), lambda j: (0, j)),
        out_shape=jax.ShapeDtypeStruct((8, n), F32),
        compiler_params=_params(1),
        name="mod",
    )(cc, ada_w, ada_b)


def _inproj_body(*refs, latent):
    if latent:
        (x_ref, mod_ref, n1w_ref, w_ref, kw_ref, seg_ref, qw_ref, bg_ref, cos_ref, sin_ref,
         k_ref, vT_ref, u_ref, qT_ref, g_ref, u_scr) = refs
    else:
        x_ref, mod_ref, n1w_ref, w_ref, kw_ref, seg_ref, k_ref, vT_ref, u_ref, u_scr = refs
    T = x_ref.shape[0]
    xt = x_ref[...]
    ms = jnp.mean(xt * xt, axis=-1, keepdims=True)
    mod = mod_ref[...]
    h = (xt * lax.rsqrt(ms + EPS)) * n1w_ref[...] * (1.0 + mod[1:2]) + mod[0:1]
    hb = h.astype(BF16)
    seg = seg_ref[...]
    if latent:
        cos = cos_ref[...]
        sin = sin_ref[...]
        first_half = (lax.broadcasted_iota(jnp.int32, cos.shape, 1) % 32) < 16

    def norm_rope(z, w):
        y = z * lax.rsqrt(_dot((z * z).astype(BF16), seg) + EPS) * w
        if not latent:
            return y
        partner = jnp.where(first_half, pltpu.roll(y, PAIR - 16, 1), pltpu.roll(y, 16, 1))
        return y * cos + partner * sin

    def put_k(c, z):
        y = norm_rope(z, kw_ref[...]).astype(BF16)
        k_ref[2 * c] = y[:, :LANES]
        k_ref[2 * c + 1] = y[:, LANES:]

    def put_v(c, z):
        vT_ref[2 * c] = z[:, :LANES].T.astype(BF16)
        vT_ref[2 * c + 1] = z[:, LANES:].T.astype(BF16)

    def put_u(c, z):
        u_scr[2 * c] = z[:, :SLAB]
        u_scr[2 * c + 1] = z[:, SLAB:]
        if c == S5_W // PAIR - 1:
            for s in range(N_SLABS):
                for t in range(CHUNK):
                    piece = u_scr[s, pl.ds(t, T // CHUNK, stride=CHUNK), :]
                    u_ref[s, :, t * SLAB:(t + 1) * SLAB] = piece.astype(BF16)

    def put_q(c, z):
        y = norm_rope(z, qw_ref[...]) * Q_SCALE
        row = lax.broadcasted_iota(jnp.int32, (LANES, T), 0)
        for hh, qT in ((2 * c, y[:, :LANES].T), (2 * c + 1, y[:, LANES:].T)):
            qT_ref[hh, 0] = jnp.where(row < HEAD_DIM, qT, 0.0).astype(BF16)
            qT_ref[hh, 1] = jnp.where(row >= HEAD_DIM, qT, 0.0).astype(BF16)

    def put_g(c, z):
        lo = c * PAIR
        g_ref[:, lo:lo + PAIR] = _sigmoid(z + bg_ref[:, lo:lo + PAIR]).astype(BF16)

    sections = [(K_OFF, QK_W, put_k), (V_OFF, ATTN_W, put_v), (U_OFF, S5_W, put_u)]
    if latent:
        sections += [(0, QK_W, put_q), (G_OFF, 2 * D_MODEL, put_g)]
    tasks = [(off + c * PAIR, c, put) for off, width, put in sections for c in range(width // PAIR)]
    project = lambda lo: _dot(hb, w_ref[:, lo:lo + PAIR])
    z_next = project(tasks[0][0])
    for n, (_, c, put) in enumerate(tasks):
        z = z_next
        if n + 1 < len(tasks):
            z_next = project(tasks[n + 1][0])
        put(c, z)


def _inproj_call(x, mod3, mod_row, n1w, w_in_b, kw, seg, latent_args=None):
    B, n, D = x.shape
    latent = latent_args is not None
    T = latent_args[4] if latent else n
    assert n % T == 0
    const2 = lambda b, i: (0, 0)
    in_specs = [
        pl.BlockSpec((None, T, D), lambda b, i: (b, i, 0)),
        pl.BlockSpec((None, 6, D), lambda b, i: (mod_row(b), 0, 0)),
        pl.BlockSpec((1, D), const2),
        pl.BlockSpec((D, N_IN), const2),
        pl.BlockSpec((1, PAIR), const2),
        pl.BlockSpec((PAIR, PAIR), const2),
    ]
    out_specs = [
        pl.BlockSpec((None, N_HEADS, T, LANES), lambda b, i: (b, 0, i, 0)),
        pl.BlockSpec((None, N_HEADS, V_DIM, T), lambda b, i: (b, 0, 0, i)),
        pl.BlockSpec((N_SLABS, None, T // CHUNK, SLAB_IN), lambda b, i: (0, b, i, 0)),
    ]
    out_shape = [
        jax.ShapeDtypeStruct((B, N_HEADS, n, LANES), BF16),
        jax.ShapeDtypeStruct((B, N_HEADS, V_DIM, n), BF16),
        jax.ShapeDtypeStruct((N_SLABS, B, n // CHUNK, SLAB_IN), BF16),
    ]
    args = [x, mod3, n1w, w_in_b, kw, seg]
    if latent:
        qw, bg, cos_t, sin_t, _ = latent_args
        in_specs += [pl.BlockSpec((1, PAIR), const2), pl.BlockSpec((1, 2 * D), const2),
                     pl.BlockSpec((T, PAIR), lambda b, i: (i, 0)), pl.BlockSpec((T, PAIR), lambda b, i: (i, 0))]
        out_specs += [pl.BlockSpec((None, N_HEADS, 2, LANES, T), lambda b, i: (b, 0, 0, 0, i)),
                      pl.BlockSpec((None, T, 2 * D), lambda b, i: (b, i, 0))]
        out_shape += [jax.ShapeDtypeStruct((B, N_HEADS, 2, LANES, n), BF16),
                      jax.ShapeDtypeStruct((B, n, 2 * D), BF16)]
        args += [qw, bg, cos_t, sin_t]
    return pl.pallas_call(
        functools.partial(_inproj_body, latent=latent),
        grid=(B, n // T),
        in_specs=in_specs,
        out_specs=out_specs,
        out_shape=out_shape,
        scratch_shapes=[pltpu.VMEM((N_SLABS, T, SLAB), F32)],
        compiler_params=_params(2),
        name="in_proj_latent" if latent else "in_proj_ctx",
    )(*args)


def _attn_body(qT_ref, kc_ref, k_ref, vTc_ref, vT_ref, lq1_ref, lk1_ref, lq2_ref, lk2_ref, sw_ref, o_ref,
               s0_ref, s1_ref, m0_ref, m1_ref, acc_ref, l_ref, *, tq):
    g = pl.program_id(0)
    tk = TOK_TILE
    n_ctx = kc_ref.shape[0]
    n_all = n_ctx + k_ref.shape[0]
    assert n_ctx == tk

    @pl.when(g == 0)
    def _():
        s1_ref[...] = jnp.zeros(s1_ref.shape, F32)
        m1_ref[...] = jnp.zeros(m1_ref.shape, F32)
        acc_ref[...] = jnp.zeros(acc_ref.shape, F32)
        l_ref[...] = jnp.ones(l_ref.shape, F32)

    def step(s_w, m_w, s_r, m_r):
        oT = acc_ref[...] / jnp.sum(l_ref[...], axis=0, keepdims=True)
        lam = (jnp.exp(jnp.sum(lq1_ref[...] * lk1_ref[...], axis=-1, keepdims=True))
               - jnp.exp(jnp.sum(lq2_ref[...] * lk2_ref[...], axis=-1, keepdims=True)) + LAM_INIT)
        od = (oT[:, :tq] - lam * oT[:, tq:]).T
        ms = jnp.mean(od * od, axis=-1, keepdims=True)
        o_ref[...] = ((od * lax.rsqrt(ms + EPS)) * sw_ref[...] * (1.0 - LAM_INIT)).astype(BF16)

        qT = jnp.concatenate([qT_ref[0], qT_ref[1]], axis=1)
        m_prev = m_r[...]
        m8 = l8 = acc = None
        for lo in range(0, n_all, tk):
            k_tile = kc_ref[...] if lo == 0 else k_ref[lo - n_ctx:lo - n_ctx + tk, :]
            v_tile = vTc_ref[...] if lo == 0 else vT_ref[:, lo - n_ctx:lo - n_ctx + tk]
            s_new = _dot(k_tile, qT)
            s_w[lo:lo + tk, :] = s_new
            part = jnp.max(s_new.reshape(tk // 8, 8, 2 * tq), axis=0)
            m8 = part if m8 is None else jnp.maximum(m8, part)

            p = jnp.exp2(s_r[lo:lo + tk, :] - m_prev)
            part = jnp.sum(p.reshape(tk // 8, 8, 2 * tq), axis=0)
            l8 = part if l8 is None else l8 + part
            pv = _dot(v_tile, p.astype(BF16))
            acc = pv if acc is None else acc + pv
        m_w[...] = jnp.max(m8, axis=0, keepdims=True)
        acc_ref[...] = acc
        l_ref[...] = l8

    @pl.when(g % 2 == 0)
    def _():
        step(s0_ref, m0_ref, s1_ref, m1_ref)

    @pl.when(g % 2 == 1)
    def _():
        step(s1_ref, m1_ref, s0_ref, m0_ref)


def _attn_call(qT, kc, k, vTc, vT, lq1, lk1, lq2, lk2, sw, tq):
    B, H, _, _, L = qT.shape
    n_ctx = kc.shape[2]
    n_all = n_ctx + L
    nq = L // tq
    n_tiles = B * H * nq

    def tile(t):
        return t // (H * nq), (t // nq) % H, t % nq

    def scored(g):
        return tile(jnp.minimum(g, n_tiles - 1))

    def finished(g):
        return tile(jnp.clip(g - 1, 0, n_tiles - 1))

    def written(g):
        return tile(jnp.maximum(g - 2, 0))

    def q_map(g):
        b, h, i = scored(g)
        return b, h, 0, 0, i

    def k_map(g):
        b, h, _ = scored(g)
        return b, h, 0, 0

    def v_map(g):
        b, h, _ = finished(g)
        return b, h, 0, 0

    def o_map(g):
        b, h, i = written(g)
        return b, i, h

    vec = pl.BlockSpec((1, HEAD_DIM), lambda g: (0, 0))
    return pl.pallas_call(
        functools.partial(_attn_body, tq=tq),
        grid=(n_tiles + 2,),
        in_specs=[
            pl.BlockSpec((None, None, 2, LANES, tq), q_map),
            pl.BlockSpec((None, None, n_ctx, LANES), k_map),
            pl.BlockSpec((None, None, L, LANES), k_map),
            pl.BlockSpec((None, None, V_DIM, n_ctx), v_map),
            pl.BlockSpec((None, None, V_DIM, L), v_map),
            vec, vec, vec, vec,
            pl.BlockSpec((1, V_DIM), lambda g: (0, 0)),
        ],
        out_specs=pl.BlockSpec((None, tq, V_DIM), o_map),
        out_shape=jax.ShapeDtypeStruct((B, L, ATTN_W), BF16),
        scratch_shapes=[pltpu.VMEM((n_all, 2 * tq), F32), pltpu.VMEM((n_all, 2 * tq), F32),
                        pltpu.VMEM((1, 2 * tq), F32), pltpu.VMEM((1, 2 * tq), F32),
                        pltpu.VMEM((V_DIM, 2 * tq), F32), pltpu.VMEM((8, 2 * tq), F32)],
        compiler_params=_params(0, 1),
        name="attn",
    )(qT, kc, k, vTc, vT, lq1, lk1, lq2, lk2, sw)


def _cmul(xr, xi, yr, yi):
    return xr * yr - xi * yi, xr * yi + xi * yr


def _a_bar(are, aim, ldt):
    lr = jnp.minimum(are, MIN_NEG_RE)
    dt = jnp.exp(ldt)
    mag = jnp.exp(lr * dt)
    return lr, aim, mag * jnp.cos(aim * dt), mag * jnp.sin(aim * dt)


def _powers(ar, ai, n):
    pows = [(jnp.ones_like(ar), jnp.zeros_like(ar))]
    for _ in range(n):
        pows.append(_cmul(pows[-1][0], pows[-1][1], ar, ai))
    return pows


def _s5tab_body(aar_ref, aai_ref, adt_ref, abr_ref, abi_ref, bar_ref, bai_ref, bdt_ref, bcr_ref, bci_ref,
                car_ref, cai_ref, cdt_ref, dsk_ref, m_ref, bz_ref, cc_ref, a16_ref, lag_ref):
    d = pl.program_id(1)
    fwd = d == 0
    half = SLAB_STATE // 2
    sel = lambda a, b: (jnp.where(fwd, a[0], b[0]), jnp.where(fwd, a[1], b[1]))

    lr, li, ar, ai = _a_bar(aar_ref[...], aai_ref[...], adt_ref[...])
    den = lr * lr + li * li
    nr = ar - 1.0
    cfr = (nr * lr + ai * li) / den
    cfi = (ai * lr - nr * li) / den
    bbr, bbi = _cmul(cfr, cfi, abr_ref[...], abi_ref[...])
    pa = _powers(ar, ai, CHUNK - 1)
    row_a = lax.broadcasted_iota(jnp.int32, (SLAB, half), 0) // S5_GROUP
    col_a = lax.broadcasted_iota(jnp.int32, (SLAB, half), 1) // S5_STATE
    diag_a = row_a == col_a

    def expand_a(v):
        return jnp.where(diag_a, jnp.concatenate([v] * (half // LANES), axis=1), 0.0)

    for t in range(CHUNK):
        pw = sel(pa[CHUNK - 1 - t], pa[t])
        zr, zi = _cmul(pw[0], pw[1], bbr, bbi)
        bz_ref[t * SLAB:(t + 1) * SLAB, :] = jnp.concatenate([expand_a(zr), expand_a(zi)], axis=1).astype(BF16)

    _, _, br_, bi_ = _a_bar(bar_ref[...], bai_ref[...], bdt_ref[...])
    pb = _powers(br_, bi_, CHUNK)
    cr = bcr_ref[...]
    ci = bci_ref[...]
    row_b = lax.broadcasted_iota(jnp.int32, (half, SLAB), 0) // S5_STATE
    col_b = lax.broadcasted_iota(jnp.int32, (half, SLAB), 1) // S5_GROUP
    diag_b = row_b == col_b

    def expand_b(v):
        return jnp.where(diag_b, jnp.concatenate([v] * GROUPS_PER_SLAB, axis=0), 0.0)

    for t in range(CHUNK):
        pw = sel(pb[t + 1], pb[CHUNK - t])
        ctr, cti = _cmul(cr, ci, pw[0], pw[1])
        cc_ref[:, t * SLAB:(t + 1) * SLAB] = jnp.concatenate([expand_b(ctr), expand_b(-cti)], axis=0).astype(BF16)

    lane = lax.broadcasted_iota(jnp.int32, (SLAB, LANES), 1)
    lhs = jnp.where(lane < S5_STATE, bbr, bbi)
    r_k = lax.broadcasted_iota(jnp.int32, (SLAB, SLAB), 0)
    c_k = lax.broadcasted_iota(jnp.int32, (SLAB, SLAB), 1)
    diag_k = (r_k // S5_GROUP) == (c_k // S5_GROUP)
    sign = 1 - 2 * d
    for tau in range(CHUNK):
        ctr, cti = _cmul(cr, ci, pb[tau][0], pb[tau][1])
        rhs = jnp.concatenate([ctr, -cti], axis=0)
        ktau = jnp.where(diag_k, jnp.dot(lhs, rhs, preferred_element_type=F32,
                                         precision=lax.Precision.HIGHEST), 0.0)
        if tau == 0:
            @pl.when(fwd)
            def _():
                lag_ref[CHUNK - 1] = ktau + jnp.where(r_k == c_k, dsk_ref[...], 0.0)

            @pl.when(jnp.logical_not(fwd))
            def _():
                lag_ref[CHUNK - 1] = lag_ref[CHUNK - 1] + ktau
        else:
            lag_ref[CHUNK - 1 + sign * tau] = ktau

    @pl.when(jnp.logical_not(fwd))
    def _():
        lags = [lag_ref[n].astype(BF16) for n in range(2 * CHUNK - 1)]
        for tp in range(CHUNK):
            m_ref[tp * SLAB:(tp + 1) * SLAB, :] = jnp.concatenate(
                [lags[t - tp + CHUNK - 1] for t in range(CHUNK)], axis=1)

    _, _, xr, xi = _a_bar(car_ref[...], cai_ref[...], cdt_ref[...])
    for _ in range(4):
        xr, xi = _cmul(xr, xi, xr, xi)
    a16_ref[...] = jnp.concatenate([xr, xi], axis=1)


def _s5tab_call(s5_a_re, s5_a_im, s5_log_dt, s5_b_re, s5_b_im, s5_c_re, s5_c_im, s5_d):
    S, GS, C, P = N_SLABS, GROUPS_PER_SLAB, S5_GROUP, S5_STATE
    half = SLAB_STATE // 2

    def lay_a(a):
        return jnp.broadcast_to(a.reshape(2, S, GS, 1, 1, P), (2, S, GS, C, 2, P)).reshape(2, S, SLAB, LANES)

    def lay_b(a):
        a = a.reshape(2, S, GS, P).transpose(0, 1, 3, 2)
        return jnp.broadcast_to(a[..., None], (2, S, P, GS, C)).reshape(2, S, P, SLAB)

    ldt = jnp.broadcast_to(s5_log_dt[..., None], (2, S5_GROUPS, P))
    b_a = lambda b: jnp.broadcast_to(b.transpose(0, 1, 3, 2).reshape(2, S, GS, C, 1, P),
                                     (2, S, GS, C, 2, P)).reshape(2, S, SLAB, LANES)
    c_b = lambda c: c.reshape(2, S, GS, C, P).transpose(0, 1, 4, 2, 3).reshape(2, S, P, SLAB)
    lay_c = lambda a: a.reshape(2, S, 1, half)
    blk = lambda r, c: pl.BlockSpec((None, None, r, c), lambda s, d: (d, s, 0, 0))
    return pl.pallas_call(
        _s5tab_body,
        grid=(S, 2),
        in_specs=[blk(SLAB, LANES)] * 5 + [blk(P, SLAB)] * 5 + [blk(1, half)] * 3
                 + [pl.BlockSpec((None, 1, SLAB), lambda s, d: (s, 0, 0))],
        out_specs=[pl.BlockSpec((None, SLAB_IN, SLAB_IN), lambda s, d: (s, 0, 0)),
                   blk(SLAB_IN, SLAB_STATE), blk(SLAB_STATE, SLAB_IN), blk(1, SLAB_STATE)],
        out_shape=[jax.ShapeDtypeStruct((S, SLAB_IN, SLAB_IN), BF16),
                   jax.ShapeDtypeStruct((2, S, SLAB_IN, SLAB_STATE), BF16),
                   jax.ShapeDtypeStruct((2, S, SLAB_STATE, SLAB_IN), BF16),
                   jax.ShapeDtypeStruct((2, S, 1, SLAB_STATE), F32)],
        scratch_shapes=[pltpu.VMEM((2 * CHUNK - 1, SLAB, SLAB), F32)],
        compiler_params=_params(0, 2),
        name="s5_tab",
    )(lay_a(s5_a_re), lay_a(s5_a_im), lay_a(ldt), b_a(s5_b_re), b_a(s5_b_im),
      lay_b(s5_a_re), lay_b(s5_a_im), lay_b(ldt), c_b(s5_c_re), c_b(s5_c_im),
      lay_c(s5_a_re), lay_c(s5_a_im), lay_c(ldt), s5_d.reshape(S, 1, SLAB))


def _s5scan_body(uc_ref, ul_ref, bz_ref, a_ref, sin_ref, zc_ref, zl_ref, st_ref, *, n_cc, n_lc):
    d = pl.program_id(1)
    B = uc_ref.shape[0]
    n_lt = SLAB_STATE // LANES
    w = bz_ref[...]

    def chunk_inputs(u_ref, n, z_ref):
        z = _dot(u_ref[...].reshape(B * n, SLAB_IN), w)
        for b in range(B):
            for t in range(n_lt):
                z_ref[t, pl.ds(b, n, stride=B), :] = z[b * n:(b + 1) * n, t * LANES:(t + 1) * LANES]

    chunk_inputs(uc_ref, n_cc, zc_ref)
    chunk_inputs(ul_ref, n_lc, zl_ref)
    a = a_ref[...]
    half = n_lt // 2
    a_re = [a[:, (t % half) * LANES:(t % half + 1) * LANES] for t in range(n_lt)]
    a_im = [a[:, (t % half + half) * LANES:(t % half + half + 1) * LANES] for t in range(n_lt)]

    def advance(s, z_ref, rows):
        out = []
        for t in range(n_lt):
            other = s[(t + half) % n_lt]
            cross = -(a_im[t] * other) if t < half else a_im[t] * other
            out.append(a_re[t] * s[t] + cross + z_ref[t, rows, :])
        return tuple(out)

    def ctx_step(i, s):
        kk = jnp.where(d == 0, i, n_cc - 1 - i)
        return advance(s, zc_ref, pl.ds(pl.multiple_of(kk * B, B), B))

    def lat_step(i, s):
        kk = jnp.where(d == 0, i, n_lc - 1 - i)
        rows = pl.ds(pl.multiple_of(kk * B, B), B)
        for t in range(n_lt):
            st_ref[t, rows, :] = s[t]
        return advance(s, zl_ref, rows)

    s = lax.fori_loop(0, n_cc, ctx_step, tuple(jnp.zeros((B, LANES), F32) for _ in range(n_lt)))
    lax.fori_loop(0, n_lc, lat_step, s)
    for b in range(B):
        for t in range(n_lt):
            sin_ref[b, :, t * LANES:(t + 1) * LANES] = st_ref[t, pl.ds(b, n_lc, stride=B), :].astype(BF16)


def _s5scan_call(uc, ul, bzx, a16x):
    S, B, n_cc, _ = uc.shape
    n_lc = ul.shape[2]
    n_lt = SLAB_STATE // LANES
    return pl.pallas_call(
        functools.partial(_s5scan_body, n_cc=n_cc, n_lc=n_lc),
        grid=(S, 2),
        in_specs=[pl.BlockSpec((None, B, n_cc, SLAB_IN), lambda s, d: (s, 0, 0, 0)),
                  pl.BlockSpec((None, B, n_lc, SLAB_IN), lambda s, d: (s, 0, 0, 0)),
                  pl.BlockSpec((None, None, SLAB_IN, SLAB_STATE), lambda s, d: (d, s, 0, 0)),
                  pl.BlockSpec((None, None, 1, SLAB_STATE), lambda s, d: (d, s, 0, 0))],
        out_specs=pl.BlockSpec((None, None, B, n_lc, SLAB_STATE), lambda s, d: (d, s, 0, 0, 0)),
        out_shape=jax.ShapeDtypeStruct((2, S, B, n_lc, SLAB_STATE), BF16),
        scratch_shapes=[pltpu.VMEM((n_lt, B * n_cc, LANES), F32),
                        pltpu.VMEM((n_lt, B * n_lc, LANES), F32),
                        pltpu.VMEM((n_lt, B * n_lc, LANES), F32)],
        compiler_params=_params(2),
        name="s5_scan",
    )(uc, ul, bzx, a16x)


def _s5out_body(ul_ref, m_ref, cc_ref, sin_ref, y_ref):
    B, n_lc, _ = ul_ref.shape
    rows = B * n_lc
    y = _dot(ul_ref[...].reshape(rows, SLAB_IN), m_ref[...])
    y = y + _dot(sin_ref[0].reshape(rows, SLAB_STATE), cc_ref[0])
    y = y + _dot(sin_ref[1].reshape(rows, SLAB_STATE), cc_ref[1])
    y_ref[...] = y.reshape(B, n_lc, y.shape[-1])


def _s5out_call(ul, m, ccx, sin, nb):
    S, B, n_lc, _ = ul.shape
    return pl.pallas_call(
        _s5out_body,
        grid=(S, SLAB_IN // nb),
        in_specs=[pl.BlockSpec((None, B, n_lc, SLAB_IN), lambda s, j: (s, 0, 0, 0)),
                  pl.BlockSpec((None, SLAB_IN, nb), lambda s, j: (s, 0, j)),
                  pl.BlockSpec((2, None, SLAB_STATE, nb), lambda s, j: (0, s, 0, j)),
                  pl.BlockSpec((2, None, B, n_lc, SLAB_STATE), lambda s, j: (0, s, 0, 0, 0))],
        out_specs=pl.BlockSpec((None, B, n_lc, nb), lambda s, j: (s, 0, 0, j)),
        out_shape=jax.ShapeDtypeStruct((S, B, n_lc, SLAB_IN), F32),
        compiler_params=_params(2),
        name="s5_out",
    )(ul, m, ccx, sin)


def _merge_body(y_ref, o_ref, g_ref, x_ref, mod_ref, gw_ref, gb_ref, wbs_ref, wba_ref, wo_ref, out_ref, y_scr):
    ya = _dot(o_ref[...], wba_ref[...])
    n_rows = y_ref.shape[1]
    for s in range(N_SLABS):
        for t in range(CHUNK):
            y_scr[s, pl.ds(t, n_rows, stride=CHUNK), :] = y_ref[s, :, t * SLAB:(t + 1) * SLAB]
    y = jnp.concatenate([y_scr[s] for s in range(N_SLABS)], axis=1)
    hh = 0.5 * y * (1.0 + jnp.tanh(math.sqrt(2.0 / math.pi) * (y + 0.044715 * (y * y * y))))
    gl = _dot(hh.astype(BF16), gw_ref[...]) + gb_ref[...]
    ys = _dot((hh * _sigmoid(gl)).astype(BF16), wbs_ref[...])
    g = g_ref[...].astype(F32)
    mix = g[:, :D_MODEL] * ys + g[:, D_MODEL:] * ya
    out_ref[...] = x_ref[...] + mod_ref[2:3] * _dot(mix.astype(BF16), wo_ref[...])


def _merge_call(y, o, g, x, mod3, gw, gb, wbs, wba, wo, tm):
    B, L, D = x.shape
    c2 = lambda b, i: (0, 0)
    return pl.pallas_call(
        _merge_body,
        grid=(B, L // tm),
        in_specs=[pl.BlockSpec((N_SLABS, None, tm // CHUNK, SLAB_IN), lambda b, i: (0, b, i, 0)),
                  pl.BlockSpec((None, tm, ATTN_W), lambda b, i: (b, i, 0)),
                  pl.BlockSpec((None, tm, 2 * D), lambda b, i: (b, i, 0)),
                  pl.BlockSpec((None, tm, D), lambda b, i: (b, i, 0)),
                  pl.BlockSpec((None, 6, D), lambda b, i: (b, 0, 0)),
                  pl.BlockSpec((S5_W, S5_W), c2), pl.BlockSpec((1, S5_W), c2),
                  pl.BlockSpec((S5_W, D), c2), pl.BlockSpec((ATTN_W, D), c2), pl.BlockSpec((D, D), c2)],
        out_specs=pl.BlockSpec((None, tm, D), lambda b, i: (b, i, 0)),
        out_shape=jax.ShapeDtypeStruct((B, L, D), F32),
        scratch_shapes=[pltpu.VMEM((N_SLABS, tm, SLAB), F32)],
        compiler_params=_params(2),
        name="merge",
    )(y, o, g, x, mod3, gw, gb, wbs, wba, wo)


def _ffn_body(xp_ref, x_ref, xn_ref, mod_ref, n2w_ref, wup_ref, cw_ref, cb_ref, wdn_ref, out_ref, *, tm):
    i = pl.program_id(1)
    last = pl.num_programs(1) - 1
    mod = mod_ref[...]
    n2w = n2w_ref[...]

    def modulated(v):
        ms = jnp.mean(v * v, axis=-1, keepdims=True)
        return (v * lax.rsqrt(ms + EPS)) * n2w * (1.0 + mod[4:5]) + mod[3:4]

    x = x_ref[...]
    hp = modulated(xp_ref[...]) * (i > 0).astype(F32)
    hn = modulated(xn_ref[...]) * (i < last).astype(F32)
    hcat = jnp.concatenate([hp, modulated(x), hn], axis=0).astype(BF16)
    rows = tm + 2 * HALO

    def conv(u, cw, cb):
        prev = pltpu.roll(u, 1, 0)[HALO:HALO + tm]
        nxt = pltpu.roll(u, rows - 1, 0)[HALO:HALO + tm]
        return cb + prev * cw[0:1] + u[HALO:HALO + tm] * cw[1:2] + nxt * cw[2:3]

    def col(ref, part, f):
        lo = part * D_FF + f * FF_CHUNK
        return ref[:, lo:lo + FF_CHUNK]

    ups, acts, acc = {}, {}, None
    for f in range(N_FF_CHUNKS + 2):
        if f < N_FF_CHUNKS:
            ups[f] = (_dot(hcat, col(wup_ref, 0, f)), _dot(hcat, col(wup_ref, 1, f)))
        e = f - 1
        if 0 <= e < N_FF_CHUNKS:
            ua, ug = ups.pop(e)
            ya = conv(ua, col(cw_ref, 0, e), col(cb_ref, 0, e))
            half_g = 0.5 * conv(ug, col(cw_ref, 1, e), col(cb_ref, 1, e))
            acts[e] = ((half_g + half_g * jnp.tanh(half_g)) * ya).astype(BF16)
        e = f - 2
        if 0 <= e < N_FF_CHUNKS and ((e + 1) % FF_DOWN_GROUP == 0 or e == N_FF_CHUNKS - 1):
            lo = e // FF_DOWN_GROUP * FF_DOWN_GROUP
            group = jnp.concatenate([acts.pop(j) for j in range(lo, e + 1)], axis=1)
            down = _dot(group, wdn_ref[lo * FF_CHUNK:(e + 1) * FF_CHUNK, :])
            acc = down if acc is None else acc + down
    out_ref[...] = x + mod[5:6] * acc


def _ffn_call(x1, mod3, n2w, wup, cw, cb, wdn, tm):
    B, L, D = x1.shape
    nh = tm // HALO
    n_halo_blocks = L // HALO
    c2 = lambda b, i: (0, 0)
    return pl.pallas_call(
        functools.partial(_ffn_body, tm=tm),
        grid=(B, L // tm),
        in_specs=[pl.BlockSpec((None, HALO, D), lambda b, i: (b, jnp.maximum(i * nh - 1, 0), 0)),
                  pl.BlockSpec((None, tm, D), lambda b, i: (b, i, 0)),
                  pl.BlockSpec((None, HALO, D), lambda b, i: (b, jnp.minimum((i + 1) * nh, n_halo_blocks - 1), 0)),
                  pl.BlockSpec((None, 6, D), lambda b, i: (b, 0, 0)),
                  pl.BlockSpec((1, D), c2),
                  pl.BlockSpec((D, 2 * D_FF), c2),
                  pl.BlockSpec((3, 2 * D_FF), c2),
                  pl.BlockSpec((1, 2 * D_FF), c2),
                  pl.BlockSpec((D_FF, D), c2)],
        out_specs=pl.BlockSpec((None, tm, D), lambda b, i: (b, i, 0)),
        out_shape=jax.ShapeDtypeStruct((B, L, D), F32),
        compiler_params=_params(2),
        name="ffn",
    )(x1, x1, x1, mod3, n2w, wup, cw, cb, wdn)


def _rope_tables(L):
    half = HEAD_DIM // 2
    inv_freq = ROPE_THETA ** (-jnp.arange(0, half, 2, dtype=F32) / half)
    lane = jnp.arange(PAIR)
    freq = inv_freq[lane % (half // 2)]
    sign = jnp.where(lane % half < half // 2, -1.0, 1.0)
    row_lane = (lane % HEAD_DIM < half)[None, None, :]
    ang_r = jnp.arange(L // GRID_W, dtype=F32)[:, None] * freq[None, :]
    ang_c = jnp.arange(GRID_W, dtype=F32)[:, None] * freq[None, :]
    table = lambda f: jnp.where(row_lane, f(ang_r)[:, None, :], f(ang_c)[None, :, :]).reshape(L, PAIR)
    return table(jnp.cos), table(lambda a: jnp.sin(a) * sign)


def kernel(x, c, ctx, c_ctx, ada_w, ada_b, norm1_w, w_in, b_gate, q_norm_w, k_norm_w, lam_q1, lam_k1, lam_q2, lam_k2, subln_w, s5_a_re, s5_a_im, s5_log_dt, s5_b_re, s5_b_im, s5_c_re, s5_c_im, s5_d, glu_w, glu_b, w_branch_s5, w_branch_attn, w_out, norm2_w, w_up, conv_w, conv_b, w_down):
    B, L, D = x.shape
    assert ada_w.shape[0] == 1 and D == D_MODEL and B + 1 <= 8

    cc = jnp.zeros((8, D), F32).at[:B].set(c).at[B].set(c_ctx)
    mod3 = _mod_call(cc, ada_w[0], ada_b).reshape(8, 6, D)

    cos_t, sin_t = _rope_tables(L)
    lane = jnp.arange(PAIR)
    seg = jnp.where((lane[:, None] // HEAD_DIM) == (lane[None, :] // HEAD_DIM), 1.0 / HEAD_DIM, 0.0).astype(BF16)
    qw = jnp.tile(q_norm_w[0], PAIR // HEAD_DIM)[None]
    kw = jnp.tile(k_norm_w[0], PAIR // HEAD_DIM)[None]
    w_in_b = w_in[0].astype(BF16)
    kc, vTc, uc = _inproj_call(ctx, mod3, lambda b: B, norm1_w, w_in_b, kw, seg)
    k, vT, ul, qT, g = _inproj_call(x, mod3, lambda b: b, norm1_w, w_in_b, kw, seg,
                                    latent_args=(qw, b_gate, cos_t, sin_t, IN_TILE))

    o = _attn_call(qT, kc, k, vTc, vT, lam_q1, lam_k1, lam_q2, lam_k2, subln_w, tq=512)

    m, bzx, ccx, a16x = _s5tab_call(s5_a_re[0], s5_a_im[0], s5_log_dt[0], s5_b_re[0], s5_b_im[0],
                                    s5_c_re[0], s5_c_im[0], s5_d[0])
    sin = _s5scan_call(uc, ul, bzx, a16x)
    y = _s5out_call(ul, m, ccx, sin, nb=512)

    x1 = _merge_call(y, o, g, x, mod3, glu_w[0].astype(BF16), glu_b, w_branch_s5[0].astype(BF16),
                     w_branch_attn[0].astype(BF16), w_out[0].astype(BF16), tm=512)

    return _ffn_call(x1, mod3, norm2_w, w_up[0].astype(BF16), conv_w[0], conv_b, w_down[0].astype(BF16), tm=512)
```

```python
import functools
import math

import jax
import jax.numpy as jnp
from jax import lax
from jax.experimental import pallas as pl
from jax.experimental.pallas import tpu as pltpu

D_MODEL = 1024
GRID_W = 64
N_HEADS = 8
HEAD_DIM = 64
V_DIM = 2 * HEAD_DIM
QK_W = N_HEADS * 2 * HEAD_DIM
ATTN_W = N_HEADS * V_DIM
S5_W = 512
S5_GROUP = 16
S5_GROUPS = S5_W // S5_GROUP
S5_STATE = 64
D_FF = 2816
ROPE_THETA = 10000.0
EPS = 1e-6
MIN_NEG_RE = -1e-4
K_OFF = QK_W
V_OFF = 2 * QK_W
U_OFF = 2 * QK_W + ATTN_W
G_OFF = U_OFF + S5_W
N_IN = G_OFF + 2 * D_MODEL
LAM_INIT = 0.8 - 0.6 * math.exp(-0.3 * 0)
Q_SCALE = HEAD_DIM ** -0.5 * math.log2(math.e)

LANES = 128
PAIR = 2 * LANES
TOK_TILE = 256
ATTN_COLS = 256
IN_TILE = 512
CHUNK = 16
SLAB = LANES
N_SLABS = S5_W // SLAB
GROUPS_PER_SLAB = SLAB // S5_GROUP
SLAB_IN = CHUNK * SLAB
SLAB_STATE = GROUPS_PER_SLAB * 2 * S5_STATE
FF_CHUNK = 256
N_FF_CHUNKS = D_FF // FF_CHUNK
FF_DOWN_GROUP = 4
HALO = 8
VMEM_LIMIT = 56 * 1024 * 1024

F32 = jnp.float32
BF16 = jnp.bfloat16


def _sigmoid(x):
    return 1.0 / (1.0 + jnp.exp(-x))


def _dot(a, b):
    return jnp.dot(a, b, preferred_element_type=F32)


def _params(n_parallel, n_arbitrary=0):
    return pltpu.CompilerParams(
        dimension_semantics=("parallel",) * n_parallel + ("arbitrary",) * n_arbitrary,
        vmem_limit_bytes=VMEM_LIMIT)


def _mod_body(c_ref, w_ref, b_ref, o_ref):
    c = c_ref[...]
    s = c * _sigmoid(c)
    o_ref[...] = jnp.dot(s, w_ref[...], preferred_element_type=F32,
                         precision=lax.Precision.HIGHEST) + b_ref[...]


def _mod_call(cc, ada_w, ada_b):
    n = ada_w.shape[1]
    return pl.pallas_call(
        _mod_body,
        grid=(n // D_MODEL,),
        in_specs=[pl.BlockSpec((8, D_MODEL), lambda j: (0, 0)),
                  pl.BlockSpec((D_MODEL, D_MODEL), lambda j: (0, j)),
                  pl.BlockSpec((1, D_MODEL), lambda j: (0, j))],
        out_specs=pl.BlockSpec((8, D_MODEL), lambda j: (0, j)),
        out_shape=jax.ShapeDtypeStruct((8, n), F32),
        compiler_params=_params(1),
        name="mod",
    )(cc, ada_w, ada_b)


def _inproj_body(*refs, latent):
    if latent:
        (x_ref, mod_ref, n1w_ref, w_ref, kw_ref, seg_ref, qw_ref, bg_ref, cos_ref, sin_ref,
         k_ref, vT_ref, u_ref, qT_ref, g_ref, u_scr) = refs
    else:
        x_ref, mod_ref, n1w_ref, w_ref, kw_ref, seg_ref, k_ref, vT_ref, u_ref, u_scr = refs
    T = x_ref.shape[0]
    xt = x_ref[...]
    ms = jnp.mean(xt * xt, axis=-1, keepdims=True)
    mod = mod_ref[...]
    h = (xt * lax.rsqrt(ms + EPS)) * n1w_ref[...] * (1.0 + mod[1:2]) + mod[0:1]
    hb = h.astype(BF16)
    seg = seg_ref[...]
    if latent:
        cos = cos_ref[...]
        sin = sin_ref[...]
        first_half = (lax.broadcasted_iota(jnp.int32, cos.shape, 1) % 32) < 16

    def norm_rope(z, w):
        y = z * lax.rsqrt(_dot((z * z).astype(BF16), seg) + EPS) * w
        if not latent:
            return y
        partner = jnp.where(first_half, pltpu.roll(y, PAIR - 16, 1), pltpu.roll(y, 16, 1))
        return y * cos + partner * sin

    def put_k(c, z):
        y = norm_rope(z, kw_ref[...]).astype(BF16)
        k_ref[2 * c] = y[:, :LANES]
        k_ref[2 * c + 1] = y[:, LANES:]

    def put_v(c, z):
        vT_ref[2 * c] = z[:, :LANES].T.astype(BF16)
        vT_ref[2 * c + 1] = z[:, LANES:].T.astype(BF16)

    def put_u(c, z):
        u_scr[2 * c] = z[:, :SLAB]
        u_scr[2 * c + 1] = z[:, SLAB:]
        if c == S5_W // PAIR - 1:
            for s in range(N_SLABS):
                for t in range(CHUNK):
                    piece = u_scr[s, pl.ds(t, T // CHUNK, stride=CHUNK), :]
                    u_ref[s, :, t * SLAB:(t + 1) * SLAB] = piece.astype(BF16)

    def put_q(c, z):
        y = norm_rope(z, qw_ref[...]) * Q_SCALE
        row = lax.broadcasted_iota(jnp.int32, (LANES, T), 0)
        for hh, qT in ((2 * c, y[:, :LANES].T), (2 * c + 1, y[:, LANES:].T)):
            qT_ref[hh, 0] = jnp.where(row < HEAD_DIM, qT, 0.0).astype(BF16)
            qT_ref[hh, 1] = jnp.where(row >= HEAD_DIM, qT, 0.0).astype(BF16)

    def put_g(c, z):
        lo = c * PAIR
        g_ref[:, lo:lo + PAIR] = _sigmoid(z + bg_ref[:, lo:lo + PAIR]).astype(BF16)

    sections = [(K_OFF, QK_W, put_k), (V_OFF, ATTN_W, put_v), (U_OFF, S5_W, put_u)]
    if latent:
        sections += [(0, QK_W, put_q), (G_OFF, 2 * D_MODEL, put_g)]
    tasks = [(off + c * PAIR, c, put) for off, width, put in sections for c in range(width // PAIR)]
    project = lambda lo: _dot(hb, w_ref[:, lo:lo + PAIR])
    z_next = project(tasks[0][0])
    for n, (_, c, put) in enumerate(tasks):
        z = z_next
        if n + 1 < len(tasks):
            z_next = project(tasks[n + 1][0])
        put(c, z)


def _inproj_call(x, mod3, mod_row, n1w, w_in_b, kw, seg, latent_args=None):
    B, n, D = x.shape
    latent = latent_args is not None
    T = latent_args[4] if latent else n
    assert n % T == 0
    const2 = lambda b, i: (0, 0)
    in_specs = [
        pl.BlockSpec((None, T, D), lambda b, i: (b, i, 0)),
        pl.BlockSpec((None, 6, D), lambda b, i: (mod_row(b), 0, 0)),
        pl.BlockSpec((1, D), const2),
        pl.BlockSpec((D, N_IN), const2),
        pl.BlockSpec((1, PAIR), const2),
        pl.BlockSpec((PAIR, PAIR), const2),
    ]
    out_specs = [
        pl.BlockSpec((None, N_HEADS, T, LANES), lambda b, i: (b, 0, i, 0)),
        pl.BlockSpec((None, N_HEADS, V_DIM, T), lambda b, i: (b, 0, 0, i)),
        pl.BlockSpec((N_SLABS, None, T // CHUNK, SLAB_IN), lambda b, i: (0, b, i, 0)),
    ]
    out_shape = [
        jax.ShapeDtypeStruct((B, N_HEADS, n, LANES), BF16),
        jax.ShapeDtypeStruct((B, N_HEADS, V_DIM, n), BF16),
        jax.ShapeDtypeStruct((N_SLABS, B, n // CHUNK, SLAB_IN), BF16),
    ]
    args = [x, mod3, n1w, w_in_b, kw, seg]
    if latent:
        qw, bg, cos_t, sin_t, _ = latent_args
        in_specs += [pl.BlockSpec((1, PAIR), const2), pl.BlockSpec((1, 2 * D), const2),
                     pl.BlockSpec((T, PAIR), lambda b, i: (i, 0)), pl.BlockSpec((T, PAIR), lambda b, i: (i, 0))]
        out_specs += [pl.BlockSpec((None, N_HEADS, 2, LANES, T), lambda b, i: (b, 0, 0, 0, i)),
                      pl.BlockSpec((None, T, 2 * D), lambda b, i: (b, i, 0))]
        out_shape += [jax.ShapeDtypeStruct((B, N_HEADS, 2, LANES, n), BF16),
                      jax.ShapeDtypeStruct((B, n, 2 * D), BF16)]
        args += [qw, bg, cos_t, sin_t]
    return pl.pallas_call(
        functools.partial(_inproj_body, latent=latent),
        grid=(B, n // T),
        in_specs=in_specs,
        out_specs=out_specs,
        out_shape=out_shape,
        scratch_shapes=[pltpu.VMEM((N_SLABS, T, SLAB), F32)],
        compiler_params=_params(2),
        name="in_proj_latent" if latent else "in_proj_ctx",
    )(*args)


def _attn_body(qT_ref, kc_ref, k_ref, vTc_ref, vT_ref, lq1_ref, lk1_ref, lq2_ref, lk2_ref, sw_ref, o_ref,
               s0_ref, s1_ref, m0_ref, m1_ref, acc_ref, l_ref, *, tq):
    g = pl.program_id(0)
    tk = TOK_TILE
    n_ctx = kc_ref.shape[0]
    n_all = n_ctx + k_ref.shape[0]
    assert n_ctx == tk

    @pl.when(g == 0)
    def _():
        s1_ref[...] = jnp.zeros(s1_ref.shape, F32)
        m1_ref[...] = jnp.zeros(m1_ref.shape, F32)
        acc_ref[...] = jnp.zeros(acc_ref.shape, F32)
        l_ref[...] = jnp.ones(l_ref.shape, F32)

    def step(s_w, m_w, s_r, m_r):
        oT = acc_ref[...] / jnp.sum(l_ref[...], axis=0, keepdims=True)
        lam = (jnp.exp(jnp.sum(lq1_ref[...] * lk1_ref[...], axis=-1, keepdims=True))
               - jnp.exp(jnp.sum(lq2_ref[...] * lk2_ref[...], axis=-1, keepdims=True)) + LAM_INIT)
        od = (oT[:, :tq] - lam * oT[:, tq:]).T
        ms = jnp.mean(od * od, axis=-1, keepdims=True)
        o_ref[...] = ((od * lax.rsqrt(ms + EPS)) * sw_ref[...] * (1.0 - LAM_INIT)).astype(BF16)

        qT = jnp.concatenate([qT_ref[0], qT_ref[1]], axis=1)
        m_prev = m_r[...]
        n_parts = 2 * tq // ATTN_COLS
        m8, l8, acc = [None] * n_parts, [None] * n_parts, [None] * n_parts
        for lo in range(0, n_all, tk):
            k_tile = kc_ref[...] if lo == 0 else k_ref[lo - n_ctx:lo - n_ctx + tk, :]
            v_tile = vTc_ref[...] if lo == 0 else vT_ref[:, lo - n_ctx:lo - n_ctx + tk]
            for h in range(n_parts):
                cols = slice(h * ATTN_COLS, (h + 1) * ATTN_COLS)
                s_new = _dot(k_tile, qT[:, cols])
                s_w[lo:lo + tk, cols] = s_new
                part = jnp.max(s_new.reshape(tk // 8, 8, ATTN_COLS), axis=0)
                m8[h] = part if m8[h] is None else jnp.maximum(m8[h], part)

                p = jnp.exp2(s_r[lo:lo + tk, cols] - m_prev[:, cols])
                part = jnp.sum(p.reshape(tk // 8, 8, ATTN_COLS), axis=0)
                l8[h] = part if l8[h] is None else l8[h] + part
                pv = _dot(v_tile, p.astype(BF16))
                acc[h] = pv if acc[h] is None else acc[h] + pv
        m_w[...] = jnp.max(jnp.concatenate(m8, axis=1), axis=0, keepdims=True)
        acc_ref[...] = jnp.concatenate(acc, axis=1)
        l_ref[...] = jnp.concatenate(l8, axis=1)

    @pl.when(g % 2 == 0)
    def _():
        step(s0_ref, m0_ref, s1_ref, m1_ref)

    @pl.when(g % 2 == 1)
    def _():
        step(s1_ref, m1_ref, s0_ref, m0_ref)


def _attn_call(qT, kc, k, vTc, vT, lq1, lk1, lq2, lk2, sw, tq):
    B, H, _, _, L = qT.shape
    n_ctx = kc.shape[2]
    n_all = n_ctx + L
    nq = L // tq
    n_tiles = B * H * nq

    def tile(t):
        return t // (H * nq), (t // nq) % H, t % nq

    def scored(g):
        return tile(jnp.minimum(g, n_tiles - 1))

    def finished(g):
        return tile(jnp.clip(g - 1, 0, n_tiles - 1))

    def written(g):
        return tile(jnp.maximum(g - 2, 0))

    def q_map(g):
        b, h, i = scored(g)
        return b, h, 0, 0, i

    def k_map(g):
        b, h, _ = scored(g)
        return b, h, 0, 0

    def v_map(g):
        b, h, _ = finished(g)
        return b, h, 0, 0

    def o_map(g):
        b, h, i = written(g)
        return b, i, h

    vec = pl.BlockSpec((1, HEAD_DIM), lambda g: (0, 0))
    return pl.pallas_call(
        functools.partial(_attn_body, tq=tq),
        grid=(n_tiles + 2,),
        in_specs=[
            pl.BlockSpec((None, None, 2, LANES, tq), q_map),
            pl.BlockSpec((None, None, n_ctx, LANES), k_map),
            pl.BlockSpec((None, None, L, LANES), k_map),
            pl.BlockSpec((None, None, V_DIM, n_ctx), v_map),
            pl.BlockSpec((None, None, V_DIM, L), v_map),
            vec, vec, vec, vec,
            pl.BlockSpec((1, V_DIM), lambda g: (0, 0)),
        ],
        out_specs=pl.BlockSpec((None, tq, V_DIM), o_map),
        out_shape=jax.ShapeDtypeStruct((B, L, ATTN_W), BF16),
        scratch_shapes=[pltpu.VMEM((n_all, 2 * tq), F32), pltpu.VMEM((n_all, 2 * tq), F32),
                        pltpu.VMEM((1, 2 * tq), F32), pltpu.VMEM((1, 2 * tq), F32),
                        pltpu.VMEM((V_DIM, 2 * tq), F32), pltpu.VMEM((8, 2 * tq), F32)],
        compiler_params=_params(0, 1),
        name="attn",
    )(qT, kc, k, vTc, vT, lq1, lk1, lq2, lk2, sw)


def _cmul(xr, xi, yr, yi):
    return xr * yr - xi * yi, xr * yi + xi * yr


def _a_bar(are, aim, ldt):
    lr = jnp.minimum(are, MIN_NEG_RE)
    dt = jnp.exp(ldt)
    mag = jnp.exp(lr * dt)
    return lr, aim, mag * jnp.cos(aim * dt), mag * jnp.sin(aim * dt)


def _powers(ar, ai, n):
    pows = [(jnp.ones_like(ar), jnp.zeros_like(ar))]
    for _ in range(n):
        pows.append(_cmul(pows[-1][0], pows[-1][1], ar, ai))
    return pows


def _s5tab_body(aar_ref, aai_ref, adt_ref, abr_ref, abi_ref, bar_ref, bai_ref, bdt_ref, bcr_ref, bci_ref,
                car_ref, cai_ref, cdt_ref, dsk_ref, m_ref, bz_ref, cc_ref, a16_ref, lag_ref):
    d = pl.program_id(1)
    fwd = d == 0
    half = SLAB_STATE // 2
    sel = lambda a, b: (jnp.where(fwd, a[0], b[0]), jnp.where(fwd, a[1], b[1]))

    lr, li, ar, ai = _a_bar(aar_ref[...], aai_ref[...], adt_ref[...])
    den = lr * lr + li * li
    nr = ar - 1.0
    cfr = (nr * lr + ai * li) / den
    cfi = (ai * lr - nr * li) / den
    bbr, bbi = _cmul(cfr, cfi, abr_ref[...], abi_ref[...])
    pa = _powers(ar, ai, CHUNK - 1)
    row_a = lax.broadcasted_iota(jnp.int32, (SLAB, half), 0) // S5_GROUP
    col_a = lax.broadcasted_iota(jnp.int32, (SLAB, half), 1) // S5_STATE
    diag_a = row_a == col_a

    def expand_a(v):
        return jnp.where(diag_a, jnp.concatenate([v] * (half // LANES), axis=1), 0.0)

    for t in range(CHUNK):
        pw = sel(pa[CHUNK - 1 - t], pa[t])
        zr, zi = _cmul(pw[0], pw[1], bbr, bbi)
        bz_ref[t * SLAB:(t + 1) * SLAB, :] = jnp.concatenate([expand_a(zr), expand_a(zi)], axis=1).astype(BF16)

    _, _, br_, bi_ = _a_bar(bar_ref[...], bai_ref[...], bdt_ref[...])
    pb = _powers(br_, bi_, CHUNK)
    cr = bcr_ref[...]
    ci = bci_ref[...]
    row_b = lax.broadcasted_iota(jnp.int32, (half, SLAB), 0) // S5_STATE
    col_b = lax.broadcasted_iota(jnp.int32, (half, SLAB), 1) // S5_GROUP
    diag_b = row_b == col_b

    def expand_b(v):
        return jnp.where(diag_b, jnp.concatenate([v] * GROUPS_PER_SLAB, axis=0), 0.0)

    for t in range(CHUNK):
        pw = sel(pb[t + 1], pb[CHUNK - t])
        ctr, cti = _cmul(cr, ci, pw[0], pw[1])
        cc_ref[:, t * SLAB:(t + 1) * SLAB] = jnp.concatenate([expand_b(ctr), expand_b(-cti)], axis=0).astype(BF16)

    lane = lax.broadcasted_iota(jnp.int32, (SLAB, LANES), 1)
    lhs = jnp.where(lane < S5_STATE, bbr, bbi)
    r_k = lax.broadcasted_iota(jnp.int32, (SLAB, SLAB), 0)
    c_k = lax.broadcasted_iota(jnp.int32, (SLAB, SLAB), 1)
    diag_k = (r_k // S5_GROUP) == (c_k // S5_GROUP)
    sign = 1 - 2 * d
    for tau in range(CHUNK):
        ctr, cti = _cmul(cr, ci, pb[tau][0], pb[tau][1])
        rhs = jnp.concatenate([ctr, -cti], axis=0)
        ktau = jnp.where(diag_k, jnp.dot(lhs, rhs, preferred_element_type=F32,
                                         precision=lax.Precision.HIGHEST), 0.0)
        if tau == 0:
            @pl.when(fwd)
            def _():
                lag_ref[CHUNK - 1] = ktau + jnp.where(r_k == c_k, dsk_ref[...], 0.0)

            @pl.when(jnp.logical_not(fwd))
            def _():
                lag_ref[CHUNK - 1] = lag_ref[CHUNK - 1] + ktau
        else:
            lag_ref[CHUNK - 1 + sign * tau] = ktau

    @pl.when(jnp.logical_not(fwd))
    def _():
        lags = [lag_ref[n].astype(BF16) for n in range(2 * CHUNK - 1)]
        for tp in range(CHUNK):
            m_ref[tp * SLAB:(tp + 1) * SLAB, :] = jnp.concatenate(
                [lags[t - tp + CHUNK - 1] for t in range(CHUNK)], axis=1)

    _, _, xr, xi = _a_bar(car_ref[...], cai_ref[...], cdt_ref[...])
    for _ in range(4):
        xr, xi = _cmul(xr, xi, xr, xi)
    a16_ref[...] = jnp.concatenate([xr, xi], axis=1)


def _s5tab_call(s5_a_re, s5_a_im, s5_log_dt, s5_b_re, s5_b_im, s5_c_re, s5_c_im, s5_d):
    S, GS, C, P = N_SLABS, GROUPS_PER_SLAB, S5_GROUP, S5_STATE
    half = SLAB_STATE // 2

    def lay_a(a):
        return jnp.broadcast_to(a.reshape(2, S, GS, 1, 1, P), (2, S, GS, C, 2, P)).reshape(2, S, SLAB, LANES)

    def lay_b(a):
        a = a.reshape(2, S, GS, P).transpose(0, 1, 3, 2)
        return jnp.broadcast_to(a[..., None], (2, S, P, GS, C)).reshape(2, S, P, SLAB)

    ldt = jnp.broadcast_to(s5_log_dt[..., None], (2, S5_GROUPS, P))
    b_a = lambda b: jnp.broadcast_to(b.transpose(0, 1, 3, 2).reshape(2, S, GS, C, 1, P),
                                     (2, S, GS, C, 2, P)).reshape(2, S, SLAB, LANES)
    c_b = lambda c: c.reshape(2, S, GS, C, P).transpose(0, 1, 4, 2, 3).reshape(2, S, P, SLAB)
    lay_c = lambda a: a.reshape(2, S, 1, half)
    blk = lambda r, c: pl.BlockSpec((None, None, r, c), lambda s, d: (d, s, 0, 0))
    return pl.pallas_call(
        _s5tab_body,
        grid=(S, 2),
        in_specs=[blk(SLAB, LANES)] * 5 + [blk(P, SLAB)] * 5 + [blk(1, half)] * 3
                 + [pl.BlockSpec((None, 1, SLAB), lambda s, d: (s, 0, 0))],
        out_specs=[pl.BlockSpec((None, SLAB_IN, SLAB_IN), lambda s, d: (s, 0, 0)),
                   blk(SLAB_IN, SLAB_STATE), blk(SLAB_STATE, SLAB_IN), blk(1, SLAB_STATE)],
        out_shape=[jax.ShapeDtypeStruct((S, SLAB_IN, SLAB_IN), BF16),
                   jax.ShapeDtypeStruct((2, S, SLAB_IN, SLAB_STATE), BF16),
                   jax.ShapeDtypeStruct((2, S, SLAB_STATE, SLAB_IN), BF16),
                   jax.ShapeDtypeStruct((2, S, 1, SLAB_STATE), F32)],
        scratch_shapes=[pltpu.VMEM((2 * CHUNK - 1, SLAB, SLAB), F32)],
        compiler_params=_params(0, 2),
        name="s5_tab",
    )(lay_a(s5_a_re), lay_a(s5_a_im), lay_a(ldt), b_a(s5_b_re), b_a(s5_b_im),
      lay_b(s5_a_re), lay_b(s5_a_im), lay_b(ldt), c_b(s5_c_re), c_b(s5_c_im),
      lay_c(s5_a_re), lay_c(s5_a_im), lay_c(ldt), s5_d.reshape(S, 1, SLAB))


def _s5scan_body(uc_ref, ul_ref, bz_ref, a_ref, sin_ref, zc_ref, zl_ref, st_ref, *, n_cc, n_lc):
    d = pl.program_id(1)
    B = uc_ref.shape[0]
    n_lt = SLAB_STATE // LANES
    w = bz_ref[...]

    def chunk_inputs(u_ref, n, z_ref):
        z = _dot(u_ref[...].reshape(B * n, SLAB_IN), w)
        for b in range(B):
            for t in range(n_lt):
                z_ref[t, pl.ds(b, n, stride=B), :] = z[b * n:(b + 1) * n, t * LANES:(t + 1) * LANES]

    chunk_inputs(uc_ref, n_cc, zc_ref)
    chunk_inputs(ul_ref, n_lc, zl_ref)
    a = a_ref[...]
    half = n_lt // 2
    a_re = [a[:, (t % half) * LANES:(t % half + 1) * LANES] for t in range(n_lt)]
    a_im = [a[:, (t % half + half) * LANES:(t % half + half + 1) * LANES] for t in range(n_lt)]

    def advance(s, z_ref, rows):
        out = []
        for t in range(n_lt):
            other = s[(t + half) % n_lt]
            cross = -(a_im[t] * other) if t < half else a_im[t] * other
            out.append(a_re[t] * s[t] + cross + z_ref[t, rows, :])
        return tuple(out)

    def ctx_step(i, s):
        kk = jnp.where(d == 0, i, n_cc - 1 - i)
        return advance(s, zc_ref, pl.ds(pl.multiple_of(kk * B, B), B))

    def lat_step(i, s):
        kk = jnp.where(d == 0, i, n_lc - 1 - i)
        rows = pl.ds(pl.multiple_of(kk * B, B), B)
        for t in range(n_lt):
            st_ref[t, rows, :] = s[t]
        return advance(s, zl_ref, rows)

    s = lax.fori_loop(0, n_cc, ctx_step, tuple(jnp.zeros((B, LANES), F32) for _ in range(n_lt)))
    lax.fori_loop(0, n_lc, lat_step, s)
    for b in range(B):
        for t in range(n_lt):
            sin_ref[b, :, t * LANES:(t + 1) * LANES] = st_ref[t, pl.ds(b, n_lc, stride=B), :].astype(BF16)


def _s5scan_call(uc, ul, bzx, a16x):
    S, B, n_cc, _ = uc.shape
    n_lc = ul.shape[2]
    n_lt = SLAB_STATE // LANES
    return pl.pallas_call(
        functools.partial(_s5scan_body, n_cc=n_cc, n_lc=n_lc),
        grid=(S, 2),
        in_specs=[pl.BlockSpec((None, B, n_cc, SLAB_IN), lambda s, d: (s, 0, 0, 0)),
                  pl.BlockSpec((None, B, n_lc, SLAB_IN), lambda s, d: (s, 0, 0, 0)),
                  pl.BlockSpec((None, None, SLAB_IN, SLAB_STATE), lambda s, d: (d, s, 0, 0)),
                  pl.BlockSpec((None, None, 1, SLAB_STATE), lambda s, d: (d, s, 0, 0))],
        out_specs=pl.BlockSpec((None, None, B, n_lc, SLAB_STATE), lambda s, d: (d, s, 0, 0, 0)),
        out_shape=jax.ShapeDtypeStruct((2, S, B, n_lc, SLAB_STATE), BF16),
        scratch_shapes=[pltpu.VMEM((n_lt, B * n_cc, LANES), F32),
                        pltpu.VMEM((n_lt, B * n_lc, LANES), F32),
                        pltpu.VMEM((n_lt, B * n_lc, LANES), F32)],
        compiler_params=_params(2),
        name="s5_scan",
    )(uc, ul, bzx, a16x)


def _s5out_body(ul_ref, m_ref, cc_ref, sin_ref, y_ref):
    B, n_lc, _ = ul_ref.shape
    rows = B * n_lc
    y = _dot(ul_ref[...].reshape(rows, SLAB_IN), m_ref[...])
    y = y + _dot(sin_ref[0].reshape(rows, SLAB_STATE), cc_ref[0])
    y = y + _dot(sin_ref[1].reshape(rows, SLAB_STATE), cc_ref[1])
    y_ref[...] = y.reshape(B, n_lc, y.shape[-1])


def _s5out_call(ul, m, ccx, sin, nb):
    S, B, n_lc, _ = ul.shape
    return pl.pallas_call(
        _s5out_body,
        grid=(S, SLAB_IN // nb),
        in_specs=[pl.BlockSpec((None, B, n_lc, SLAB_IN), lambda s, j: (s, 0, 0, 0)),
                  pl.BlockSpec((None, SLAB_IN, nb), lambda s, j: (s, 0, j)),
                  pl.BlockSpec((2, None, SLAB_STATE, nb), lambda s, j: (0, s, 0, j)),
                  pl.BlockSpec((2, None, B, n_lc, SLAB_STATE), lambda s, j: (0, s, 0, 0, 0))],
        out_specs=pl.BlockSpec((None, B, n_lc, nb), lambda s, j: (s, 0, 0, j)),
        out_shape=jax.ShapeDtypeStruct((S, B, n_lc, SLAB_IN), F32),
        compiler_params=_params(2),
        name="s5_out",
    )(ul, m, ccx, sin)


def _merge_body(y_ref, o_ref, g_ref, x_ref, mod_ref, gw_ref, gb_ref, wbs_ref, wba_ref, wo_ref, out_ref, y_scr):
    ya = _dot(o_ref[...], wba_ref[...])
    n_rows = y_ref.shape[1]
    for s in range(N_SLABS):
        for t in range(CHUNK):
            y_scr[s, pl.ds(t, n_rows, stride=CHUNK), :] = y_ref[s, :, t * SLAB:(t + 1) * SLAB]
    y = jnp.concatenate([y_scr[s] for s in range(N_SLABS)], axis=1)
    hh = 0.5 * y * (1.0 + jnp.tanh(math.sqrt(2.0 / math.pi) * (y + 0.044715 * (y * y * y))))
    gl = _dot(hh.astype(BF16), gw_ref[...]) + gb_ref[...]
    ys = _dot((hh * _sigmoid(gl)).astype(BF16), wbs_ref[...])
    g = g_ref[...].astype(F32)
    mix = g[:, :D_MODEL] * ys + g[:, D_MODEL:] * ya
    out_ref[...] = x_ref[...] + mod_ref[2:3] * _dot(mix.astype(BF16), wo_ref[...])


def _merge_call(y, o, g, x, mod3, gw, gb, wbs, wba, wo, tm):
    B, L, D = x.shape
    c2 = lambda b, i: (0, 0)
    return pl.pallas_call(
        _merge_body,
        grid=(B, L // tm),
        in_specs=[pl.BlockSpec((N_SLABS, None, tm // CHUNK, SLAB_IN), lambda b, i: (0, b, i, 0)),
                  pl.BlockSpec((None, tm, ATTN_W), lambda b, i: (b, i, 0)),
                  pl.BlockSpec((None, tm, 2 * D), lambda b, i: (b, i, 0)),
                  pl.BlockSpec((None, tm, D), lambda b, i: (b, i, 0)),
                  pl.BlockSpec((None, 6, D), lambda b, i: (b, 0, 0)),
                  pl.BlockSpec((S5_W, S5_W), c2), pl.BlockSpec((1, S5_W), c2),
                  pl.BlockSpec((S5_W, D), c2), pl.BlockSpec((ATTN_W, D), c2), pl.BlockSpec((D, D), c2)],
        out_specs=pl.BlockSpec((None, tm, D), lambda b, i: (b, i, 0)),
        out_shape=jax.ShapeDtypeStruct((B, L, D), F32),
        scratch_shapes=[pltpu.VMEM((N_SLABS, tm, SLAB), F32)],
        compiler_params=_params(2),
        name="merge",
    )(y, o, g, x, mod3, gw, gb, wbs, wba, wo)


def _ffn_body(xp_ref, x_ref, xn_ref, mod_ref, n2w_ref, wup_ref, cw_ref, cb_ref, wdn_ref, out_ref, *, tm):
    i = pl.program_id(1)
    last = pl.num_programs(1) - 1
    mod = mod_ref[...]
    n2w = n2w_ref[...]

    def modulated(v):
        ms = jnp.mean(v * v, axis=-1, keepdims=True)
        return (v * lax.rsqrt(ms + EPS)) * n2w * (1.0 + mod[4:5]) + mod[3:4]

    x = x_ref[...]
    hp = modulated(xp_ref[...]) * (i > 0).astype(F32)
    hn = modulated(xn_ref[...]) * (i < last).astype(F32)
    hcat = jnp.concatenate([hp, modulated(x), hn], axis=0).astype(BF16)
    rows = tm + 2 * HALO

    def conv(u, cw, cb):
        prev = pltpu.roll(u, 1, 0)[HALO:HALO + tm]
        nxt = pltpu.roll(u, rows - 1, 0)[HALO:HALO + tm]
        return cb + prev * cw[0:1] + u[HALO:HALO + tm] * cw[1:2] + nxt * cw[2:3]

    def col(ref, part, f):
        lo = part * D_FF + f * FF_CHUNK
        return ref[:, lo:lo + FF_CHUNK]

    ups, acts, acc = {}, {}, None
    for f in range(N_FF_CHUNKS + 2):
        if f < N_FF_CHUNKS:
            ups[f] = (_dot(hcat, col(wup_ref, 0, f)), _dot(hcat, col(wup_ref, 1, f)))
        e = f - 1
        if 0 <= e < N_FF_CHUNKS:
            ua, ug = ups.pop(e)
            ya = conv(ua, col(cw_ref, 0, e), col(cb_ref, 0, e))
            half_g = 0.5 * conv(ug, col(cw_ref, 1, e), col(cb_ref, 1, e))
            acts[e] = ((half_g + half_g * jnp.tanh(half_g)) * ya).astype(BF16)
        e = f - 2
        if 0 <= e < N_FF_CHUNKS and ((e + 1) % FF_DOWN_GROUP == 0 or e == N_FF_CHUNKS - 1):
            lo = e // FF_DOWN_GROUP * FF_DOWN_GROUP
            group = jnp.concatenate([acts.pop(j) for j in range(lo, e + 1)], axis=1)
            down = _dot(group, wdn_ref[lo * FF_CHUNK:(e + 1) * FF_CHUNK, :])
            acc = down if acc is None else acc + down
    out_ref[...] = x + mod[5:6] * acc


def _ffn_call(x1, mod3, n2w, wup, cw, cb, wdn, tm):
    B, L, D = x1.shape
    nh = tm // HALO
    n_halo_blocks = L // HALO
    c2 = lambda b, i: (0, 0)
    return pl.pallas_call(
        functools.partial(_ffn_body, tm=tm),
        grid=(B, L // tm),
        in_specs=[pl.BlockSpec((None, HALO, D), lambda b, i: (b, jnp.maximum(i * nh - 1, 0), 0)),
                  pl.BlockSpec((None, tm, D), lambda b, i: (b, i, 0)),
                  pl.BlockSpec((None, HALO, D), lambda b, i: (b, jnp.minimum((i + 1) * nh, n_halo_blocks - 1), 0)),
                  pl.BlockSpec((None, 6, D), lambda b, i: (b, 0, 0)),
                  pl.BlockSpec((1, D), c2),
                  pl.BlockSpec((D, 2 * D_FF), c2),
                  pl.BlockSpec((3, 2 * D_FF), c2),
                  pl.BlockSpec((1, 2 * D_FF), c2),
                  pl.BlockSpec((D_FF, D), c2)],
        out_specs=pl.BlockSpec((None, tm, D), lambda b, i: (b, i, 0)),
        out_shape=jax.ShapeDtypeStruct((B, L, D), F32),
        compiler_params=_params(2),
        name="ffn",
    )(x1, x1, x1, mod3, n2w, wup, cw, cb, wdn)


def _rope_tables(L):
    half = HEAD_DIM // 2
    inv_freq = ROPE_THETA ** (-jnp.arange(0, half, 2, dtype=F32) / half)
    lane = jnp.arange(PAIR)
    freq = inv_freq[lane % (half // 2)]
    sign = jnp.where(lane % half < half // 2, -1.0, 1.0)
    row_lane = (lane % HEAD_DIM < half)[None, None, :]
    ang_r = jnp.arange(L // GRID_W, dtype=F32)[:, None] * freq[None, :]
    ang_c = jnp.arange(GRID_W, dtype=F32)[:, None] * freq[None, :]
    table = lambda f: jnp.where(row_lane, f(ang_r)[:, None, :], f(ang_c)[None, :, :]).reshape(L, PAIR)
    return table(jnp.cos), table(lambda a: jnp.sin(a) * sign)


def kernel(x, c, ctx, c_ctx, ada_w, ada_b, norm1_w, w_in, b_gate, q_norm_w, k_norm_w, lam_q1, lam_k1, lam_q2, lam_k2, subln_w, s5_a_re, s5_a_im, s5_log_dt, s5_b_re, s5_b_im, s5_c_re, s5_c_im, s5_d, glu_w, glu_b, w_branch_s5, w_branch_attn, w_out, norm2_w, w_up, conv_w, conv_b, w_down):
    B, L, D = x.shape
    assert ada_w.shape[0] == 1 and D == D_MODEL and B + 1 <= 8

    cc = jnp.zeros((8, D), F32).at[:B].set(c).at[B].set(c_ctx)
    mod3 = _mod_call(cc, ada_w[0], ada_b).reshape(8, 6, D)

    cos_t, sin_t = _rope_tables(L)
    lane = jnp.arange(PAIR)
    seg = jnp.where((lane[:, None] // HEAD_DIM) == (lane[None, :] // HEAD_DIM), 1.0 / HEAD_DIM, 0.0).astype(BF16)
    qw = jnp.tile(q_norm_w[0], PAIR // HEAD_DIM)[None]
    kw = jnp.tile(k_norm_w[0], PAIR // HEAD_DIM)[None]
    w_in_b = w_in[0].astype(BF16)
    kc, vTc, uc = _inproj_call(ctx, mod3, lambda b: B, norm1_w, w_in_b, kw, seg)
    k, vT, ul, qT, g = _inproj_call(x, mod3, lambda b: b, norm1_w, w_in_b, kw, seg,
                                    latent_args=(qw, b_gate, cos_t, sin_t, IN_TILE))

    o = _attn_call(qT, kc, k, vTc, vT, lam_q1, lam_k1, lam_q2, lam_k2, subln_w, tq=512)

    m, bzx, ccx, a16x = _s5tab_call(s5_a_re[0], s5_a_im[0], s5_log_dt[0], s5_b_re[0], s5_b_im[0],
                                    s5_c_re[0], s5_c_im[0], s5_d[0])
    sin = _s5scan_call(uc, ul, bzx, a16x)
    y = _s5out_call(ul, m, ccx, sin, nb=512)

    x1 = _merge_call(y, o, g, x, mod3, glu_w[0].astype(BF16), glu_b, w_branch_s5[0].astype(BF16),
                     w_branch_attn[0].astype(BF16), w_out[0].astype(BF16), tm=512)

    return _ffn_call(x1, mod3, norm2_w, w_up[0].astype(BF16), conv_w[0], conv_b, w_down[0].astype(BF16), tm=512)
```

```python
import functools
import math

import jax
import jax.numpy as jnp
from jax import lax
from jax.experimental import pallas as pl
from jax.experimental.pallas import tpu as pltpu

D_MODEL = 1024
GRID_W = 64
N_HEADS = 8
HEAD_DIM = 64
V_DIM = 2 * HEAD_DIM
QK_W = N_HEADS * 2 * HEAD_DIM
ATTN_W = N_HEADS * V_DIM
S5_W = 512
S5_GROUP = 16
S5_GROUPS = S5_W // S5_GROUP
S5_STATE = 64
D_FF = 2816
ROPE_THETA = 10000.0
EPS = 1e-6
MIN_NEG_RE = -1e-4
K_OFF = QK_W
V_OFF = 2 * QK_W
U_OFF = 2 * QK_W + ATTN_W
G_OFF = U_OFF + S5_W
N_IN = G_OFF + 2 * D_MODEL
LAM_INIT = 0.8 - 0.6 * math.exp(-0.3 * 0)
Q_SCALE = HEAD_DIM ** -0.5 * math.log2(math.e)

LANES = 128
PAIR = 2 * LANES
TOK_TILE = 256
ATTN_COLS = 256
IN_TILE = 512
CHUNK = 16
SLAB = LANES
N_SLABS = S5_W // SLAB
GROUPS_PER_SLAB = SLAB // S5_GROUP
SLAB_IN = CHUNK * SLAB
SLAB_STATE = GROUPS_PER_SLAB * 2 * S5_STATE
FF_CHUNK = 256
N_FF_CHUNKS = D_FF // FF_CHUNK
FF_DOWN_GROUP = 4
HALO = 8
VMEM_LIMIT = 56 * 1024 * 1024

F32 = jnp.float32
BF16 = jnp.bfloat16


def _sigmoid(x):
    return 1.0 / (1.0 + jnp.exp(-x))


def _dot(a, b):
    return jnp.dot(a, b, preferred_element_type=F32)


def _dot_cols(a, b_ref, rows=slice(None)):
    n = b_ref.shape[-1]
    return jnp.concatenate([_dot(a, b_ref[rows, c:c + PAIR]) for c in range(0, n, PAIR)], axis=1)


def _params(n_parallel, n_arbitrary=0):
    return pltpu.CompilerParams(
        dimension_semantics=("parallel",) * n_parallel + ("arbitrary",) * n_arbitrary,
        vmem_limit_bytes=VMEM_LIMIT)


def _mod_body(c_ref, w_ref, b_ref, o_ref):
    c = c_ref[...]
    s = c * _sigmoid(c)
    o_ref[...] = jnp.dot(s, w_ref[...], preferred_element_type=F32,
                         precision=lax.Precision.HIGHEST) + b_ref[...]


def _mod_call(cc, ada_w, ada_b):
    n = ada_w.shape[1]
    return pl.pallas_call(
        _mod_body,
        grid=(n // D_MODEL,),
        in_specs=[pl.BlockSpec((8, D_MODEL), lambda j: (0, 0)),
                  pl.BlockSpec((D_MODEL, D_MODEL), lambda j: (0, j)),
                  pl.BlockSpec((1, D_MODEL), lambda j: (0, j))],
        out_specs=pl.BlockSpec((8, D_MODEL), lambda j: (0, j)),
        out_shape=jax.ShapeDtypeStruct((8, n), F32),
        compiler_params=_params(1),
        name="mod",
    )(cc, ada_w, ada_b)


def _inproj_body(*refs, latent):
    if latent:
        (x_ref, mod_ref, n1w_ref, w_ref, kw_ref, seg_ref, qw_ref, bg_ref, cos_ref, sin_ref,
         k_ref, vT_ref, u_ref, qT_ref, g_ref, u_scr) = refs
    else:
        x_ref, mod_ref, n1w_ref, w_ref, kw_ref, seg_ref, k_ref, vT_ref, u_ref, u_scr = refs
    T = x_ref.shape[0]
    xt = x_ref[...]
    ms = jnp.mean(xt * xt, axis=-1, keepdims=True)
    mod = mod_ref[...]
    h = (xt * lax.rsqrt(ms + EPS)) * n1w_ref[...] * (1.0 + mod[1:2]) + mod[0:1]
    hb = h.astype(BF16)
    seg = seg_ref[...]
    if latent:
        cos = cos_ref[...]
        sin = sin_ref[...]
        first_half = (lax.broadcasted_iota(jnp.int32, cos.shape, 1) % 32) < 16

    def norm_rope(z, w):
        y = z * lax.rsqrt(_dot((z * z).astype(BF16), seg) + EPS) * w
        if not latent:
            return y
        partner = jnp.where(first_half, pltpu.roll(y, PAIR - 16, 1), pltpu.roll(y, 16, 1))
        return y * cos + partner * sin

    def put_k(c, z):
        y = norm_rope(z, kw_ref[...]).astype(BF16)
        k_ref[2 * c] = y[:, :LANES]
        k_ref[2 * c + 1] = y[:, LANES:]

    def put_v(c, z):
        vT_ref[2 * c] = z[:, :LANES].T.astype(BF16)
        vT_ref[2 * c + 1] = z[:, LANES:].T.astype(BF16)

    def put_u(c, z):
        u_scr[2 * c] = z[:, :SLAB]
        u_scr[2 * c + 1] = z[:, SLAB:]
        if c == S5_W // PAIR - 1:
            for s in range(N_SLABS):
                for t in range(CHUNK):
                    piece = u_scr[s, pl.ds(t, T // CHUNK, stride=CHUNK), :]
                    u_ref[s, :, t * SLAB:(t + 1) * SLAB] = piece.astype(BF16)

    def put_q(c, z):
        y = norm_rope(z, qw_ref[...]) * Q_SCALE
        row = lax.broadcasted_iota(jnp.int32, (LANES, T), 0)
        for hh, qT in ((2 * c, y[:, :LANES].T), (2 * c + 1, y[:, LANES:].T)):
            qT_ref[hh, 0] = jnp.where(row < HEAD_DIM, qT, 0.0).astype(BF16)
            qT_ref[hh, 1] = jnp.where(row >= HEAD_DIM, qT, 0.0).astype(BF16)

    def put_g(c, z):
        lo = c * PAIR
        g_ref[:, lo:lo + PAIR] = _sigmoid(z + bg_ref[:, lo:lo + PAIR]).astype(BF16)

    sections = [(K_OFF, QK_W, put_k), (V_OFF, ATTN_W, put_v), (U_OFF, S5_W, put_u)]
    if latent:
        sections += [(0, QK_W, put_q), (G_OFF, 2 * D_MODEL, put_g)]
    tasks = [(off + c * PAIR, c, put) for off, width, put in sections for c in range(width // PAIR)]
    project = lambda lo: _dot(hb, w_ref[:, lo:lo + PAIR])
    z_next = project(tasks[0][0])
    for n, (_, c, put) in enumerate(tasks):
        z = z_next
        if n + 1 < len(tasks):
            z_next = project(tasks[n + 1][0])
        put(c, z)


def _inproj_call(x, mod3, mod_row, n1w, w_in_b, kw, seg, latent_args=None):
    B, n, D = x.shape
    latent = latent_args is not None
    T = latent_args[4] if latent else n
    assert n % T == 0
    const2 = lambda b, i: (0, 0)
    in_specs = [
        pl.BlockSpec((None, T, D), lambda b, i: (b, i, 0)),
        pl.BlockSpec((None, 6, D), lambda b, i: (mod_row(b), 0, 0)),
        pl.BlockSpec((1, D), const2),
        pl.BlockSpec((D, N_IN), const2),
        pl.BlockSpec((1, PAIR), const2),
        pl.BlockSpec((PAIR, PAIR), const2),
    ]
    out_specs = [
        pl.BlockSpec((None, N_HEADS, T, LANES), lambda b, i: (b, 0, i, 0)),
        pl.BlockSpec((None, N_HEADS, V_DIM, T), lambda b, i: (b, 0, 0, i)),
        pl.BlockSpec((N_SLABS, None, T // CHUNK, SLAB_IN), lambda b, i: (0, b, i, 0)),
    ]
    out_shape = [
        jax.ShapeDtypeStruct((B, N_HEADS, n, LANES), BF16),
        jax.ShapeDtypeStruct((B, N_HEADS, V_DIM, n), BF16),
        jax.ShapeDtypeStruct((N_SLABS, B, n // CHUNK, SLAB_IN), BF16),
    ]
    args = [x, mod3, n1w, w_in_b, kw, seg]
    if latent:
        qw, bg, cos_t, sin_t, _ = latent_args
        in_specs += [pl.BlockSpec((1, PAIR), const2), pl.BlockSpec((1, 2 * D), const2),
                     pl.BlockSpec((T, PAIR), lambda b, i: (i, 0)), pl.BlockSpec((T, PAIR), lambda b, i: (i, 0))]
        out_specs += [pl.BlockSpec((None, N_HEADS, 2, LANES, T), lambda b, i: (b, 0, 0, 0, i)),
                      pl.BlockSpec((None, T, 2 * D), lambda b, i: (b, i, 0))]
        out_shape += [jax.ShapeDtypeStruct((B, N_HEADS, 2, LANES, n), BF16),
                      jax.ShapeDtypeStruct((B, n, 2 * D), BF16)]
        args += [qw, bg, cos_t, sin_t]
    return pl.pallas_call(
        functools.partial(_inproj_body, latent=latent),
        grid=(B, n // T),
        in_specs=in_specs,
        out_specs=out_specs,
        out_shape=out_shape,
        scratch_shapes=[pltpu.VMEM((N_SLABS, T, SLAB), F32)],
        compiler_params=_params(2),
        name="in_proj_latent" if latent else "in_proj_ctx",
    )(*args)


def _attn_body(qT_ref, kc_ref, k_ref, vTc_ref, vT_ref, lq1_ref, lk1_ref, lq2_ref, lk2_ref, sw_ref, o_ref,
               s0_ref, s1_ref, m0_ref, m1_ref, acc_ref, l_ref, *, tq):
    g = pl.program_id(0)
    tk = TOK_TILE
    n_ctx = kc_ref.shape[0]
    n_all = n_ctx + k_ref.shape[0]
    assert n_ctx == tk

    @pl.when(g == 0)
    def _():
        s1_ref[...] = jnp.zeros(s1_ref.shape, F32)
        m1_ref[...] = jnp.zeros(m1_ref.shape, F32)
        acc_ref[...] = jnp.zeros(acc_ref.shape, F32)
        l_ref[...] = jnp.ones(l_ref.shape, F32)

    def step(s_w, m_w, s_r, m_r):
        oT = acc_ref[...] / jnp.sum(l_ref[...], axis=0, keepdims=True)
        lam = (jnp.exp(jnp.sum(lq1_ref[...] * lk1_ref[...], axis=-1, keepdims=True))
               - jnp.exp(jnp.sum(lq2_ref[...] * lk2_ref[...], axis=-1, keepdims=True)) + LAM_INIT)
        od = (oT[:, :tq] - lam * oT[:, tq:]).T
        ms = jnp.mean(od * od, axis=-1, keepdims=True)
        o_ref[...] = ((od * lax.rsqrt(ms + EPS)) * sw_ref[...] * (1.0 - LAM_INIT)).astype(BF16)

        qT = jnp.concatenate([qT_ref[0], qT_ref[1]], axis=1)
        m_prev = m_r[...]
        n_parts = 2 * tq // ATTN_COLS
        m8, l8, acc = [None] * n_parts, [None] * n_parts, [None] * n_parts
        for lo in range(0, n_all, tk):
            k_tile = kc_ref[...] if lo == 0 else k_ref[lo - n_ctx:lo - n_ctx + tk, :]
            v_tile = vTc_ref[...] if lo == 0 else vT_ref[:, lo - n_ctx:lo - n_ctx + tk]
            for h in range(n_parts):
                cols = slice(h * ATTN_COLS, (h + 1) * ATTN_COLS)
                s_new = _dot(k_tile, qT[:, cols])
                s_w[lo:lo + tk, cols] = s_new
                part = jnp.max(s_new.reshape(tk // 8, 8, ATTN_COLS), axis=0)
                m8[h] = part if m8[h] is None else jnp.maximum(m8[h], part)

                p = jnp.exp2(s_r[lo:lo + tk, cols] - m_prev[:, cols])
                part = jnp.sum(p.reshape(tk // 8, 8, ATTN_COLS), axis=0)
                l8[h] = part if l8[h] is None else l8[h] + part
                pv = _dot(v_tile, p.astype(BF16))
                acc[h] = pv if acc[h] is None else acc[h] + pv
        m_w[...] = jnp.max(jnp.concatenate(m8, axis=1), axis=0, keepdims=True)
        acc_ref[...] = jnp.concatenate(acc, axis=1)
        l_ref[...] = jnp.concatenate(l8, axis=1)

    @pl.when(g % 2 == 0)
    def _():
        step(s0_ref, m0_ref, s1_ref, m1_ref)

    @pl.when(g % 2 == 1)
    def _():
        step(s1_ref, m1_ref, s0_ref, m0_ref)


def _attn_call(qT, kc, k, vTc, vT, lq1, lk1, lq2, lk2, sw, tq):
    B, H, _, _, L = qT.shape
    n_ctx = kc.shape[2]
    n_all = n_ctx + L
    nq = L // tq
    n_tiles = B * H * nq

    def tile(t):
        return t // (H * nq), (t // nq) % H, t % nq

    def scored(g):
        return tile(jnp.minimum(g, n_tiles - 1))

    def finished(g):
        return tile(jnp.clip(g - 1, 0, n_tiles - 1))

    def written(g):
        return tile(jnp.maximum(g - 2, 0))

    def q_map(g):
        b, h, i = scored(g)
        return b, h, 0, 0, i

    def k_map(g):
        b, h, _ = scored(g)
        return b, h, 0, 0

    def v_map(g):
        b, h, _ = finished(g)
        return b, h, 0, 0

    def o_map(g):
        b, h, i = written(g)
        return b, i, h

    vec = pl.BlockSpec((1, HEAD_DIM), lambda g: (0, 0))
    return pl.pallas_call(
        functools.partial(_attn_body, tq=tq),
        grid=(n_tiles + 2,),
        in_specs=[
            pl.BlockSpec((None, None, 2, LANES, tq), q_map),
            pl.BlockSpec((None, None, n_ctx, LANES), k_map),
            pl.BlockSpec((None, None, L, LANES), k_map),
            pl.BlockSpec((None, None, V_DIM, n_ctx), v_map),
            pl.BlockSpec((None, None, V_DIM, L), v_map),
            vec, vec, vec, vec,
            pl.BlockSpec((1, V_DIM), lambda g: (0, 0)),
        ],
        out_specs=pl.BlockSpec((None, tq, V_DIM), o_map),
        out_shape=jax.ShapeDtypeStruct((B, L, ATTN_W), BF16),
        scratch_shapes=[pltpu.VMEM((n_all, 2 * tq), F32), pltpu.VMEM((n_all, 2 * tq), F32),
                        pltpu.VMEM((1, 2 * tq), F32), pltpu.VMEM((1, 2 * tq), F32),
                        pltpu.VMEM((V_DIM, 2 * tq), F32), pltpu.VMEM((8, 2 * tq), F32)],
        compiler_params=_params(0, 1),
        name="attn",
    )(qT, kc, k, vTc, vT, lq1, lk1, lq2, lk2, sw)


def _cmul(xr, xi, yr, yi):
    return xr * yr - xi * yi, xr * yi + xi * yr


def _a_bar(are, aim, ldt):
    lr = jnp.minimum(are, MIN_NEG_RE)
    dt = jnp.exp(ldt)
    mag = jnp.exp(lr * dt)
    return lr, aim, mag * jnp.cos(aim * dt), mag * jnp.sin(aim * dt)


def _powers(ar, ai, n):
    pows = [(jnp.ones_like(ar), jnp.zeros_like(ar))]
    for _ in range(n):
        pows.append(_cmul(pows[-1][0], pows[-1][1], ar, ai))
    return pows


def _s5tab_body(aar_ref, aai_ref, adt_ref, abr_ref, abi_ref, bar_ref, bai_ref, bdt_ref, bcr_ref, bci_ref,
                car_ref, cai_ref, cdt_ref, dsk_ref, m_ref, bz_ref, cc_ref, a16_ref, lag_ref):
    d = pl.program_id(1)
    fwd = d == 0
    half = SLAB_STATE // 2
    sel = lambda a, b: (jnp.where(fwd, a[0], b[0]), jnp.where(fwd, a[1], b[1]))

    lr, li, ar, ai = _a_bar(aar_ref[...], aai_ref[...], adt_ref[...])
    den = lr * lr + li * li
    nr = ar - 1.0
    cfr = (nr * lr + ai * li) / den
    cfi = (ai * lr - nr * li) / den
    bbr, bbi = _cmul(cfr, cfi, abr_ref[...], abi_ref[...])
    pa = _powers(ar, ai, CHUNK - 1)
    row_a = lax.broadcasted_iota(jnp.int32, (SLAB, half), 0) // S5_GROUP
    col_a = lax.broadcasted_iota(jnp.int32, (SLAB, half), 1) // S5_STATE
    diag_a = row_a == col_a

    def expand_a(v):
        return jnp.where(diag_a, jnp.concatenate([v] * (half // LANES), axis=1), 0.0)

    for t in range(CHUNK):
        pw = sel(pa[CHUNK - 1 - t], pa[t])
        zr, zi = _cmul(pw[0], pw[1], bbr, bbi)
        bz_ref[t * SLAB:(t + 1) * SLAB, :] = jnp.concatenate([expand_a(zr), expand_a(zi)], axis=1).astype(BF16)

    _, _, br_, bi_ = _a_bar(bar_ref[...], bai_ref[...], bdt_ref[...])
    pb = _powers(br_, bi_, CHUNK)
    cr = bcr_ref[...]
    ci = bci_ref[...]
    row_b = lax.broadcasted_iota(jnp.int32, (half, SLAB), 0) // S5_STATE
    col_b = lax.broadcasted_iota(jnp.int32, (half, SLAB), 1) // S5_GROUP
    diag_b = row_b == col_b

    def expand_b(v):
        return jnp.where(diag_b, jnp.concatenate([v] * GROUPS_PER_SLAB, axis=0), 0.0)

    for t in range(CHUNK):
        pw = sel(pb[t + 1], pb[CHUNK - t])
        ctr, cti = _cmul(cr, ci, pw[0], pw[1])
        cc_ref[:, t * SLAB:(t + 1) * SLAB] = jnp.concatenate([expand_b(ctr), expand_b(-cti)], axis=0).astype(BF16)

    lane = lax.broadcasted_iota(jnp.int32, (SLAB, LANES), 1)
    lhs = jnp.where(lane < S5_STATE, bbr, bbi)
    r_k = lax.broadcasted_iota(jnp.int32, (SLAB, SLAB), 0)
    c_k = lax.broadcasted_iota(jnp.int32, (SLAB, SLAB), 1)
    diag_k = (r_k // S5_GROUP) == (c_k // S5_GROUP)
    sign = 1 - 2 * d
    for tau in range(CHUNK):
        ctr, cti = _cmul(cr, ci, pb[tau][0], pb[tau][1])
        rhs = jnp.concatenate([ctr, -cti], axis=0)
        ktau = jnp.where(diag_k, jnp.dot(lhs, rhs, preferred_element_type=F32,
                                         precision=lax.Precision.HIGHEST), 0.0)
        if tau == 0:
            @pl.when(fwd)
            def _():
                lag_ref[CHUNK - 1] = ktau + jnp.where(r_k == c_k, dsk_ref[...], 0.0)

            @pl.when(jnp.logical_not(fwd))
            def _():
                lag_ref[CHUNK - 1] = lag_ref[CHUNK - 1] + ktau
        else:
            lag_ref[CHUNK - 1 + sign * tau] = ktau

    @pl.when(jnp.logical_not(fwd))
    def _():
        lags = [lag_ref[n].astype(BF16) for n in range(2 * CHUNK - 1)]
        for tp in range(CHUNK):
            m_ref[tp * SLAB:(tp + 1) * SLAB, :] = jnp.concatenate(
                [lags[t - tp + CHUNK - 1] for t in range(CHUNK)], axis=1)

    _, _, xr, xi = _a_bar(car_ref[...], cai_ref[...], cdt_ref[...])
    for _ in range(4):
        xr, xi = _cmul(xr, xi, xr, xi)
    a16_ref[...] = jnp.concatenate([xr, xi], axis=1)


def _s5tab_call(s5_a_re, s5_a_im, s5_log_dt, s5_b_re, s5_b_im, s5_c_re, s5_c_im, s5_d):
    S, GS, C, P = N_SLABS, GROUPS_PER_SLAB, S5_GROUP, S5_STATE
    half = SLAB_STATE // 2

    def lay_a(a):
        return jnp.broadcast_to(a.reshape(2, S, GS, 1, 1, P), (2, S, GS, C, 2, P)).reshape(2, S, SLAB, LANES)

    def lay_b(a):
        a = a.reshape(2, S, GS, P).transpose(0, 1, 3, 2)
        return jnp.broadcast_to(a[..., None], (2, S, P, GS, C)).reshape(2, S, P, SLAB)

    ldt = jnp.broadcast_to(s5_log_dt[..., None], (2, S5_GROUPS, P))
    b_a = lambda b: jnp.broadcast_to(b.transpose(0, 1, 3, 2).reshape(2, S, GS, C, 1, P),
                                     (2, S, GS, C, 2, P)).reshape(2, S, SLAB, LANES)
    c_b = lambda c: c.reshape(2, S, GS, C, P).transpose(0, 1, 4, 2, 3).reshape(2, S, P, SLAB)
    lay_c = lambda a: a.reshape(2, S, 1, half)
    blk = lambda r, c: pl.BlockSpec((None, None, r, c), lambda s, d: (d, s, 0, 0))
    return pl.pallas_call(
        _s5tab_body,
        grid=(S, 2),
        in_specs=[blk(SLAB, LANES)] * 5 + [blk(P, SLAB)] * 5 + [blk(1, half)] * 3
                 + [pl.BlockSpec((None, 1, SLAB), lambda s, d: (s, 0, 0))],
        out_specs=[pl.BlockSpec((None, SLAB_IN, SLAB_IN), lambda s, d: (s, 0, 0)),
                   blk(SLAB_IN, SLAB_STATE), blk(SLAB_STATE, SLAB_IN), blk(1, SLAB_STATE)],
        out_shape=[jax.ShapeDtypeStruct((S, SLAB_IN, SLAB_IN), BF16),
                   jax.ShapeDtypeStruct((2, S, SLAB_IN, SLAB_STATE), BF16),
                   jax.ShapeDtypeStruct((2, S, SLAB_STATE, SLAB_IN), BF16),
                   jax.ShapeDtypeStruct((2, S, 1, SLAB_STATE), F32)],
        scratch_shapes=[pltpu.VMEM((2 * CHUNK - 1, SLAB, SLAB), F32)],
        compiler_params=_params(0, 2),
        name="s5_tab",
    )(lay_a(s5_a_re), lay_a(s5_a_im), lay_a(ldt), b_a(s5_b_re), b_a(s5_b_im),
      lay_b(s5_a_re), lay_b(s5_a_im), lay_b(ldt), c_b(s5_c_re), c_b(s5_c_im),
      lay_c(s5_a_re), lay_c(s5_a_im), lay_c(ldt), s5_d.reshape(S, 1, SLAB))


def _s5scan_body(uc_ref, ul_ref, bz_ref, a_ref, sin_ref, zc_ref, zl_ref, st_ref, *, n_cc, n_lc):
    d = pl.program_id(1)
    B = uc_ref.shape[0]
    n_lt = SLAB_STATE // LANES

    def chunk_inputs(u_ref, n, z_ref):
        z = _dot_cols(u_ref[...].reshape(B * n, SLAB_IN), bz_ref)
        for b in range(B):
            for t in range(n_lt):
                z_ref[t, pl.ds(b, n, stride=B), :] = z[b * n:(b + 1) * n, t * LANES:(t + 1) * LANES]

    chunk_inputs(uc_ref, n_cc, zc_ref)
    chunk_inputs(ul_ref, n_lc, zl_ref)
    a = a_ref[...]
    half = n_lt // 2
    a_re = [a[:, (t % half) * LANES:(t % half + 1) * LANES] for t in range(n_lt)]
    a_im = [a[:, (t % half + half) * LANES:(t % half + half + 1) * LANES] for t in range(n_lt)]

    def advance(s, z_ref, rows):
        out = []
        for t in range(n_lt):
            other = s[(t + half) % n_lt]
            cross = -(a_im[t] * other) if t < half else a_im[t] * other
            out.append(a_re[t] * s[t] + cross + z_ref[t, rows, :])
        return tuple(out)

    def ctx_step(i, s):
        kk = jnp.where(d == 0, i, n_cc - 1 - i)
        return advance(s, zc_ref, pl.ds(pl.multiple_of(kk * B, B), B))

    def lat_step(i, s):
        kk = jnp.where(d == 0, i, n_lc - 1 - i)
        rows = pl.ds(pl.multiple_of(kk * B, B), B)
        for t in range(n_lt):
            st_ref[t, rows, :] = s[t]
        return advance(s, zl_ref, rows)

    s = lax.fori_loop(0, n_cc, ctx_step, tuple(jnp.zeros((B, LANES), F32) for _ in range(n_lt)))
    lax.fori_loop(0, n_lc, lat_step, s)
    for b in range(B):
        for t in range(n_lt):
            sin_ref[b, :, t * LANES:(t + 1) * LANES] = st_ref[t, pl.ds(b, n_lc, stride=B), :].astype(BF16)


def _s5scan_call(uc, ul, bzx, a16x):
    S, B, n_cc, _ = uc.shape
    n_lc = ul.shape[2]
    n_lt = SLAB_STATE // LANES
    return pl.pallas_call(
        functools.partial(_s5scan_body, n_cc=n_cc, n_lc=n_lc),
        grid=(S, 2),
        in_specs=[pl.BlockSpec((None, B, n_cc, SLAB_IN), lambda s, d: (s, 0, 0, 0)),
                  pl.BlockSpec((None, B, n_lc, SLAB_IN), lambda s, d: (s, 0, 0, 0)),
                  pl.BlockSpec((None, None, SLAB_IN, SLAB_STATE), lambda s, d: (d, s, 0, 0)),
                  pl.BlockSpec((None, None, 1, SLAB_STATE), lambda s, d: (d, s, 0, 0))],
        out_specs=pl.BlockSpec((None, None, B, n_lc, SLAB_STATE), lambda s, d: (d, s, 0, 0, 0)),
        out_shape=jax.ShapeDtypeStruct((2, S, B, n_lc, SLAB_STATE), BF16),
        scratch_shapes=[pltpu.VMEM((n_lt, B * n_cc, LANES), F32),
                        pltpu.VMEM((n_lt, B * n_lc, LANES), F32),
                        pltpu.VMEM((n_lt, B * n_lc, LANES), F32)],
        compiler_params=_params(2),
        name="s5_scan",
    )(uc, ul, bzx, a16x)


def _s5out_body(ul_ref, m_ref, cc_ref, sin_ref, y_ref):
    B, n_lc, _ = ul_ref.shape
    rows = B * n_lc
    y = _dot_cols(ul_ref[...].reshape(rows, SLAB_IN), m_ref)
    y = y + _dot_cols(sin_ref[0].reshape(rows, SLAB_STATE), cc_ref.at[0])
    y = y + _dot_cols(sin_ref[1].reshape(rows, SLAB_STATE), cc_ref.at[1])
    y_ref[...] = y.reshape(B, n_lc, y.shape[-1])


def _s5out_call(ul, m, ccx, sin, nb):
    S, B, n_lc, _ = ul.shape
    return pl.pallas_call(
        _s5out_body,
        grid=(S, SLAB_IN // nb),
        in_specs=[pl.BlockSpec((None, B, n_lc, SLAB_IN), lambda s, j: (s, 0, 0, 0)),
                  pl.BlockSpec((None, SLAB_IN, nb), lambda s, j: (s, 0, j)),
                  pl.BlockSpec((2, None, SLAB_STATE, nb), lambda s, j: (0, s, 0, j)),
                  pl.BlockSpec((2, None, B, n_lc, SLAB_STATE), lambda s, j: (0, s, 0, 0, 0))],
        out_specs=pl.BlockSpec((None, B, n_lc, nb), lambda s, j: (s, 0, 0, j)),
        out_shape=jax.ShapeDtypeStruct((S, B, n_lc, SLAB_IN), F32),
        compiler_params=_params(2),
        name="s5_out",
    )(ul, m, ccx, sin)


def _merge_body(y_ref, o_ref, g_ref, x_ref, mod_ref, gw_ref, gb_ref, wbs_ref, wba_ref, wo_ref, out_ref, y_scr):
    ya = _dot_cols(o_ref[...], wba_ref)
    n_rows = y_ref.shape[1]
    for s in range(N_SLABS):
        for t in range(CHUNK):
            y_scr[s, pl.ds(t, n_rows, stride=CHUNK), :] = y_ref[s, :, t * SLAB:(t + 1) * SLAB]
    y = jnp.concatenate([y_scr[s] for s in range(N_SLABS)], axis=1)
    hh = 0.5 * y * (1.0 + jnp.tanh(math.sqrt(2.0 / math.pi) * (y + 0.044715 * (y * y * y))))
    gl = _dot_cols(hh.astype(BF16), gw_ref) + gb_ref[...]
    ys = _dot_cols((hh * _sigmoid(gl)).astype(BF16), wbs_ref)
    g = g_ref[...].astype(F32)
    mix = g[:, :D_MODEL] * ys + g[:, D_MODEL:] * ya
    out_ref[...] = x_ref[...] + mod_ref[2:3] * _dot_cols(mix.astype(BF16), wo_ref)


def _merge_call(y, o, g, x, mod3, gw, gb, wbs, wba, wo, tm):
    B, L, D = x.shape
    c2 = lambda b, i: (0, 0)
    return pl.pallas_call(
        _merge_body,
        grid=(B, L // tm),
        in_specs=[pl.BlockSpec((N_SLABS, None, tm // CHUNK, SLAB_IN), lambda b, i: (0, b, i, 0)),
                  pl.BlockSpec((None, tm, ATTN_W), lambda b, i: (b, i, 0)),
                  pl.BlockSpec((None, tm, 2 * D), lambda b, i: (b, i, 0)),
                  pl.BlockSpec((None, tm, D), lambda b, i: (b, i, 0)),
                  pl.BlockSpec((None, 6, D), lambda b, i: (b, 0, 0)),
                  pl.BlockSpec((S5_W, S5_W), c2), pl.BlockSpec((1, S5_W), c2),
                  pl.BlockSpec((S5_W, D), c2), pl.BlockSpec((ATTN_W, D), c2), pl.BlockSpec((D, D), c2)],
        out_specs=pl.BlockSpec((None, tm, D), lambda b, i: (b, i, 0)),
        out_shape=jax.ShapeDtypeStruct((B, L, D), F32),
        scratch_shapes=[pltpu.VMEM((N_SLABS, tm, SLAB), F32)],
        compiler_params=_params(2),
        name="merge",
    )(y, o, g, x, mod3, gw, gb, wbs, wba, wo)


def _ffn_body(xp_ref, x_ref, xn_ref, mod_ref, n2w_ref, wup_ref, cw_ref, cb_ref, wdn_ref, out_ref, *, tm):
    i = pl.program_id(1)
    last = pl.num_programs(1) - 1
    mod = mod_ref[...]
    n2w = n2w_ref[...]

    def modulated(v):
        ms = jnp.mean(v * v, axis=-1, keepdims=True)
        return (v * lax.rsqrt(ms + EPS)) * n2w * (1.0 + mod[4:5]) + mod[3:4]

    x = x_ref[...]
    hp = modulated(xp_ref[...]) * (i > 0).astype(F32)
    hn = modulated(xn_ref[...]) * (i < last).astype(F32)
    hcat = jnp.concatenate([hp, modulated(x), hn], axis=0).astype(BF16)
    rows = tm + 2 * HALO

    def conv(u, cw, cb):
        prev = pltpu.roll(u, 1, 0)[HALO:HALO + tm]
        nxt = pltpu.roll(u, rows - 1, 0)[HALO:HALO + tm]
        return cb + prev * cw[0:1] + u[HALO:HALO + tm] * cw[1:2] + nxt * cw[2:3]

    def col(ref, part, f):
        lo = part * D_FF + f * FF_CHUNK
        return ref[:, lo:lo + FF_CHUNK]

    ups, acts, acc = {}, {}, None
    for f in range(N_FF_CHUNKS + 2):
        if f < N_FF_CHUNKS:
            ups[f] = (_dot(hcat, col(wup_ref, 0, f)), _dot(hcat, col(wup_ref, 1, f)))
        e = f - 1
        if 0 <= e < N_FF_CHUNKS:
            ua, ug = ups.pop(e)
            ya = conv(ua, col(cw_ref, 0, e), col(cb_ref, 0, e))
            half_g = 0.5 * conv(ug, col(cw_ref, 1, e), col(cb_ref, 1, e))
            acts[e] = ((half_g + half_g * jnp.tanh(half_g)) * ya).astype(BF16)
        e = f - 2
        if 0 <= e < N_FF_CHUNKS and ((e + 1) % FF_DOWN_GROUP == 0 or e == N_FF_CHUNKS - 1):
            lo = e // FF_DOWN_GROUP * FF_DOWN_GROUP
            group = jnp.concatenate([acts.pop(j) for j in range(lo, e + 1)], axis=1)
            down = _dot_cols(group, wdn_ref, slice(lo * FF_CHUNK, (e + 1) * FF_CHUNK))
            acc = down if acc is None else acc + down
    out_ref[...] = x + mod[5:6] * acc


def _ffn_call(x1, mod3, n2w, wup, cw, cb, wdn, tm):
    B, L, D = x1.shape
    nh = tm // HALO
    n_halo_blocks = L // HALO
    c2 = lambda b, i: (0, 0)
    return pl.pallas_call(
        functools.partial(_ffn_body, tm=tm),
        grid=(B, L // tm),
        in_specs=[pl.BlockSpec((None, HALO, D), lambda b, i: (b, jnp.maximum(i * nh - 1, 0), 0)),
                  pl.BlockSpec((None, tm, D), lambda b, i: (b, i, 0)),
                  pl.BlockSpec((None, HALO, D), lambda b, i: (b, jnp.minimum((i + 1) * nh, n_halo_blocks - 1), 0)),
                  pl.BlockSpec((None, 6, D), lambda b, i: (b, 0, 0)),
                  pl.BlockSpec((1, D), c2),
                  pl.BlockSpec((D, 2 * D_FF), c2),
                  pl.BlockSpec((3, 2 * D_FF), c2),
                  pl.BlockSpec((1, 2 * D_FF), c2),
                  pl.BlockSpec((D_FF, D), c2)],
        out_specs=pl.BlockSpec((None, tm, D), lambda b, i: (b, i, 0)),
        out_shape=jax.ShapeDtypeStruct((B, L, D), F32),
        compiler_params=_params(2),
        name="ffn",
    )(x1, x1, x1, mod3, n2w, wup, cw, cb, wdn)


def _rope_tables(L):
    half = HEAD_DIM // 2
    inv_freq = ROPE_THETA ** (-jnp.arange(0, half, 2, dtype=F32) / half)
    lane = jnp.arange(PAIR)
    freq = inv_freq[lane % (half // 2)]
    sign = jnp.where(lane % half < half // 2, -1.0, 1.0)
    row_lane = (lane % HEAD_DIM < half)[None, None, :]
    ang_r = jnp.arange(L // GRID_W, dtype=F32)[:, None] * freq[None, :]
    ang_c = jnp.arange(GRID_W, dtype=F32)[:, None] * freq[None, :]
    table = lambda f: jnp.where(row_lane, f(ang_r)[:, None, :], f(ang_c)[None, :, :]).reshape(L, PAIR)
    return table(jnp.cos), table(lambda a: jnp.sin(a) * sign)


def kernel(x, c, ctx, c_ctx, ada_w, ada_b, norm1_w, w_in, b_gate, q_norm_w, k_norm_w, lam_q1, lam_k1, lam_q2, lam_k2, subln_w, s5_a_re, s5_a_im, s5_log_dt, s5_b_re, s5_b_im, s5_c_re, s5_c_im, s5_d, glu_w, glu_b, w_branch_s5, w_branch_attn, w_out, norm2_w, w_up, conv_w, conv_b, w_down):
    B, L, D = x.shape
    assert ada_w.shape[0] == 1 and D == D_MODEL and B + 1 <= 8

    cc = jnp.zeros((8, D), F32).at[:B].set(c).at[B].set(c_ctx)
    mod3 = _mod_call(cc, ada_w[0], ada_b).reshape(8, 6, D)

    cos_t, sin_t = _rope_tables(L)
    lane = jnp.arange(PAIR)
    seg = jnp.where((lane[:, None] // HEAD_DIM) == (lane[None, :] // HEAD_DIM), 1.0 / HEAD_DIM, 0.0).astype(BF16)
    qw = jnp.tile(q_norm_w[0], PAIR // HEAD_DIM)[None]
    kw = jnp.tile(k_norm_w[0], PAIR // HEAD_DIM)[None]
    w_in_b = w_in[0].astype(BF16)
    kc, vTc, uc = _inproj_call(ctx, mod3, lambda b: B, norm1_w, w_in_b, kw, seg)
    k, vT, ul, qT, g = _inproj_call(x, mod3, lambda b: b, norm1_w, w_in_b, kw, seg,
                                    latent_args=(qw, b_gate, cos_t, sin_t, IN_TILE))

    o = _attn_call(qT, kc, k, vTc, vT, lam_q1, lam_k1, lam_q2, lam_k2, subln_w, tq=512)

    m, bzx, ccx, a16x = _s5tab_call(s5_a_re[0], s5_a_im[0], s5_log_dt[0], s5_b_re[0], s5_b_im[0],
                                    s5_c_re[0], s5_c_im[0], s5_d[0])
    sin = _s5scan_call(uc, ul, bzx, a16x)
    y = _s5out_call(ul, m, ccx, sin, nb=512)

    x1 = _merge_call(y, o, g, x, mod3, glu_w[0].astype(BF16), glu_b, w_branch_s5[0].astype(BF16),
                     w_branch_attn[0].astype(BF16), w_out[0].astype(BF16), tm=512)

    return _ffn_call(x1, mod3, norm2_w, w_up[0].astype(BF16), conv_w[0], conv_b, w_down[0].astype(BF16), tm=512)
```

```python
import functools
import math

import jax
import jax.numpy as jnp
from jax import lax
from jax.experimental import pallas as pl
from jax.experimental.pallas import tpu as pltpu

D_MODEL = 1024
GRID_W = 64
N_HEADS = 8
HEAD_DIM = 64
V_DIM = 2 * HEAD_DIM
QK_W = N_HEADS * 2 * HEAD_DIM
ATTN_W = N_HEADS * V_DIM
S5_W = 512
S5_GROUP = 16
S5_GROUPS = S5_W // S5_GROUP
S5_STATE = 64
D_FF = 2816
ROPE_THETA = 10000.0
EPS = 1e-6
MIN_NEG_RE = -1e-4
K_OFF = QK_W
V_OFF = 2 * QK_W
U_OFF = 2 * QK_W + ATTN_W
G_OFF = U_OFF + S5_W
N_IN = G_OFF + 2 * D_MODEL
LAM_INIT = 0.8 - 0.6 * math.exp(-0.3 * 0)
Q_SCALE = HEAD_DIM ** -0.5 * math.log2(math.e)

LANES = 128
PAIR = 2 * LANES
TOK_TILE = 256
ATTN_COLS = 256
IN_TILE = 512
CHUNK = 16
SLAB = LANES
N_SLABS = S5_W // SLAB
GROUPS_PER_SLAB = SLAB // S5_GROUP
SLAB_IN = CHUNK * SLAB
SLAB_STATE = GROUPS_PER_SLAB * 2 * S5_STATE
FF_CHUNK = 256
N_FF_CHUNKS = D_FF // FF_CHUNK
FF_DOWN_GROUP = 4
HALO = 8
VMEM_LIMIT = 56 * 1024 * 1024

F32 = jnp.float32
BF16 = jnp.bfloat16


def _sigmoid(x):
    return 1.0 / (1.0 + jnp.exp(-x))


def _dot(a, b):
    return jnp.dot(a, b, preferred_element_type=F32)


def _dot_cols(a, b_ref, rows=slice(None)):
    n = b_ref.shape[-1]
    return jnp.concatenate([_dot(a, b_ref[rows, c:c + PAIR]) for c in range(0, n, PAIR)], axis=1)


def _params(n_parallel, n_arbitrary=0):
    return pltpu.CompilerParams(
        dimension_semantics=("parallel",) * n_parallel + ("arbitrary",) * n_arbitrary,
        vmem_limit_bytes=VMEM_LIMIT)


def _mod_body(c_ref, w_ref, b_ref, o_ref):
    c = c_ref[...]
    s = c * _sigmoid(c)
    o_ref[...] = jnp.dot(s, w_ref[...], preferred_element_type=F32,
                         precision=lax.Precision.HIGHEST) + b_ref[...]


def _mod_call(cc, ada_w, ada_b):
    n = ada_w.shape[1]
    return pl.pallas_call(
        _mod_body,
        grid=(n // D_MODEL,),
        in_specs=[pl.BlockSpec((8, D_MODEL), lambda j: (0, 0)),
                  pl.BlockSpec((D_MODEL, D_MODEL), lambda j: (0, j)),
                  pl.BlockSpec((1, D_MODEL), lambda j: (0, j))],
        out_specs=pl.BlockSpec((8, D_MODEL), lambda j: (0, j)),
        out_shape=jax.ShapeDtypeStruct((8, n), F32),
        compiler_params=_params(1),
        name="mod",
    )(cc, ada_w, ada_b)


def _inproj_body(*refs, latent):
    if latent:
        (x_ref, mod_ref, n1w_ref, w_ref, kw_ref, seg_ref, qw_ref, bg_ref, cos_ref, sin_ref,
         k_ref, vT_ref, u_ref, qT_ref, g_ref, u_scr) = refs
    else:
        x_ref, mod_ref, n1w_ref, w_ref, kw_ref, seg_ref, k_ref, vT_ref, u_ref, u_scr = refs
    T = x_ref.shape[0]
    xt = x_ref[...]
    ms = jnp.mean(xt * xt, axis=-1, keepdims=True)
    mod = mod_ref[...]
    h = (xt * lax.rsqrt(ms + EPS)) * n1w_ref[...] * (1.0 + mod[1:2]) + mod[0:1]
    hb = h.astype(BF16)
    seg = seg_ref[...]
    if latent:
        cos = cos_ref[...]
        sin = sin_ref[...]
        first_half = (lax.broadcasted_iota(jnp.int32, cos.shape, 1) % 32) < 16

    def norm_rope(z, w):
        y = z * lax.rsqrt(_dot((z * z).astype(BF16), seg) + EPS) * w
        if not latent:
            return y
        partner = jnp.where(first_half, pltpu.roll(y, PAIR - 16, 1), pltpu.roll(y, 16, 1))
        return y * cos + partner * sin

    def put_k(c, z):
        y = norm_rope(z, kw_ref[...]).astype(BF16)
        k_ref[2 * c] = y[:, :LANES]
        k_ref[2 * c + 1] = y[:, LANES:]

    def put_v(c, z):
        vT_ref[2 * c] = z[:, :LANES].T.astype(BF16)
        vT_ref[2 * c + 1] = z[:, LANES:].T.astype(BF16)

    def put_u(c, z):
        u_scr[2 * c] = z[:, :SLAB]
        u_scr[2 * c + 1] = z[:, SLAB:]
        if c == S5_W // PAIR - 1:
            for s in range(N_SLABS):
                for t in range(CHUNK):
                    piece = u_scr[s, pl.ds(t, T // CHUNK, stride=CHUNK), :]
                    u_ref[s, :, t * SLAB:(t + 1) * SLAB] = piece.astype(BF16)

    def put_q(c, z):
        y = norm_rope(z, qw_ref[...]) * Q_SCALE
        row = lax.broadcasted_iota(jnp.int32, (LANES, T), 0)
        for hh, qT in ((2 * c, y[:, :LANES].T), (2 * c + 1, y[:, LANES:].T)):
            qT_ref[hh, 0] = jnp.where(row < HEAD_DIM, qT, 0.0).astype(BF16)
            qT_ref[hh, 1] = jnp.where(row >= HEAD_DIM, qT, 0.0).astype(BF16)

    def put_g(c, z):
        lo = c * PAIR
        g_ref[:, lo:lo + PAIR] = _sigmoid(z + bg_ref[:, lo:lo + PAIR]).astype(BF16)

    sections = [(K_OFF, QK_W, put_k), (V_OFF, ATTN_W, put_v), (U_OFF, S5_W, put_u)]
    if latent:
        sections += [(0, QK_W, put_q), (G_OFF, 2 * D_MODEL, put_g)]
    tasks = [(off + c * PAIR, c, put) for off, width, put in sections for c in range(width // PAIR)]
    project = lambda lo: _dot(hb, w_ref[:, lo:lo + PAIR])
    z_next = project(tasks[0][0])
    for n, (_, c, put) in enumerate(tasks):
        z = z_next
        if n + 1 < len(tasks):
            z_next = project(tasks[n + 1][0])
        put(c, z)


def _inproj_call(x, mod3, mod_row, n1w, w_in_b, kw, seg, latent_args=None):
    B, n, D = x.shape
    latent = latent_args is not None
    T = latent_args[4] if latent else n
    assert n % T == 0
    const2 = lambda b, i: (0, 0)
    in_specs = [
        pl.BlockSpec((None, T, D), lambda b, i: (b, i, 0)),
        pl.BlockSpec((None, 6, D), lambda b, i: (mod_row(b), 0, 0)),
        pl.BlockSpec((1, D), const2),
        pl.BlockSpec((D, N_IN), const2),
        pl.BlockSpec((1, PAIR), const2),
        pl.BlockSpec((PAIR, PAIR), const2),
    ]
    out_specs = [
        pl.BlockSpec((None, N_HEADS, T, LANES), lambda b, i: (b, 0, i, 0)),
        pl.BlockSpec((None, N_HEADS, V_DIM, T), lambda b, i: (b, 0, 0, i)),
        pl.BlockSpec((N_SLABS, None, T // CHUNK, SLAB_IN), lambda b, i: (0, b, i, 0)),
    ]
    out_shape = [
        jax.ShapeDtypeStruct((B, N_HEADS, n, LANES), BF16),
        jax.ShapeDtypeStruct((B, N_HEADS, V_DIM, n), BF16),
        jax.ShapeDtypeStruct((N_SLABS, B, n // CHUNK, SLAB_IN), BF16),
    ]
    args = [x, mod3, n1w, w_in_b, kw, seg]
    if latent:
        qw, bg, cos_t, sin_t, _ = latent_args
        in_specs += [pl.BlockSpec((1, PAIR), const2), pl.BlockSpec((1, 2 * D), const2),
                     pl.BlockSpec((T, PAIR), lambda b, i: (i, 0)), pl.BlockSpec((T, PAIR), lambda b, i: (i, 0))]
        out_specs += [pl.BlockSpec((None, N_HEADS, 2, LANES, T), lambda b, i: (b, 0, 0, 0, i)),
                      pl.BlockSpec((None, T, 2 * D), lambda b, i: (b, i, 0))]
        out_shape += [jax.ShapeDtypeStruct((B, N_HEADS, 2, LANES, n), BF16),
                      jax.ShapeDtypeStruct((B, n, 2 * D), BF16)]
        args += [qw, bg, cos_t, sin_t]
    return pl.pallas_call(
        functools.partial(_inproj_body, latent=latent),
        grid=(B, n // T),
        in_specs=in_specs,
        out_specs=out_specs,
        out_shape=out_shape,
        scratch_shapes=[pltpu.VMEM((N_SLABS, T, SLAB), F32)],
        compiler_params=_params(2),
        name="in_proj_latent" if latent else "in_proj_ctx",
    )(*args)


def _attn_body(qT_ref, kc_ref, k_ref, vTc_ref, vT_ref, lq1_ref, lk1_ref, lq2_ref, lk2_ref, sw_ref, wup_ref, wdn_ref,
               o_ref, wupb_ref, wdnb_ref, s0_ref, s1_ref, m0_ref, m1_ref, acc_ref, l_ref, *, tq):
    g = pl.program_id(0)
    wupb_ref[...] = wup_ref[...].astype(BF16)
    wdnb_ref[...] = wdn_ref[...].astype(BF16)
    tk = TOK_TILE
    n_ctx = kc_ref.shape[0]
    n_all = n_ctx + k_ref.shape[0]
    assert n_ctx == tk

    @pl.when(g == 0)
    def _():
        s1_ref[...] = jnp.zeros(s1_ref.shape, F32)
        m1_ref[...] = jnp.zeros(m1_ref.shape, F32)
        acc_ref[...] = jnp.zeros(acc_ref.shape, F32)
        l_ref[...] = jnp.ones(l_ref.shape, F32)

    def step(s_w, m_w, s_r, m_r):
        oT = acc_ref[...] / jnp.sum(l_ref[...], axis=0, keepdims=True)
        lam = (jnp.exp(jnp.sum(lq1_ref[...] * lk1_ref[...], axis=-1, keepdims=True))
               - jnp.exp(jnp.sum(lq2_ref[...] * lk2_ref[...], axis=-1, keepdims=True)) + LAM_INIT)
        od = (oT[:, :tq] - lam * oT[:, tq:]).T
        ms = jnp.mean(od * od, axis=-1, keepdims=True)
        o_ref[...] = ((od * lax.rsqrt(ms + EPS)) * sw_ref[...] * (1.0 - LAM_INIT)).astype(BF16)

        qT = jnp.concatenate([qT_ref[0], qT_ref[1]], axis=1)
        m_prev = m_r[...]
        n_parts = 2 * tq // ATTN_COLS
        m8, l8, acc = [None] * n_parts, [None] * n_parts, [None] * n_parts
        for lo in range(0, n_all, tk):
            k_tile = kc_ref[...] if lo == 0 else k_ref[lo - n_ctx:lo - n_ctx + tk, :]
            v_tile = vTc_ref[...] if lo == 0 else vT_ref[:, lo - n_ctx:lo - n_ctx + tk]
            for h in range(n_parts):
                cols = slice(h * ATTN_COLS, (h + 1) * ATTN_COLS)
                s_new = _dot(k_tile, qT[:, cols])
                s_w[lo:lo + tk, cols] = s_new
                part = jnp.max(s_new.reshape(tk // 8, 8, ATTN_COLS), axis=0)
                m8[h] = part if m8[h] is None else jnp.maximum(m8[h], part)

                p = jnp.exp2(s_r[lo:lo + tk, cols] - m_prev[:, cols])
                part = jnp.sum(p.reshape(tk // 8, 8, ATTN_COLS), axis=0)
                l8[h] = part if l8[h] is None else l8[h] + part
                pv = _dot(v_tile, p.astype(BF16))
                acc[h] = pv if acc[h] is None else acc[h] + pv
        m_w[...] = jnp.max(jnp.concatenate(m8, axis=1), axis=0, keepdims=True)
        acc_ref[...] = jnp.concatenate(acc, axis=1)
        l_ref[...] = jnp.concatenate(l8, axis=1)

    @pl.when(g % 2 == 0)
    def _():
        step(s0_ref, m0_ref, s1_ref, m1_ref)

    @pl.when(g % 2 == 1)
    def _():
        step(s1_ref, m1_ref, s0_ref, m0_ref)


def _attn_call(qT, kc, k, vTc, vT, lq1, lk1, lq2, lk2, sw, w_up, w_down, tq):
    B, H, _, _, L = qT.shape
    n_ctx = kc.shape[2]
    n_all = n_ctx + L
    nq = L // tq
    n_tiles = B * H * nq

    def tile(t):
        return t // (H * nq), (t // nq) % H, t % nq

    def scored(g):
        return tile(jnp.minimum(g, n_tiles - 1))

    def finished(g):
        return tile(jnp.clip(g - 1, 0, n_tiles - 1))

    def written(g):
        return tile(jnp.maximum(g - 2, 0))

    def q_map(g):
        b, h, i = scored(g)
        return b, h, 0, 0, i

    def k_map(g):
        b, h, _ = scored(g)
        return b, h, 0, 0

    def v_map(g):
        b, h, _ = finished(g)
        return b, h, 0, 0

    def o_map(g):
        b, h, i = written(g)
        return b, i, h

    vec = pl.BlockSpec((1, HEAD_DIM), lambda g: (0, 0))

    def cast_spec(w):
        rows = 16
        while w.shape[0] % rows or w.shape[0] // rows > n_tiles:
            rows += 16
        return pl.BlockSpec((rows, w.shape[1]), lambda g: (jnp.minimum(g, w.shape[0] // rows - 1), 0))

    up_spec, dn_spec = cast_spec(w_up), cast_spec(w_down)
    return pl.pallas_call(
        functools.partial(_attn_body, tq=tq),
        grid=(n_tiles + 2,),
        in_specs=[
            pl.BlockSpec((None, None, 2, LANES, tq), q_map),
            pl.BlockSpec((None, None, n_ctx, LANES), k_map),
            pl.BlockSpec((None, None, L, LANES), k_map),
            pl.BlockSpec((None, None, V_DIM, n_ctx), v_map),
            pl.BlockSpec((None, None, V_DIM, L), v_map),
            vec, vec, vec, vec,
            pl.BlockSpec((1, V_DIM), lambda g: (0, 0)),
            up_spec, dn_spec,
        ],
        out_specs=[pl.BlockSpec((None, tq, V_DIM), o_map), up_spec, dn_spec],
        out_shape=[jax.ShapeDtypeStruct((B, L, ATTN_W), BF16),
                   jax.ShapeDtypeStruct(w_up.shape, BF16), jax.ShapeDtypeStruct(w_down.shape, BF16)],
        scratch_shapes=[pltpu.VMEM((n_all, 2 * tq), F32), pltpu.VMEM((n_all, 2 * tq), F32),
                        pltpu.VMEM((1, 2 * tq), F32), pltpu.VMEM((1, 2 * tq), F32),
                        pltpu.VMEM((V_DIM, 2 * tq), F32), pltpu.VMEM((8, 2 * tq), F32)],
        compiler_params=_params(0, 1),
        name="attn",
    )(qT, kc, k, vTc, vT, lq1, lk1, lq2, lk2, sw, w_up, w_down)


def _cmul(xr, xi, yr, yi):
    return xr * yr - xi * yi, xr * yi + xi * yr


def _a_bar(are, aim, ldt):
    lr = jnp.minimum(are, MIN_NEG_RE)
    dt = jnp.exp(ldt)
    mag = jnp.exp(lr * dt)
    return lr, aim, mag * jnp.cos(aim * dt), mag * jnp.sin(aim * dt)


def _powers(ar, ai, n):
    pows = [(jnp.ones_like(ar), jnp.zeros_like(ar))]
    for _ in range(n):
        pows.append(_cmul(pows[-1][0], pows[-1][1], ar, ai))
    return pows


def _s5tab_body(aar_ref, aai_ref, adt_ref, abr_ref, abi_ref, bar_ref, bai_ref, bdt_ref, bcr_ref, bci_ref,
                car_ref, cai_ref, cdt_ref, dsk_ref, m_ref, bz_ref, cc_ref, a16_ref, lag_ref):
    d = pl.program_id(1)
    fwd = d == 0
    half = SLAB_STATE // 2
    sel = lambda a, b: (jnp.where(fwd, a[0], b[0]), jnp.where(fwd, a[1], b[1]))

    lr, li, ar, ai = _a_bar(aar_ref[...], aai_ref[...], adt_ref[...])
    den = lr * lr + li * li
    nr = ar - 1.0
    cfr = (nr * lr + ai * li) / den
    cfi = (ai * lr - nr * li) / den
    bbr, bbi = _cmul(cfr, cfi, abr_ref[...], abi_ref[...])
    pa = _powers(ar, ai, CHUNK - 1)
    row_a = lax.broadcasted_iota(jnp.int32, (SLAB, half), 0) // S5_GROUP
    col_a = lax.broadcasted_iota(jnp.int32, (SLAB, half), 1) // S5_STATE
    diag_a = row_a == col_a

    def expand_a(v):
        return jnp.where(diag_a, jnp.concatenate([v] * (half // LANES), axis=1), 0.0)

    for t in range(CHUNK):
        pw = sel(pa[CHUNK - 1 - t], pa[t])
        zr, zi = _cmul(pw[0], pw[1], bbr, bbi)
        bz_ref[t * SLAB:(t + 1) * SLAB, :] = jnp.concatenate([expand_a(zr), expand_a(zi)], axis=1).astype(BF16)

    _, _, br_, bi_ = _a_bar(bar_ref[...], bai_ref[...], bdt_ref[...])
    pb = _powers(br_, bi_, CHUNK)
    cr = bcr_ref[...]
    ci = bci_ref[...]
    row_b = lax.broadcasted_iota(jnp.int32, (half, SLAB), 0) // S5_STATE
    col_b = lax.broadcasted_iota(jnp.int32, (half, SLAB), 1) // S5_GROUP
    diag_b = row_b == col_b

    def expand_b(v):
        return jnp.where(diag_b, jnp.concatenate([v] * GROUPS_PER_SLAB, axis=0), 0.0)

    for t in range(CHUNK):
        pw = sel(pb[t + 1], pb[CHUNK - t])
        ctr, cti = _cmul(cr, ci, pw[0], pw[1])
        cc_ref[:, t * SLAB:(t + 1) * SLAB] = jnp.concatenate([expand_b(ctr), expand_b(-cti)], axis=0).astype(BF16)

    lane = lax.broadcasted_iota(jnp.int32, (SLAB, LANES), 1)
    lhs = jnp.where(lane < S5_STATE, bbr, bbi)
    r_k = lax.broadcasted_iota(jnp.int32, (SLAB, SLAB), 0)
    c_k = lax.broadcasted_iota(jnp.int32, (SLAB, SLAB), 1)
    diag_k = (r_k // S5_GROUP) == (c_k // S5_GROUP)
    sign = 1 - 2 * d
    for tau in range(CHUNK):
        ctr, cti = _cmul(cr, ci, pb[tau][0], pb[tau][1])
        rhs = jnp.concatenate([ctr, -cti], axis=0)
        ktau = jnp.where(diag_k, jnp.dot(lhs, rhs, preferred_element_type=F32,
                                         precision=lax.Precision.HIGHEST), 0.0)
        if tau == 0:
            @pl.when(fwd)
            def _():
                lag_ref[CHUNK - 1] = ktau + jnp.where(r_k == c_k, dsk_ref[...], 0.0)

            @pl.when(jnp.logical_not(fwd))
            def _():
                lag_ref[CHUNK - 1] = lag_ref[CHUNK - 1] + ktau
        else:
            lag_ref[CHUNK - 1 + sign * tau] = ktau

    @pl.when(jnp.logical_not(fwd))
    def _():
        lags = [lag_ref[n].astype(BF16) for n in range(2 * CHUNK - 1)]
        for tp in range(CHUNK):
            m_ref[tp * SLAB:(tp + 1) * SLAB, :] = jnp.concatenate(
                [lags[t - tp + CHUNK - 1] for t in range(CHUNK)], axis=1)

    _, _, xr, xi = _a_bar(car_ref[...], cai_ref[...], cdt_ref[...])
    for _ in range(4):
        xr, xi = _cmul(xr, xi, xr, xi)
    a16_ref[...] = jnp.concatenate([xr, xi], axis=1)


def _s5tab_call(s5_a_re, s5_a_im, s5_log_dt, s5_b_re, s5_b_im, s5_c_re, s5_c_im, s5_d):
    S, GS, C, P = N_SLABS, GROUPS_PER_SLAB, S5_GROUP, S5_STATE
    half = SLAB_STATE // 2

    def lay_a(a):
        return jnp.broadcast_to(a.reshape(2, S, GS, 1, 1, P), (2, S, GS, C, 2, P)).reshape(2, S, SLAB, LANES)

    def lay_b(a):
        a = a.reshape(2, S, GS, P).transpose(0, 1, 3, 2)
        return jnp.broadcast_to(a[..., None], (2, S, P, GS, C)).reshape(2, S, P, SLAB)

    ldt = jnp.broadcast_to(s5_log_dt[..., None], (2, S5_GROUPS, P))
    b_a = lambda b: jnp.broadcast_to(b.transpose(0, 1, 3, 2).reshape(2, S, GS, C, 1, P),
                                     (2, S, GS, C, 2, P)).reshape(2, S, SLAB, LANES)
    c_b = lambda c: c.reshape(2, S, GS, C, P).transpose(0, 1, 4, 2, 3).reshape(2, S, P, SLAB)
    lay_c = lambda a: a.reshape(2, S, 1, half)
    blk = lambda r, c: pl.BlockSpec((None, None, r, c), lambda s, d: (d, s, 0, 0))
    return pl.pallas_call(
        _s5tab_body,
        grid=(S, 2),
        in_specs=[blk(SLAB, LANES)] * 5 + [blk(P, SLAB)] * 5 + [blk(1, half)] * 3
                 + [pl.BlockSpec((None, 1, SLAB), lambda s, d: (s, 0, 0))],
        out_specs=[pl.BlockSpec((None, SLAB_IN, SLAB_IN), lambda s, d: (s, 0, 0)),
                   blk(SLAB_IN, SLAB_STATE), blk(SLAB_STATE, SLAB_IN), blk(1, SLAB_STATE)],
        out_shape=[jax.ShapeDtypeStruct((S, SLAB_IN, SLAB_IN), BF16),
                   jax.ShapeDtypeStruct((2, S, SLAB_IN, SLAB_STATE), BF16),
                   jax.ShapeDtypeStruct((2, S, SLAB_STATE, SLAB_IN), BF16),
                   jax.ShapeDtypeStruct((2, S, 1, SLAB_STATE), F32)],
        scratch_shapes=[pltpu.VMEM((2 * CHUNK - 1, SLAB, SLAB), F32)],
        compiler_params=_params(0, 2),
        name="s5_tab",
    )(lay_a(s5_a_re), lay_a(s5_a_im), lay_a(ldt), b_a(s5_b_re), b_a(s5_b_im),
      lay_b(s5_a_re), lay_b(s5_a_im), lay_b(ldt), c_b(s5_c_re), c_b(s5_c_im),
      lay_c(s5_a_re), lay_c(s5_a_im), lay_c(ldt), s5_d.reshape(S, 1, SLAB))


def _s5scan_body(uc_ref, ul_ref, bz_ref, a_ref, sin_ref, zc_ref, zl_ref, st_ref, *, n_cc, n_lc):
    d = pl.program_id(1)
    B = uc_ref.shape[0]
    n_lt = SLAB_STATE // LANES

    def chunk_inputs(u_ref, n, z_ref):
        z = _dot_cols(u_ref[...].reshape(B * n, SLAB_IN), bz_ref)
        for b in range(B):
            for t in range(n_lt):
                z_ref[t, pl.ds(b, n, stride=B), :] = z[b * n:(b + 1) * n, t * LANES:(t + 1) * LANES]

    chunk_inputs(uc_ref, n_cc, zc_ref)
    chunk_inputs(ul_ref, n_lc, zl_ref)
    a = a_ref[...]
    half = n_lt // 2
    a_re = [a[:, (t % half) * LANES:(t % half + 1) * LANES] for t in range(n_lt)]
    a_im = [a[:, (t % half + half) * LANES:(t % half + half + 1) * LANES] for t in range(n_lt)]

    def advance(s, z_ref, rows):
        out = []
        for t in range(n_lt):
            other = s[(t + half) % n_lt]
            cross = -(a_im[t] * other) if t < half else a_im[t] * other
            out.append(a_re[t] * s[t] + cross + z_ref[t, rows, :])
        return tuple(out)

    def ctx_step(i, s):
        kk = jnp.where(d == 0, i, n_cc - 1 - i)
        return advance(s, zc_ref, pl.ds(pl.multiple_of(kk * B, B), B))

    def lat_step(i, s):
        kk = jnp.where(d == 0, i, n_lc - 1 - i)
        rows = pl.ds(pl.multiple_of(kk * B, B), B)
        for t in range(n_lt):
            st_ref[t, rows, :] = s[t]
        return advance(s, zl_ref, rows)

    s = lax.fori_loop(0, n_cc, ctx_step, tuple(jnp.zeros((B, LANES), F32) for _ in range(n_lt)))
    lax.fori_loop(0, n_lc, lat_step, s)
    for b in range(B):
        for t in range(n_lt):
            sin_ref[b, :, t * LANES:(t + 1) * LANES] = st_ref[t, pl.ds(b, n_lc, stride=B), :].astype(BF16)


def _s5scan_call(uc, ul, bzx, a16x):
    S, B, n_cc, _ = uc.shape
    n_lc = ul.shape[2]
    n_lt = SLAB_STATE // LANES
    return pl.pallas_call(
        functools.partial(_s5scan_body, n_cc=n_cc, n_lc=n_lc),
        grid=(S, 2),
        in_specs=[pl.BlockSpec((None, B, n_cc, SLAB_IN), lambda s, d: (s, 0, 0, 0)),
                  pl.BlockSpec((None, B, n_lc, SLAB_IN), lambda s, d: (s, 0, 0, 0)),
                  pl.BlockSpec((None, None, SLAB_IN, SLAB_STATE), lambda s, d: (d, s, 0, 0)),
                  pl.BlockSpec((None, None, 1, SLAB_STATE), lambda s, d: (d, s, 0, 0))],
        out_specs=pl.BlockSpec((None, None, B, n_lc, SLAB_STATE), lambda s, d: (d, s, 0, 0, 0)),
        out_shape=jax.ShapeDtypeStruct((2, S, B, n_lc, SLAB_STATE), BF16),
        scratch_shapes=[pltpu.VMEM((n_lt, B * n_cc, LANES), F32),
                        pltpu.VMEM((n_lt, B * n_lc, LANES), F32),
                        pltpu.VMEM((n_lt, B * n_lc, LANES), F32)],
        compiler_params=_params(2),
        name="s5_scan",
    )(uc, ul, bzx, a16x)


def _s5out_body(ul_ref, m_ref, cc_ref, sin_ref, y_ref):
    B, n_lc, _ = ul_ref.shape
    rows = B * n_lc
    y = _dot_cols(ul_ref[...].reshape(rows, SLAB_IN), m_ref)
    y = y + _dot_cols(sin_ref[0].reshape(rows, SLAB_STATE), cc_ref.at[0])
    y = y + _dot_cols(sin_ref[1].reshape(rows, SLAB_STATE), cc_ref.at[1])
    y_ref[...] = y.reshape(B, n_lc, y.shape[-1])


def _s5out_call(ul, m, ccx, sin, nb):
    S, B, n_lc, _ = ul.shape
    return pl.pallas_call(
        _s5out_body,
        grid=(S, SLAB_IN // nb),
        in_specs=[pl.BlockSpec((None, B, n_lc, SLAB_IN), lambda s, j: (s, 0, 0, 0)),
                  pl.BlockSpec((None, SLAB_IN, nb), lambda s, j: (s, 0, j)),
                  pl.BlockSpec((2, None, SLAB_STATE, nb), lambda s, j: (0, s, 0, j)),
                  pl.BlockSpec((2, None, B, n_lc, SLAB_STATE), lambda s, j: (0, s, 0, 0, 0))],
        out_specs=pl.BlockSpec((None, B, n_lc, nb), lambda s, j: (s, 0, 0, j)),
        out_shape=jax.ShapeDtypeStruct((S, B, n_lc, SLAB_IN), F32),
        compiler_params=_params(2),
        name="s5_out",
    )(ul, m, ccx, sin)


def _merge_body(y_ref, o_ref, g_ref, x_ref, mod_ref, gw_ref, gb_ref, wbs_ref, wba_ref, wo_ref, out_ref, y_scr):
    n_rows = y_ref.shape[1]
    for s in range(N_SLABS):
        for t in range(CHUNK):
            y_scr[s, pl.ds(t, n_rows, stride=CHUNK), :] = y_ref[s, :, t * SLAB:(t + 1) * SLAB]
    tm = x_ref.shape[0]
    halves = [slice(0, tm // 2), slice(tm // 2, tm)]
    ya = [_dot_cols(o_ref[r, :], wba_ref) for r in halves]
    hh = []
    for r in halves:
        y = jnp.concatenate([y_scr[s, r, :] for s in range(N_SLABS)], axis=1)
        hh.append(0.5 * y * (1.0 + jnp.tanh(math.sqrt(2.0 / math.pi) * (y + 0.044715 * (y * y * y)))))
    gl = [_dot_cols(v.astype(BF16), gw_ref) + gb_ref[...] for v in hh]
    ys = [_dot_cols((v * _sigmoid(u)).astype(BF16), wbs_ref) for v, u in zip(hh, gl)]
    for r, ys_r, ya_r in zip(halves, ys, ya):
        g = g_ref[r, :].astype(F32)
        mix = g[:, :D_MODEL] * ys_r + g[:, D_MODEL:] * ya_r
        out_ref[r, :] = x_ref[r, :] + mod_ref[2:3] * _dot_cols(mix.astype(BF16), wo_ref)


def _merge_call(y, o, g, x, mod3, gw, gb, wbs, wba, wo, tm):
    B, L, D = x.shape
    c2 = lambda b, i: (0, 0)
    return pl.pallas_call(
        _merge_body,
        grid=(B, L // tm),
        in_specs=[pl.BlockSpec((N_SLABS, None, tm // CHUNK, SLAB_IN), lambda b, i: (0, b, i, 0)),
                  pl.BlockSpec((None, tm, ATTN_W), lambda b, i: (b, i, 0)),
                  pl.BlockSpec((None, tm, 2 * D), lambda b, i: (b, i, 0)),
                  pl.BlockSpec((None, tm, D), lambda b, i: (b, i, 0)),
                  pl.BlockSpec((None, 6, D), lambda b, i: (b, 0, 0)),
                  pl.BlockSpec((S5_W, S5_W), c2), pl.BlockSpec((1, S5_W), c2),
                  pl.BlockSpec((S5_W, D), c2), pl.BlockSpec((ATTN_W, D), c2), pl.BlockSpec((D, D), c2)],
        out_specs=pl.BlockSpec((None, tm, D), lambda b, i: (b, i, 0)),
        out_shape=jax.ShapeDtypeStruct((B, L, D), F32),
        scratch_shapes=[pltpu.VMEM((N_SLABS, tm, SLAB), F32)],
        compiler_params=_params(2),
        name="merge",
    )(y, o, g, x, mod3, gw, gb, wbs, wba, wo)


def _ffn_body(xp_ref, x_ref, xn_ref, mod_ref, n2w_ref, wup_ref, cw_ref, cb_ref, wdn_ref, out_ref, *, tm):
    i = pl.program_id(1)
    last = pl.num_programs(1) - 1
    mod = mod_ref[...]
    n2w = n2w_ref[...]

    def modulated(v):
        ms = jnp.mean(v * v, axis=-1, keepdims=True)
        return (v * lax.rsqrt(ms + EPS)) * n2w * (1.0 + mod[4:5]) + mod[3:4]

    x = x_ref[...]
    hp = modulated(xp_ref[...]) * (i > 0).astype(F32)
    hn = modulated(xn_ref[...]) * (i < last).astype(F32)
    hcat = jnp.concatenate([hp, modulated(x), hn], axis=0).astype(BF16)
    rows = tm + 2 * HALO

    def conv(u, cw, cb):
        prev = pltpu.roll(u, 1, 0)[HALO:HALO + tm]
        nxt = pltpu.roll(u, rows - 1, 0)[HALO:HALO + tm]
        return cb + prev * cw[0:1] + u[HALO:HALO + tm] * cw[1:2] + nxt * cw[2:3]

    def col(ref, part, f):
        lo = part * D_FF + f * FF_CHUNK
        return ref[:, lo:lo + FF_CHUNK]

    ups, acts, acc = {}, {}, None
    for f in range(N_FF_CHUNKS + 2):
        if f < N_FF_CHUNKS:
            ups[f] = (_dot(hcat, col(wup_ref, 0, f)), _dot(hcat, col(wup_ref, 1, f)))
        e = f - 1
        if 0 <= e < N_FF_CHUNKS:
            ua, ug = ups.pop(e)
            ya = conv(ua, col(cw_ref, 0, e), col(cb_ref, 0, e))
            half_g = 0.5 * conv(ug, col(cw_ref, 1, e), col(cb_ref, 1, e))
            acts[e] = ((half_g + half_g * jnp.tanh(half_g)) * ya).astype(BF16)
        e = f - 2
        if 0 <= e < N_FF_CHUNKS and ((e + 1) % FF_DOWN_GROUP == 0 or e == N_FF_CHUNKS - 1):
            lo = e // FF_DOWN_GROUP * FF_DOWN_GROUP
            group = jnp.concatenate([acts.pop(j) for j in range(lo, e + 1)], axis=1)
            down = _dot(group, wdn_ref[lo * FF_CHUNK:(e + 1) * FF_CHUNK, :])
            acc = down if acc is None else acc + down
    out_ref[...] = x + mod[5:6] * acc


def _ffn_call(x1, mod3, n2w, wup, cw, cb, wdn, tm):
    B, L, D = x1.shape
    nh = tm // HALO
    n_halo_blocks = L // HALO
    c2 = lambda b, i: (0, 0)
    return pl.pallas_call(
        functools.partial(_ffn_body, tm=tm),
        grid=(B, L // tm),
        in_specs=[pl.BlockSpec((None, HALO, D), lambda b, i: (b, jnp.maximum(i * nh - 1, 0), 0)),
                  pl.BlockSpec((None, tm, D), lambda b, i: (b, i, 0)),
                  pl.BlockSpec((None, HALO, D), lambda b, i: (b, jnp.minimum((i + 1) * nh, n_halo_blocks - 1), 0)),
                  pl.BlockSpec((None, 6, D), lambda b, i: (b, 0, 0)),
                  pl.BlockSpec((1, D), c2),
                  pl.BlockSpec((D, 2 * D_FF), c2),
                  pl.BlockSpec((3, 2 * D_FF), c2),
                  pl.BlockSpec((1, 2 * D_FF), c2),
                  pl.BlockSpec((D_FF, D), c2)],
        out_specs=pl.BlockSpec((None, tm, D), lambda b, i: (b, i, 0)),
        out_shape=jax.ShapeDtypeStruct((B, L, D), F32),
        compiler_params=_params(2),
        name="ffn",
    )(x1, x1, x1, mod3, n2w, wup, cw, cb, wdn)


def _rope_tables(L):
    half = HEAD_DIM // 2
    inv_freq = ROPE_THETA ** (-jnp.arange(0, half, 2, dtype=F32) / half)
    lane = jnp.arange(PAIR)
    freq = inv_freq[lane % (half // 2)]
    sign = jnp.where(lane % half < half // 2, -1.0, 1.0)
    row_lane = (lane % HEAD_DIM < half)[None, None, :]
    ang_r = jnp.arange(L // GRID_W, dtype=F32)[:, None] * freq[None, :]
    ang_c = jnp.arange(GRID_W, dtype=F32)[:, None] * freq[None, :]
    table = lambda f: jnp.where(row_lane, f(ang_r)[:, None, :], f(ang_c)[None, :, :]).reshape(L, PAIR)
    return table(jnp.cos), table(lambda a: jnp.sin(a) * sign)


def kernel(x, c, ctx, c_ctx, ada_w, ada_b, norm1_w, w_in, b_gate, q_norm_w, k_norm_w, lam_q1, lam_k1, lam_q2, lam_k2, subln_w, s5_a_re, s5_a_im, s5_log_dt, s5_b_re, s5_b_im, s5_c_re, s5_c_im, s5_d, glu_w, glu_b, w_branch_s5, w_branch_attn, w_out, norm2_w, w_up, conv_w, conv_b, w_down):
    B, L, D = x.shape
    assert ada_w.shape[0] == 1 and D == D_MODEL and B + 1 <= 8

    cc = jnp.zeros((8, D), F32).at[:B].set(c).at[B].set(c_ctx)
    mod3 = _mod_call(cc, ada_w[0], ada_b).reshape(8, 6, D)

    cos_t, sin_t = _rope_tables(L)
    lane = jnp.arange(PAIR)
    seg = jnp.where((lane[:, None] // HEAD_DIM) == (lane[None, :] // HEAD_DIM), 1.0 / HEAD_DIM, 0.0).astype(BF16)
    qw = jnp.tile(q_norm_w[0], PAIR // HEAD_DIM)[None]
    kw = jnp.tile(k_norm_w[0], PAIR // HEAD_DIM)[None]
    w_in_b = w_in[0].astype(BF16)
    kc, vTc, uc = _inproj_call(ctx, mod3, lambda b: B, norm1_w, w_in_b, kw, seg)
    k, vT, ul, qT, g = _inproj_call(x, mod3, lambda b: b, norm1_w, w_in_b, kw, seg,
                                    latent_args=(qw, b_gate, cos_t, sin_t, IN_TILE))

    o, w_up_b, w_down_b = _attn_call(qT, kc, k, vTc, vT, lam_q1, lam_k1, lam_q2, lam_k2, subln_w,
                                      w_up[0], w_down[0], tq=512)

    m, bzx, ccx, a16x = _s5tab_call(s5_a_re[0], s5_a_im[0], s5_log_dt[0], s5_b_re[0], s5_b_im[0],
                                    s5_c_re[0], s5_c_im[0], s5_d[0])
    sin = _s5scan_call(uc, ul, bzx, a16x)
    y = _s5out_call(ul, m, ccx, sin, nb=512)

    x1 = _merge_call(y, o, g, x, mod3, glu_w[0].astype(BF16), glu_b, w_branch_s5[0].astype(BF16),
                     w_branch_attn[0].astype(BF16), w_out[0].astype(BF16), tm=512)

    return _ffn_call(x1, mod3, norm2_w, w_up_b, conv_w[0], conv_b, w_down_b, tm=512)
```

```python
import functools
import math

import jax
import jax.numpy as jnp
from jax import lax
from jax.experimental import pallas as pl
from jax.experimental.pallas import tpu as pltpu

D_MODEL = 1024
GRID_W = 64
N_HEADS = 8
HEAD_DIM = 64
V_DIM = 2 * HEAD_DIM
QK_W = N_HEADS * 2 * HEAD_DIM
ATTN_W = N_HEADS * V_DIM
S5_W = 512
S5_GROUP = 16
S5_GROUPS = S5_W // S5_GROUP
S5_STATE = 64
D_FF = 2816
ROPE_THETA = 10000.0
EPS = 1e-6
MIN_NEG_RE = -1e-4
K_OFF = QK_W
V_OFF = 2 * QK_W
U_OFF = 2 * QK_W + ATTN_W
G_OFF = U_OFF + S5_W
N_IN = G_OFF + 2 * D_MODEL
LAM_INIT = 0.8 - 0.6 * math.exp(-0.3 * 0)
Q_SCALE = HEAD_DIM ** -0.5 * math.log2(math.e)

LANES = 128
PAIR = 2 * LANES
TOK_TILE = 256
ATTN_COLS = 256
IN_TILE = 512
CHUNK = 16
SLAB = LANES
N_SLABS = S5_W // SLAB
GROUPS_PER_SLAB = SLAB // S5_GROUP
SLAB_IN = CHUNK * SLAB
SLAB_STATE = GROUPS_PER_SLAB * 2 * S5_STATE
FF_CHUNK = 256
N_FF_CHUNKS = D_FF // FF_CHUNK
FF_DOWN_GROUP = 4
HALO = 8
VMEM_LIMIT = 56 * 1024 * 1024

F32 = jnp.float32
BF16 = jnp.bfloat16


def _sigmoid(x):
    return 1.0 / (1.0 + jnp.exp(-x))


def _dot(a, b):
    return jnp.dot(a, b, preferred_element_type=F32)


def _dot_cols(a, b_ref, rows=slice(None)):
    n = b_ref.shape[-1]
    return jnp.concatenate([_dot(a, b_ref[rows, c:c + PAIR]) for c in range(0, n, PAIR)], axis=1)


def _params(n_parallel, n_arbitrary=0):
    return pltpu.CompilerParams(
        dimension_semantics=("parallel",) * n_parallel + ("arbitrary",) * n_arbitrary,
        vmem_limit_bytes=VMEM_LIMIT)


def _mod_body(c_ref, w_ref, b_ref, o_ref):
    c = c_ref[...]
    s = c * _sigmoid(c)
    o_ref[...] = jnp.dot(s, w_ref[...], preferred_element_type=F32,
                         precision=lax.Precision.HIGHEST) + b_ref[...]


def _mod_call(cc, ada_w, ada_b):
    n = ada_w.shape[1]
    return pl.pallas_call(
        _mod_body,
        grid=(n // D_MODEL,),
        in_specs=[pl.BlockSpec((8, D_MODEL), lambda j: (0, 0)),
                  pl.BlockSpec((D_MODEL, D_MODEL), lambda j: (0, j)),
                  pl.BlockSpec((1, D_MODEL), lambda j: (0, j))],
        out_specs=pl.BlockSpec((8, D_MODEL), lambda j: (0, j)),
        out_shape=jax.ShapeDtypeStruct((8, n), F32),
        compiler_params=_params(1),
        name="mod",
    )(cc, ada_w, ada_b)


def _inproj_body(*refs, latent):
    if latent:
        (x_ref, mod_ref, n1w_ref, w_ref, kw_ref, seg_ref, qw_ref, bg_ref, cos_ref, sin_ref,
         k_ref, vT_ref, u_ref, qT_ref, g_ref, u_scr) = refs
    else:
        x_ref, mod_ref, n1w_ref, w_ref, kw_ref, seg_ref, k_ref, vT_ref, u_ref, u_scr = refs
    T = x_ref.shape[0]
    xt = x_ref[...]
    ms = jnp.mean(xt * xt, axis=-1, keepdims=True)
    mod = mod_ref[...]
    h = (xt * lax.rsqrt(ms + EPS)) * n1w_ref[...] * (1.0 + mod[1:2]) + mod[0:1]
    hb = h.astype(BF16)
    seg = seg_ref[...]
    if latent:
        cos = cos_ref[...]
        sin = sin_ref[...]
        first_half = (lax.broadcasted_iota(jnp.int32, cos.shape, 1) % 32) < 16

    def norm_rope(z, w):
        y = z * lax.rsqrt(_dot((z * z).astype(BF16), seg) + EPS) * w
        if not latent:
            return y
        partner = jnp.where(first_half, pltpu.roll(y, PAIR - 16, 1), pltpu.roll(y, 16, 1))
        return y * cos + partner * sin

    def put_k(c, z):
        y = norm_rope(z, kw_ref[...]).astype(BF16)
        k_ref[2 * c] = y[:, :LANES]
        k_ref[2 * c + 1] = y[:, LANES:]

    def put_v(c, z):
        vT_ref[2 * c] = z[:, :LANES].T.astype(BF16)
        vT_ref[2 * c + 1] = z[:, LANES:].T.astype(BF16)

    def put_u(c, z):
        u_scr[2 * c] = z[:, :SLAB]
        u_scr[2 * c + 1] = z[:, SLAB:]
        if c == S5_W // PAIR - 1:
            for s in range(N_SLABS):
                for t in range(CHUNK):
                    piece = u_scr[s, pl.ds(t, T // CHUNK, stride=CHUNK), :]
                    u_ref[s, :, t * SLAB:(t + 1) * SLAB] = piece.astype(BF16)

    def put_q(c, z):
        y = norm_rope(z, qw_ref[...]) * Q_SCALE
        row = lax.broadcasted_iota(jnp.int32, (LANES, T), 0)
        for hh, qT in ((2 * c, y[:, :LANES].T), (2 * c + 1, y[:, LANES:].T)):
            qT_ref[hh, 0] = jnp.where(row < HEAD_DIM, qT, 0.0).astype(BF16)
            qT_ref[hh, 1] = jnp.where(row >= HEAD_DIM, qT, 0.0).astype(BF16)

    def put_g(c, z):
        lo = c * PAIR
        g_ref[:, lo:lo + PAIR] = _sigmoid(z + bg_ref[:, lo:lo + PAIR]).astype(BF16)

    sections = [(K_OFF, QK_W, put_k), (V_OFF, ATTN_W, put_v), (U_OFF, S5_W, put_u)]
    if latent:
        sections += [(0, QK_W, put_q), (G_OFF, 2 * D_MODEL, put_g)]
    tasks = [(off + c * PAIR, c, put) for off, width, put in sections for c in range(width // PAIR)]
    project = lambda lo: _dot(hb, w_ref[:, lo:lo + PAIR])
    z_next = project(tasks[0][0])
    for n, (_, c, put) in enumerate(tasks):
        z = z_next
        if n + 1 < len(tasks):
            z_next = project(tasks[n + 1][0])
        put(c, z)


def _inproj_call(x, mod3, mod_row, n1w, w_in_b, kw, seg, latent_args=None):
    B, n, D = x.shape
    latent = latent_args is not None
    T = latent_args[4] if latent else n
    assert n % T == 0
    const2 = lambda b, i: (0, 0)
    in_specs = [
        pl.BlockSpec((None, T, D), lambda b, i: (b, i, 0)),
        pl.BlockSpec((None, 6, D), lambda b, i: (mod_row(b), 0, 0)),
        pl.BlockSpec((1, D), const2),
        pl.BlockSpec((D, N_IN), const2),
        pl.BlockSpec((1, PAIR), const2),
        pl.BlockSpec((PAIR, PAIR), const2),
    ]
    out_specs = [
        pl.BlockSpec((None, N_HEADS, T, LANES), lambda b, i: (b, 0, i, 0)),
        pl.BlockSpec((None, N_HEADS, V_DIM, T), lambda b, i: (b, 0, 0, i)),
        pl.BlockSpec((N_SLABS, None, T // CHUNK, SLAB_IN), lambda b, i: (0, b, i, 0)),
    ]
    out_shape = [
        jax.ShapeDtypeStruct((B, N_HEADS, n, LANES), BF16),
        jax.ShapeDtypeStruct((B, N_HEADS, V_DIM, n), BF16),
        jax.ShapeDtypeStruct((N_SLABS, B, n // CHUNK, SLAB_IN), BF16),
    ]
    args = [x, mod3, n1w, w_in_b, kw, seg]
    if latent:
        qw, bg, cos_t, sin_t, _ = latent_args
        in_specs += [pl.BlockSpec((1, PAIR), const2), pl.BlockSpec((1, 2 * D), const2),
                     pl.BlockSpec((T, PAIR), lambda b, i: (i, 0)), pl.BlockSpec((T, PAIR), lambda b, i: (i, 0))]
        out_specs += [pl.BlockSpec((None, N_HEADS, 2, LANES, T), lambda b, i: (b, 0, 0, 0, i)),
                      pl.BlockSpec((None, T, 2 * D), lambda b, i: (b, i, 0))]
        out_shape += [jax.ShapeDtypeStruct((B, N_HEADS, 2, LANES, n), BF16),
                      jax.ShapeDtypeStruct((B, n, 2 * D), BF16)]
        args += [qw, bg, cos_t, sin_t]
    return pl.pallas_call(
        functools.partial(_inproj_body, latent=latent),
        grid=(B, n // T),
        in_specs=in_specs,
        out_specs=out_specs,
        out_shape=out_shape,
        scratch_shapes=[pltpu.VMEM((N_SLABS, T, SLAB), F32)],
        compiler_params=_params(2),
        name="in_proj_latent" if latent else "in_proj_ctx",
    )(*args)


def _attn_body(qT_ref, kc_ref, k_ref, vTc_ref, vT_ref, lq1_ref, lk1_ref, lq2_ref, lk2_ref, sw_ref, o_ref,
               s0_ref, s1_ref, m0_ref, m1_ref, acc_ref, l_ref, *, tq):
    g = pl.program_id(0)
    tk = TOK_TILE
    n_ctx = kc_ref.shape[0]
    n_all = n_ctx + k_ref.shape[0]
    assert n_ctx == tk

    @pl.when(g == 0)
    def _():
        s1_ref[...] = jnp.zeros(s1_ref.shape, F32)
        m1_ref[...] = jnp.zeros(m1_ref.shape, F32)
        acc_ref[...] = jnp.zeros(acc_ref.shape, F32)
        l_ref[...] = jnp.ones(l_ref.shape, F32)

    def step(s_w, m_w, s_r, m_r):
        oT = acc_ref[...] / jnp.sum(l_ref[...], axis=0, keepdims=True)
        lam = (jnp.exp(jnp.sum(lq1_ref[...] * lk1_ref[...], axis=-1, keepdims=True))
               - jnp.exp(jnp.sum(lq2_ref[...] * lk2_ref[...], axis=-1, keepdims=True)) + LAM_INIT)
        od = (oT[:, :tq] - lam * oT[:, tq:]).T
        ms = jnp.mean(od * od, axis=-1, keepdims=True)
        o_ref[...] = ((od * lax.rsqrt(ms + EPS)) * sw_ref[...] * (1.0 - LAM_INIT)).astype(BF16)

        qT = jnp.concatenate([qT_ref[0], qT_ref[1]], axis=1)
        m_prev = m_r[...]
        n_parts = 2 * tq // ATTN_COLS
        m8, l8, acc = [None] * n_parts, [None] * n_parts, [None] * n_parts
        for lo in range(0, n_all, tk):
            k_tile = kc_ref[...] if lo == 0 else k_ref[lo - n_ctx:lo - n_ctx + tk, :]
            v_tile = vTc_ref[...] if lo == 0 else vT_ref[:, lo - n_ctx:lo - n_ctx + tk]
            for h in range(n_parts):
                cols = slice(h * ATTN_COLS, (h + 1) * ATTN_COLS)
                s_new = _dot(k_tile, qT[:, cols])
                s_w[lo:lo + tk, cols] = s_new
                part = jnp.max(s_new.reshape(tk // 8, 8, ATTN_COLS), axis=0)
                m8[h] = part if m8[h] is None else jnp.maximum(m8[h], part)

                p = jnp.exp2(s_r[lo:lo + tk, cols] - m_prev[:, cols])
                part = jnp.sum(p.reshape(tk // 8, 8, ATTN_COLS), axis=0)
                l8[h] = part if l8[h] is None else l8[h] + part
                pv = _dot(v_tile, p.astype(BF16))
                acc[h] = pv if acc[h] is None else acc[h] + pv
        m_w[...] = jnp.max(jnp.concatenate(m8, axis=1), axis=0, keepdims=True)
        acc_ref[...] = jnp.concatenate(acc, axis=1)
        l_ref[...] = jnp.concatenate(l8, axis=1)

    @pl.when(g % 2 == 0)
    def _():
        step(s0_ref, m0_ref, s1_ref, m1_ref)

    @pl.when(g % 2 == 1)
    def _():
        step(s1_ref, m1_ref, s0_ref, m0_ref)


def _attn_call(qT, kc, k, vTc, vT, lq1, lk1, lq2, lk2, sw, tq):
    B, H, _, _, L = qT.shape
    n_ctx = kc.shape[2]
    n_all = n_ctx + L
    nq = L // tq
    n_tiles = B * H * nq

    def tile(t):
        return t // (H * nq), (t // nq) % H, t % nq

    def scored(g):
        return tile(jnp.minimum(g, n_tiles - 1))

    def finished(g):
        return tile(jnp.clip(g - 1, 0, n_tiles - 1))

    def written(g):
        return tile(jnp.maximum(g - 2, 0))

    def q_map(g):
        b, h, i = scored(g)
        return b, h, 0, 0, i

    def k_map(g):
        b, h, _ = scored(g)
        return b, h, 0, 0

    def v_map(g):
        b, h, _ = finished(g)
        return b, h, 0, 0

    def o_map(g):
        b, h, i = written(g)
        return b, i, h

    vec = pl.BlockSpec((1, HEAD_DIM), lambda g: (0, 0))
    return pl.pallas_call(
        functools.partial(_attn_body, tq=tq),
        grid=(n_tiles + 2,),
        in_specs=[
            pl.BlockSpec((None, None, 2, LANES, tq), q_map),
            pl.BlockSpec((None, None, n_ctx, LANES), k_map),
            pl.BlockSpec((None, None, L, LANES), k_map),
            pl.BlockSpec((None, None, V_DIM, n_ctx), v_map),
            pl.BlockSpec((None, None, V_DIM, L), v_map),
            vec, vec, vec, vec,
            pl.BlockSpec((1, V_DIM), lambda g: (0, 0)),
        ],
        out_specs=pl.BlockSpec((None, tq, V_DIM), o_map),
        out_shape=jax.ShapeDtypeStruct((B, L, ATTN_W), BF16),
        scratch_shapes=[pltpu.VMEM((n_all, 2 * tq), F32), pltpu.VMEM((n_all, 2 * tq), F32),
                        pltpu.VMEM((1, 2 * tq), F32), pltpu.VMEM((1, 2 * tq), F32),
                        pltpu.VMEM((V_DIM, 2 * tq), F32), pltpu.VMEM((8, 2 * tq), F32)],
        compiler_params=_params(0, 1),
        name="attn",
    )(qT, kc, k, vTc, vT, lq1, lk1, lq2, lk2, sw)


def _cmul(xr, xi, yr, yi):
    return xr * yr - xi * yi, xr * yi + xi * yr


def _a_bar(are, aim, ldt):
    lr = jnp.minimum(are, MIN_NEG_RE)
    dt = jnp.exp(ldt)
    mag = jnp.exp(lr * dt)
    return lr, aim, mag * jnp.cos(aim * dt), mag * jnp.sin(aim * dt)


def _powers(ar, ai, n):
    pows = [(jnp.ones_like(ar), jnp.zeros_like(ar))]
    for _ in range(n):
        pows.append(_cmul(pows[-1][0], pows[-1][1], ar, ai))
    return pows


def _s5tab_body(aar_ref, aai_ref, adt_ref, abr_ref, abi_ref, bar_ref, bai_ref, bdt_ref, bcr_ref, bci_ref,
                car_ref, cai_ref, cdt_ref, dsk_ref, m_ref, bz_ref, cc_ref, a16_ref, lag_ref):
    d = pl.program_id(1)
    fwd = d == 0
    half = SLAB_STATE // 2
    sel = lambda a, b: (jnp.where(fwd, a[0], b[0]), jnp.where(fwd, a[1], b[1]))

    lr, li, ar, ai = _a_bar(aar_ref[...], aai_ref[...], adt_ref[...])
    den = lr * lr + li * li
    nr = ar - 1.0
    cfr = (nr * lr + ai * li) / den
    cfi = (ai * lr - nr * li) / den
    bbr, bbi = _cmul(cfr, cfi, abr_ref[...], abi_ref[...])
    pa = _powers(ar, ai, CHUNK - 1)
    row_a = lax.broadcasted_iota(jnp.int32, (SLAB, half), 0) // S5_GROUP
    col_a = lax.broadcasted_iota(jnp.int32, (SLAB, half), 1) // S5_STATE
    diag_a = row_a == col_a

    def expand_a(v):
        return jnp.where(diag_a, jnp.concatenate([v] * (half // LANES), axis=1), 0.0)

    for t in range(CHUNK):
        pw = sel(pa[CHUNK - 1 - t], pa[t])
        zr, zi = _cmul(pw[0], pw[1], bbr, bbi)
        bz_ref[t * SLAB:(t + 1) * SLAB, :] = jnp.concatenate([expand_a(zr), expand_a(zi)], axis=1).astype(BF16)

    _, _, br_, bi_ = _a_bar(bar_ref[...], bai_ref[...], bdt_ref[...])
    pb = _powers(br_, bi_, CHUNK)
    cr = bcr_ref[...]
    ci = bci_ref[...]
    row_b = lax.broadcasted_iota(jnp.int32, (half, SLAB), 0) // S5_STATE
    col_b = lax.broadcasted_iota(jnp.int32, (half, SLAB), 1) // S5_GROUP
    diag_b = row_b == col_b

    def expand_b(v):
        return jnp.where(diag_b, jnp.concatenate([v] * GROUPS_PER_SLAB, axis=0), 0.0)

    for t in range(CHUNK):
        pw = sel(pb[t + 1], pb[CHUNK - t])
        ctr, cti = _cmul(cr, ci, pw[0], pw[1])
        cc_ref[:, t * SLAB:(t + 1) * SLAB] = jnp.concatenate([expand_b(ctr), expand_b(-cti)], axis=0).astype(BF16)

    lane = lax.broadcasted_iota(jnp.int32, (SLAB, LANES), 1)
    lhs = jnp.where(lane < S5_STATE, bbr, bbi)
    r_k = lax.broadcasted_iota(jnp.int32, (SLAB, SLAB), 0)
    c_k = lax.broadcasted_iota(jnp.int32, (SLAB, SLAB), 1)
    diag_k = (r_k // S5_GROUP) == (c_k // S5_GROUP)
    sign = 1 - 2 * d
    for tau in range(CHUNK):
        ctr, cti = _cmul(cr, ci, pb[tau][0], pb[tau][1])
        rhs = jnp.concatenate([ctr, -cti], axis=0)
        ktau = jnp.where(diag_k, jnp.dot(lhs, rhs, preferred_element_type=F32,
                                         precision=lax.Precision.HIGHEST), 0.0)
        if tau == 0:
            @pl.when(fwd)
            def _():
                lag_ref[CHUNK - 1] = ktau + jnp.where(r_k == c_k, dsk_ref[...], 0.0)

            @pl.when(jnp.logical_not(fwd))
            def _():
                lag_ref[CHUNK - 1] = lag_ref[CHUNK - 1] + ktau
        else:
            lag_ref[CHUNK - 1 + sign * tau] = ktau

    @pl.when(jnp.logical_not(fwd))
    def _():
        lags = [lag_ref[n].astype(BF16) for n in range(2 * CHUNK - 1)]
        for tp in range(CHUNK):
            m_ref[tp * SLAB:(tp + 1) * SLAB, :] = jnp.concatenate(
                [lags[t - tp + CHUNK - 1] for t in range(CHUNK)], axis=1)

    _, _, xr, xi = _a_bar(car_ref[...], cai_ref[...], cdt_ref[...])
    for _ in range(4):
        xr, xi = _cmul(xr, xi, xr, xi)
    a16_ref[...] = jnp.concatenate([xr, xi], axis=1)


def _s5tab_call(s5_a_re, s5_a_im, s5_log_dt, s5_b_re, s5_b_im, s5_c_re, s5_c_im, s5_d):
    S, GS, C, P = N_SLABS, GROUPS_PER_SLAB, S5_GROUP, S5_STATE
    half = SLAB_STATE // 2

    def lay_a(a):
        return jnp.broadcast_to(a.reshape(2, S, GS, 1, 1, P), (2, S, GS, C, 2, P)).reshape(2, S, SLAB, LANES)

    def lay_b(a):
        a = a.reshape(2, S, GS, P).transpose(0, 1, 3, 2)
        return jnp.broadcast_to(a[..., None], (2, S, P, GS, C)).reshape(2, S, P, SLAB)

    ldt = jnp.broadcast_to(s5_log_dt[..., None], (2, S5_GROUPS, P))
    b_a = lambda b: jnp.broadcast_to(b.transpose(0, 1, 3, 2).reshape(2, S, GS, C, 1, P),
                                     (2, S, GS, C, 2, P)).reshape(2, S, SLAB, LANES)
    c_b = lambda c: c.reshape(2, S, GS, C, P).transpose(0, 1, 4, 2, 3).reshape(2, S, P, SLAB)
    lay_c = lambda a: a.reshape(2, S, 1, half)
    blk = lambda r, c: pl.BlockSpec((None, None, r, c), lambda s, d: (d, s, 0, 0))
    return pl.pallas_call(
        _s5tab_body,
        grid=(S, 2),
        in_specs=[blk(SLAB, LANES)] * 5 + [blk(P, SLAB)] * 5 + [blk(1, half)] * 3
                 + [pl.BlockSpec((None, 1, SLAB), lambda s, d: (s, 0, 0))],
        out_specs=[pl.BlockSpec((None, SLAB_IN, SLAB_IN), lambda s, d: (s, 0, 0)),
                   blk(SLAB_IN, SLAB_STATE), blk(SLAB_STATE, SLAB_IN), blk(1, SLAB_STATE)],
        out_shape=[jax.ShapeDtypeStruct((S, SLAB_IN, SLAB_IN), BF16),
                   jax.ShapeDtypeStruct((2, S, SLAB_IN, SLAB_STATE), BF16),
                   jax.ShapeDtypeStruct((2, S, SLAB_STATE, SLAB_IN), BF16),
                   jax.ShapeDtypeStruct((2, S, 1, SLAB_STATE), F32)],
        scratch_shapes=[pltpu.VMEM((2 * CHUNK - 1, SLAB, SLAB), F32)],
        compiler_params=_params(0, 2),
        name="s5_tab",
    )(lay_a(s5_a_re), lay_a(s5_a_im), lay_a(ldt), b_a(s5_b_re), b_a(s5_b_im),
      lay_b(s5_a_re), lay_b(s5_a_im), lay_b(ldt), c_b(s5_c_re), c_b(s5_c_im),
      lay_c(s5_a_re), lay_c(s5_a_im), lay_c(ldt), s5_d.reshape(S, 1, SLAB))


def _s5scan_body(uc_ref, ul_ref, bz_ref, a_ref, sin_ref, zc_ref, zl_ref, st_ref, *, n_cc, n_lc):
    d = pl.program_id(1)
    B = uc_ref.shape[0]
    n_lt = SLAB_STATE // LANES

    def chunk_inputs(u_ref, n, z_ref):
        z = _dot_cols(u_ref[...].reshape(B * n, SLAB_IN), bz_ref)
        for b in range(B):
            for t in range(n_lt):
                z_ref[t, pl.ds(b, n, stride=B), :] = z[b * n:(b + 1) * n, t * LANES:(t + 1) * LANES]

    chunk_inputs(uc_ref, n_cc, zc_ref)
    chunk_inputs(ul_ref, n_lc, zl_ref)
    a = a_ref[...]
    half = n_lt // 2
    a_re = [a[:, (t % half) * LANES:(t % half + 1) * LANES] for t in range(n_lt)]
    a_im = [a[:, (t % half + half) * LANES:(t % half + half + 1) * LANES] for t in range(n_lt)]

    def advance(s, z_ref, rows):
        out = []
        for t in range(n_lt):
            other = s[(t + half) % n_lt]
            cross = -(a_im[t] * other) if t < half else a_im[t] * other
            out.append(a_re[t] * s[t] + cross + z_ref[t, rows, :])
        return tuple(out)

    def ctx_step(i, s):
        kk = jnp.where(d == 0, i, n_cc - 1 - i)
        return advance(s, zc_ref, pl.ds(pl.multiple_of(kk * B, B), B))

    def lat_step(i, s):
        kk = jnp.where(d == 0, i, n_lc - 1 - i)
        rows = pl.ds(pl.multiple_of(kk * B, B), B)
        for t in range(n_lt):
            st_ref[t, rows, :] = s[t]
        return advance(s, zl_ref, rows)

    s = lax.fori_loop(0, n_cc, ctx_step, tuple(jnp.zeros((B, LANES), F32) for _ in range(n_lt)))
    lax.fori_loop(0, n_lc, lat_step, s)
    for b in range(B):
        for t in range(n_lt):
            sin_ref[b, :, t * LANES:(t + 1) * LANES] = st_ref[t, pl.ds(b, n_lc, stride=B), :].astype(BF16)


def _s5scan_call(uc, ul, bzx, a16x):
    S, B, n_cc, _ = uc.shape
    n_lc = ul.shape[2]
    n_lt = SLAB_STATE // LANES
    return pl.pallas_call(
        functools.partial(_s5scan_body, n_cc=n_cc, n_lc=n_lc),
        grid=(S, 2),
        in_specs=[pl.BlockSpec((None, B, n_cc, SLAB_IN), lambda s, d: (s, 0, 0, 0)),
                  pl.BlockSpec((None, B, n_lc, SLAB_IN), lambda s, d: (s, 0, 0, 0)),
                  pl.BlockSpec((None, None, SLAB_IN, SLAB_STATE), lambda s, d: (d, s, 0, 0)),
                  pl.BlockSpec((None, None, 1, SLAB_STATE), lambda s, d: (d, s, 0, 0))],
        out_specs=pl.BlockSpec((None, None, B, n_lc, SLAB_STATE), lambda s, d: (d, s, 0, 0, 0)),
        out_shape=jax.ShapeDtypeStruct((2, S, B, n_lc, SLAB_STATE), BF16),
        scratch_shapes=[pltpu.VMEM((n_lt, B * n_cc, LANES), F32),
                        pltpu.VMEM((n_lt, B * n_lc, LANES), F32),
                        pltpu.VMEM((n_lt, B * n_lc, LANES), F32)],
        compiler_params=_params(2),
        name="s5_scan",
    )(uc, ul, bzx, a16x)


def _s5out_body(ul_ref, m_ref, cc_ref, sin_ref, y_ref):
    B, n_lc, _ = ul_ref.shape
    rows = B * n_lc
    y = _dot_cols(ul_ref[...].reshape(rows, SLAB_IN), m_ref)
    y = y + _dot_cols(sin_ref[0].reshape(rows, SLAB_STATE), cc_ref.at[0])
    y = y + _dot_cols(sin_ref[1].reshape(rows, SLAB_STATE), cc_ref.at[1])
    y_ref[...] = y.reshape(B, n_lc, y.shape[-1])


def _s5out_call(ul, m, ccx, sin, nb):
    S, B, n_lc, _ = ul.shape
    return pl.pallas_call(
        _s5out_body,
        grid=(S, SLAB_IN // nb),
        in_specs=[pl.BlockSpec((None, B, n_lc, SLAB_IN), lambda s, j: (s, 0, 0, 0)),
                  pl.BlockSpec((None, SLAB_IN, nb), lambda s, j: (s, 0, j)),
                  pl.BlockSpec((2, None, SLAB_STATE, nb), lambda s, j: (0, s, 0, j)),
                  pl.BlockSpec((2, None, B, n_lc, SLAB_STATE), lambda s, j: (0, s, 0, 0, 0))],
        out_specs=pl.BlockSpec((None, B, n_lc, nb), lambda s, j: (s, 0, 0, j)),
        out_shape=jax.ShapeDtypeStruct((S, B, n_lc, SLAB_IN), F32),
        compiler_params=_params(2),
        name="s5_out",
    )(ul, m, ccx, sin)


def _merge_body(y_ref, o_ref, g_ref, x_ref, mod_ref, gw_ref, gb_ref, wbs_ref, wba_ref, wo_ref, out_ref, y_scr):
    n_rows = y_ref.shape[1]
    for s in range(N_SLABS):
        for t in range(CHUNK):
            y_scr[s, pl.ds(t, n_rows, stride=CHUNK), :] = y_ref[s, :, t * SLAB:(t + 1) * SLAB]
    tm = x_ref.shape[0]
    halves = [slice(0, tm // 2), slice(tm // 2, tm)]
    ya = [_dot_cols(o_ref[r, :], wba_ref) for r in halves]
    hh = []
    for r in halves:
        y = jnp.concatenate([y_scr[s, r, :] for s in range(N_SLABS)], axis=1)
        hh.append(0.5 * y * (1.0 + jnp.tanh(math.sqrt(2.0 / math.pi) * (y + 0.044715 * (y * y * y)))))
    gl = [_dot_cols(v.astype(BF16), gw_ref) + gb_ref[...] for v in hh]
    ys = [_dot_cols((v * _sigmoid(u)).astype(BF16), wbs_ref) for v, u in zip(hh, gl)]
    for r, ys_r, ya_r in zip(halves, ys, ya):
        g = g_ref[r, :].astype(F32)
        mix = g[:, :D_MODEL] * ys_r + g[:, D_MODEL:] * ya_r
        out_ref[r, :] = x_ref[r, :] + mod_ref[2:3] * _dot_cols(mix.astype(BF16), wo_ref)


def _merge_call(y, o, g, x, mod3, gw, gb, wbs, wba, wo, tm):
    B, L, D = x.shape
    c2 = lambda b, i: (0, 0)
    return pl.pallas_call(
        _merge_body,
        grid=(B, L // tm),
        in_specs=[pl.BlockSpec((N_SLABS, None, tm // CHUNK, SLAB_IN), lambda b, i: (0, b, i, 0)),
                  pl.BlockSpec((None, tm, ATTN_W), lambda b, i: (b, i, 0)),
                  pl.BlockSpec((None, tm, 2 * D), lambda b, i: (b, i, 0)),
                  pl.BlockSpec((None, tm, D), lambda b, i: (b, i, 0)),
                  pl.BlockSpec((None, 6, D), lambda b, i: (b, 0, 0)),
                  pl.BlockSpec((S5_W, S5_W), c2), pl.BlockSpec((1, S5_W), c2),
                  pl.BlockSpec((S5_W, D), c2), pl.BlockSpec((ATTN_W, D), c2), pl.BlockSpec((D, D), c2)],
        out_specs=pl.BlockSpec((None, tm, D), lambda b, i: (b, i, 0)),
        out_shape=jax.ShapeDtypeStruct((B, L, D), F32),
        scratch_shapes=[pltpu.VMEM((N_SLABS, tm, SLAB), F32)],
        compiler_params=_params(2),
        name="merge",
    )(y, o, g, x, mod3, gw, gb, wbs, wba, wo)


def _ffn_body(xp_ref, x_ref, xn_ref, mod_ref, n2w_ref, wup_ref, cw_ref, cb_ref, wdn_ref, out_ref, *, tm):
    i = pl.program_id(1)
    last = pl.num_programs(1) - 1
    mod = mod_ref[...]
    n2w = n2w_ref[...]

    def modulated(v):
        ms = jnp.mean(v * v, axis=-1, keepdims=True)
        return (v * lax.rsqrt(ms + EPS)) * n2w * (1.0 + mod[4:5]) + mod[3:4]

    x = x_ref[...]
    hp = modulated(xp_ref[...]) * (i > 0).astype(F32)
    hn = modulated(xn_ref[...]) * (i < last).astype(F32)
    hcat = jnp.concatenate([hp, modulated(x), hn], axis=0).astype(BF16)
    rows = tm + 2 * HALO

    def conv(u, cw, cb):
        prev = pltpu.roll(u, 1, 0)[HALO:HALO + tm]
        nxt = pltpu.roll(u, rows - 1, 0)[HALO:HALO + tm]
        return cb + prev * cw[0:1] + u[HALO:HALO + tm] * cw[1:2] + nxt * cw[2:3]

    def col(ref, part, f):
        lo = part * D_FF + f * FF_CHUNK
        return ref[:, lo:lo + FF_CHUNK]

    ups, acts, acc = {}, {}, None
    for f in range(N_FF_CHUNKS + 2):
        if f < N_FF_CHUNKS:
            ups[f] = (_dot(hcat, col(wup_ref, 0, f)), _dot(hcat, col(wup_ref, 1, f)))
        e = f - 1
        if 0 <= e < N_FF_CHUNKS:
            ua, ug = ups.pop(e)
            ya = conv(ua, col(cw_ref, 0, e), col(cb_ref, 0, e))
            half_g = 0.5 * conv(ug, col(cw_ref, 1, e), col(cb_ref, 1, e))
            acts[e] = ((half_g + half_g * jnp.tanh(half_g)) * ya).astype(BF16)
        e = f - 2
        if 0 <= e < N_FF_CHUNKS and ((e + 1) % FF_DOWN_GROUP == 0 or e == N_FF_CHUNKS - 1):
            lo = e // FF_DOWN_GROUP * FF_DOWN_GROUP
            group = jnp.concatenate([acts.pop(j) for j in range(lo, e + 1)], axis=1)
            down = _dot(group, wdn_ref[lo * FF_CHUNK:(e + 1) * FF_CHUNK, :])
            acc = down if acc is None else acc + down
    out_ref[...] = x + mod[5:6] * acc


def _ffn_call(x1, mod3, n2w, wup, cw, cb, wdn, tm):
    B, L, D = x1.shape
    nh = tm // HALO
    n_halo_blocks = L // HALO
    c2 = lambda b, i: (0, 0)
    return pl.pallas_call(
        functools.partial(_ffn_body, tm=tm),
        grid=(B, L // tm),
        in_specs=[pl.BlockSpec((None, HALO, D), lambda b, i: (b, jnp.maximum(i * nh - 1, 0), 0)),
                  pl.BlockSpec((None, tm, D), lambda b, i: (b, i, 0)),
                  pl.BlockSpec((None, HALO, D), lambda b, i: (b, jnp.minimum((i + 1) * nh, n_halo_blocks - 1), 0)),
                  pl.BlockSpec((None, 6, D), lambda b, i: (b, 0, 0)),
                  pl.BlockSpec((1, D), c2),
                  pl.BlockSpec((D, 2 * D_FF), c2),
                  pl.BlockSpec((3, 2 * D_FF), c2),
                  pl.BlockSpec((1, 2 * D_FF), c2),
                  pl.BlockSpec((D_FF, D), c2)],
        out_specs=pl.BlockSpec((None, tm, D), lambda b, i: (b, i, 0)),
        out_shape=jax.ShapeDtypeStruct((B, L, D), F32),
        compiler_params=_params(2),
        name="ffn",
    )(x1, x1, x1, mod3, n2w, wup, cw, cb, wdn)


def _rope_tables(L):
    half = HEAD_DIM // 2
    inv_freq = ROPE_THETA ** (-jnp.arange(0, half, 2, dtype=F32) / half)
    lane = jnp.arange(PAIR)
    freq = inv_freq[lane % (half // 2)]
    sign = jnp.where(lane % half < half // 2, -1.0, 1.0)
    row_lane = (lane % HEAD_DIM < half)[None, None, :]
    ang_r = jnp.arange(L // GRID_W, dtype=F32)[:, None] * freq[None, :]
    ang_c = jnp.arange(GRID_W, dtype=F32)[:, None] * freq[None, :]
    table = lambda f: jnp.where(row_lane, f(ang_r)[:, None, :], f(ang_c)[None, :, :]).reshape(L, PAIR)
    return table(jnp.cos), table(lambda a: jnp.sin(a) * sign)


def kernel(x, c, ctx, c_ctx, ada_w, ada_b, norm1_w, w_in, b_gate, q_norm_w, k_norm_w, lam_q1, lam_k1, lam_q2, lam_k2, subln_w, s5_a_re, s5_a_im, s5_log_dt, s5_b_re, s5_b_im, s5_c_re, s5_c_im, s5_d, glu_w, glu_b, w_branch_s5, w_branch_attn, w_out, norm2_w, w_up, conv_w, conv_b, w_down):
    B, L, D = x.shape
    assert ada_w.shape[0] == 1 and D == D_MODEL and B + 1 <= 8

    cc = jnp.zeros((8, D), F32).at[:B].set(c).at[B].set(c_ctx)
    mod3 = _mod_call(cc, ada_w[0], ada_b).reshape(8, 6, D)

    cos_t, sin_t = _rope_tables(L)
    lane = jnp.arange(PAIR)
    seg = jnp.where((lane[:, None] // HEAD_DIM) == (lane[None, :] // HEAD_DIM), 1.0 / HEAD_DIM, 0.0).astype(BF16)
    qw = jnp.tile(q_norm_w[0], PAIR // HEAD_DIM)[None]
    kw = jnp.tile(k_norm_w[0], PAIR // HEAD_DIM)[None]
    w_in_b = w_in[0].astype(BF16)
    kc, vTc, uc = _inproj_call(ctx, mod3, lambda b: B, norm1_w, w_in_b, kw, seg)
    k, vT, ul, qT, g = _inproj_call(x, mod3, lambda b: b, norm1_w, w_in_b, kw, seg,
                                    latent_args=(qw, b_gate, cos_t, sin_t, IN_TILE))

    o = _attn_call(qT, kc, k, vTc, vT, lam_q1, lam_k1, lam_q2, lam_k2, subln_w, tq=512)

    m, bzx, ccx, a16x = _s5tab_call(s5_a_re[0], s5_a_im[0], s5_log_dt[0], s5_b_re[0], s5_b_im[0],
                                    s5_c_re[0], s5_c_im[0], s5_d[0])
    sin = _s5scan_call(uc, ul, bzx, a16x)
    y = _s5out_call(ul, m, ccx, sin, nb=512)

    x1 = _merge_call(y, o, g, x, mod3, glu_w[0].astype(BF16), glu_b, w_branch_s5[0].astype(BF16),
                     w_branch_attn[0].astype(BF16), w_out[0].astype(BF16), tm=512)

    return _ffn_call(x1, mod3, norm2_w, w_up[0].astype(BF16), conv_w[0], conv_b, w_down[0].astype(BF16), tm=512)
```
